```python
import jax, jax.numpy as jnp
from jax import lax
import numpy as np

D_MODEL = 2048
BATCH = 2
SEQ = 4096
DEPTH = 2

GRID_W = 64
CTX_LEN = 256
BRANCH = 1024
N_BRANCH = 3
EPS = 1e-6

RW_HEADS = 16
RW_HD = 64
RW_LORA = 64
RW_GN_EPS = 64e-5
RW_SHIFT = 3 * BRANCH + 4 * RW_LORA

AT_HEADS = 8
AT_KV = 2
AT_HD = 128
Q_BLOCK = 128
ROPE_THETA = 10000.0

ML_HEADS = 4
ML_DK = 128
ML_DV = 256
ML_CHUNK = 128
GATE_CAP = 15.0

IN_SIZES = (RW_SHIFT, BRANCH,
            AT_HEADS * AT_HD, AT_KV * AT_HD, AT_KV * AT_HD, BRANCH,
            ML_HEADS * ML_DK, ML_HEADS * ML_DK, ML_HEADS * ML_DV, BRANCH, 4 * ML_HEADS, BRANCH,
            N_BRANCH * D_MODEL)
D_IN = sum(IN_SIZES)

kernel_name = "hybrid_rwkv7_gqa_mlstm_prefix_dit_block"


def rms_norm(x, g):
    xf = x.astype(jnp.float32)
    y = xf * lax.rsqrt(jnp.mean(xf * xf, axis=-1, keepdims=True) + EPS)
    return (y * g.astype(jnp.float32)).astype(x.dtype)


def segment_neighbours(u, n_ctx):
    L = u.shape[1]
    pos = jnp.arange(L)
    zero = jnp.zeros((), u.dtype)
    prev = jnp.pad(u[:, :-1], ((0, 0), (1, 0), (0, 0)))
    nxt = jnp.pad(u[:, 1:], ((0, 0), (0, 1), (0, 0)))
    prev = jnp.where((pos == n_ctx)[None, :, None], zero, prev)
    nxt = jnp.where((pos == n_ctx - 1)[None, :, None], zero, nxt)
    return prev, nxt


def axial_rope(x, cos, sin):
    shp = x.shape
    q = AT_HD // 4
    xs = x.reshape(*shp[:-1], 2, 2, q)
    x1, x2 = xs[..., 0, :], xs[..., 1, :]
    expand = (1,) * (x.ndim - 3)
    c = cos.reshape(cos.shape[0], *expand, 2, q).astype(x.dtype)
    s = sin.reshape(sin.shape[0], *expand, 2, q).astype(x.dtype)
    return jnp.stack([x1 * c - x2 * s, x1 * s + x2 * c], axis=-2).reshape(shp)


def rwkv7_scan(r, w, a, b, k, v):
    Bsz, L, H, N = r.shape

    def step(S, inp):
        r_t, w_t, a_t, b_t, k_t, v_t = inp
        sa = jnp.einsum('bhvk,bhk->bhv', S, a_t)
        S = S * w_t[:, :, None, :] + sa[..., None] * b_t[:, :, None, :] + v_t[..., None] * k_t[:, :, None, :]
        return S, jnp.einsum('bhvk,bhk->bhv', S, r_t)

    S0 = jnp.zeros((Bsz, H, N, N), jnp.float32)
    xs = tuple(jnp.swapaxes(t, 0, 1) for t in (r, w, a, b, k, v))
    _, y = lax.scan(step, S0, xs)
    return jnp.swapaxes(y, 0, 1)


def rwkv7_branch(xr, xk, xv, xwd, xad, w_up, w0, a_up, a0, k_k, k_a, r_k, ln_w, ln_b, bwd):
    f32 = jnp.float32
    Bsz, L, _ = xr.shape
    heads = lambda t: t.astype(f32).reshape(Bsz, L, RW_HEADS, RW_HD)
    r, k, v = heads(xr), heads(xk), heads(xv)
    kk = k * k_k.astype(f32).reshape(RW_HEADS, RW_HD)
    kk = kk / jnp.maximum(jnp.linalg.norm(kk, axis=-1, keepdims=True), 1e-12)
    k_a = k_a.astype(f32).reshape(RW_HEADS, RW_HD)
    outs = []
    for d, order in ((0, None), (1, bwd)):
        wd = xwd[..., d * RW_LORA:(d + 1) * RW_LORA].astype(f32)
        ad = xad[..., d * RW_LORA:(d + 1) * RW_LORA].astype(f32)
        w_log = -jax.nn.softplus(-(w0[d].astype(f32) + jnp.tanh(wd) @ w_up[d].astype(f32))) - 0.5
        decay = heads(jnp.exp(-jnp.exp(w_log)))
        a = heads(jax.nn.sigmoid(a0[d].astype(f32) + ad @ a_up[d].astype(f32)))
        kd = k * (1.0 + (a - 1.0) * k_a)
        seqs = (r, decay, -kk, kk * a, kd, v)
        if order is not None:
            seqs = tuple(jnp.take(t, order, axis=1) for t in seqs)
        y_d = rwkv7_scan(*seqs)
        if order is not None:
            y_d = jnp.take(y_d, order, axis=1)
        outs.append(y_d)
    y = outs[0] + outs[1]
    mu = jnp.mean(y, axis=-1, keepdims=True)
    var = jnp.mean(jnp.square(y - mu), axis=-1, keepdims=True)
    y = (y - mu) * lax.rsqrt(var + RW_GN_EPS) * ln_w.astype(f32).reshape(RW_HEADS, RW_HD) \
        + ln_b.astype(f32).reshape(RW_HEADS, RW_HD)
    bonus = jnp.sum(r * k * r_k.astype(f32), axis=-1, keepdims=True) * v
    return (y + bonus).reshape(Bsz, L, BRANCH)


def gqa_branch(q, k, v, q_g, k_g, cos, sin, n_ctx):
    Bsz, L, _ = q.shape
    grp = AT_HEADS // AT_KV
    q = rms_norm(q.reshape(Bsz, L, AT_KV, grp, AT_HD), q_g)
    k = rms_norm(k.reshape(Bsz, L, AT_KV, AT_HD), k_g)
    v = v.reshape(Bsz, L, AT_KV, AT_HD)
    q = axial_rope(q, cos, sin) * (AT_HD ** -0.5)
    k = axial_rope(k, cos, sin)

    def attend(qb, keys, vals):
        s = jnp.einsum('bqgrd,bkgd->bgrqk', qb, keys).astype(jnp.float32)
        p = jax.nn.softmax(s, axis=-1).astype(vals.dtype)
        return jnp.einsum('bgrqk,bkgd->bqgrd', p, vals)

    out_c = attend(q[:, :n_ctx], k[:, :n_ctx], v[:, :n_ctx]).reshape(Bsz, n_ctx, BRANCH)
    n_lat = L - n_ctx
    nb = n_lat // Q_BLOCK
    ql = q[:, n_ctx:].reshape(Bsz, nb, Q_BLOCK, AT_KV, grp, AT_HD)
    ql = jnp.moveaxis(ql, 1, 0)
    out_l = lax.map(lambda blk: attend(blk, k, v), ql)
    out_l = jnp.moveaxis(out_l, 0, 1).reshape(Bsz, n_lat, BRANCH)
    return jnp.concatenate([out_c, out_l], axis=1)


def mlstm_chunk_scan(q, k, v, logi, logf):
    Bsz, H, L, _ = q.shape
    nc = L // ML_CHUNK
    chunks = lambda t: jnp.moveaxis(t.reshape(Bsz, H, nc, ML_CHUNK, *t.shape[3:]), 2, 0)
    tri = jnp.tril(jnp.ones((ML_CHUNK, ML_CHUNK), bool))

    def step(carry, inp):
        C, n, m = carry
        qc, kc, vc, li, lf = inp
        b = jnp.cumsum(lf, axis=-1)
        g = b[..., -1]
        dmat = jnp.where(tri, b[..., :, None] - b[..., None, :] + li[..., None, :], -jnp.inf)
        m_inter = b + m[..., None]
        m_t = jnp.maximum(m_inter, jnp.max(dmat, axis=-1))
        w_inter = jnp.exp(m_inter - m_t)
        s = jnp.einsum('bhtd,bhsd->bhts', qc, kc) * jnp.exp(dmat - m_t[..., None])
        num = w_inter[..., None] * jnp.einsum('bhtd,bhdv->bhtv', qc, C) + jnp.einsum('bhts,bhsv->bhtv', s, vc)
        den = w_inter * jnp.einsum('bhtd,bhd->bht', qc, n) + jnp.sum(s, axis=-1)
        h = num / jnp.maximum(jnp.abs(den), jnp.exp(-m_t))[..., None]
        loga = g[..., None] - b + li
        m_new = jnp.maximum(g + m, jnp.max(loga, axis=-1))
        carry_scale = jnp.exp(g + m - m_new)
        wa = jnp.exp(loga - m_new[..., None])
        C = carry_scale[..., None, None] * C + jnp.einsum('bhs,bhsd,bhsv->bhdv', wa, kc, vc)
        n = carry_scale[..., None] * n + jnp.einsum('bhs,bhsd->bhd', wa, kc)
        return (C, n, m_new), h

    init = (jnp.zeros((Bsz, H, ML_DK, ML_DV), jnp.float32),
            jnp.zeros((Bsz, H, ML_DK), jnp.float32),
            jnp.zeros((Bsz, H), jnp.float32))
    _, h = lax.scan(step, init, tuple(chunks(t) for t in (q, k, v, logi, logf)))
    return jnp.moveaxis(h, 0, 2).reshape(Bsz, H, L, ML_DV)


def mlstm_branch(q, k, v, o, gate_pre, gate_b, norm_g, bwd):
    f32 = jnp.float32
    Bsz, L, _ = q.shape
    to_heads = lambda t, dh: jnp.swapaxes(t.astype(f32).reshape(Bsz, L, ML_HEADS, dh), 1, 2)
    q = to_heads(q, ML_DK) * (ML_DK ** -0.5)
    k = to_heads(k, ML_DK)
    v = to_heads(v, ML_DV)
    pre = gate_pre.astype(f32).reshape(Bsz, L, 4, ML_HEADS) + gate_b.astype(f32)
    pre = GATE_CAP * jnp.tanh(pre / GATE_CAP)
    pre = jnp.transpose(pre, (0, 2, 3, 1))
    logi = pre[:, 0:2]
    logf = jax.nn.log_sigmoid(pre[:, 2:4])
    h_f = mlstm_chunk_scan(q, k, v, logi[:, 0], logf[:, 0])
    flip = lambda t: jnp.take(t, bwd, axis=2)
    h_b = flip(mlstm_chunk_scan(flip(q), flip(k), flip(v), flip(logi[:, 1]), flip(logf[:, 1])))
    h = jnp.swapaxes(h_f + h_b, 1, 2)
    h = rms_norm(h, norm_g.reshape(ML_HEADS, ML_DV)).reshape(Bsz, L, BRANCH)
    return jax.nn.sigmoid(o.astype(f32)) * h


def hybrid_layer(z, mod_c, mod_l, norm_g, w_in, shift_mu, rw_w_up, rw_w0, rw_a_up, rw_a0,
                 rw_k_k, rw_k_a, rw_r_k, rw_ln_w, rw_ln_b, at_q_g, at_k_g, ml_gate_b, ml_norm_g,
                 w_branch, w_out, cos, sin, bwd, n_ctx):
    Bsz, L, _ = z.shape
    sh_c, sc_c, gt_c = jnp.split(mod_c, 3, axis=-1)
    sh_l, sc_l, gt_l = jnp.split(mod_l, 3, axis=-1)
    z_c, z_l = z[:, :n_ctx], z[:, n_ctx:]
    h = jnp.concatenate([rms_norm(z_c, norm_g) * (1 + sc_c) + sh_c,
                         rms_norm(z_l, norm_g) * (1 + sc_l[:, None]) + sh_l[:, None]], axis=1)
    proj = h @ w_in
    split_idx = tuple(int(i) for i in np.cumsum(IN_SIZES)[:-1])
    (rw_s, rw_g, at_q, at_k, at_v, at_g, ml_q, ml_k, ml_v, ml_o, ml_if, ml_g, merge) = \
        jnp.split(proj, split_idx, axis=-1)

    prev, nxt = segment_neighbours(rw_s, n_ctx)
    rw_s = rw_s + shift_mu[0] * (prev - rw_s) + shift_mu[1] * (nxt - rw_s)
    xr, xk, xv, xwd, xad = jnp.split(rw_s, (BRANCH, 2 * BRANCH, 3 * BRANCH, 3 * BRANCH + 2 * RW_LORA), axis=-1)
    y_a = rwkv7_branch(xr, xk, xv, xwd, xad, rw_w_up, rw_w0, rw_a_up, rw_a0, rw_k_k, rw_k_a,
                       rw_r_k, rw_ln_w, rw_ln_b, bwd).astype(z.dtype) * jax.nn.silu(rw_g)
    y_b = gqa_branch(at_q, at_k, at_v, at_q_g, at_k_g, cos, sin, n_ctx) * jax.nn.silu(at_g)
    y_c = mlstm_branch(ml_q, ml_k, ml_v, ml_o, ml_if, ml_gate_b, ml_norm_g, bwd).astype(z.dtype) * jax.nn.silu(ml_g)

    ys = jnp.stack([y_a, y_b, y_c], axis=2)
    branch = jnp.einsum('blnc,ncd->blnd', ys, w_branch)
    gates = jax.nn.sigmoid(merge).reshape(Bsz, L, N_BRANCH, D_MODEL)
    out = jnp.sum(gates * branch, axis=2) @ w_out
    return jnp.concatenate([z_c + gt_c * out[:, :n_ctx],
                            z_l + gt_l[:, None] * out[:, n_ctx:]], axis=1)


def setup_inputs(seed: int = 0) -> dict:
    key = jax.random.key(seed)
    ks = iter(jax.random.split(key, 40))
    f32 = jnp.float32
    nrm = lambda shape, s: jax.random.normal(next(ks), shape, f32) * s
    D = D_MODEL
    x = nrm((BATCH, SEQ, D), 1.0)
    c = nrm((BATCH, D), 1.0)
    ctx = nrm((BATCH, CTX_LEN, D), 1.0)
    c_ctx = nrm((D,), 1.0)
    norm_g = 1.0 + nrm((DEPTH, D), 0.02)
    w_ada = nrm((DEPTH, D, 3 * D), 0.5 * D ** -0.5)
    b_ada = nrm((DEPTH, 3 * D), 0.01)
    w_in = nrm((DEPTH, D, D_IN), D ** -0.5)
    shift_mu = jax.random.uniform(next(ks), (DEPTH, 2, RW_SHIFT), f32, 0.1, 0.5)
    rw_w_up = nrm((DEPTH, 2, RW_LORA, BRANCH), 0.1 * RW_LORA ** -0.5)
    rw_w0 = jnp.linspace(-6.0, -1.0, BRANCH, dtype=f32)[None, None] + nrm((DEPTH, 2, BRANCH), 0.1)
    rw_a_up = nrm((DEPTH, 2, RW_LORA, BRANCH), 0.1 * RW_LORA ** -0.5)
    rw_a0 = nrm((DEPTH, 2, BRANCH), 0.1)
    rw_k_k = 0.85 + nrm((DEPTH, BRANCH), 0.02)
    rw_k_a = 1.0 + nrm((DEPTH, BRANCH), 0.02)
    rw_r_k = nrm((DEPTH, RW_HEADS, RW_HD), 0.1)
    rw_ln_w = 1.0 + nrm((DEPTH, BRANCH), 0.02)
    rw_ln_b = nrm((DEPTH, BRANCH), 0.01)
    at_q_g = 1.0 + nrm((DEPTH, AT_HD), 0.02)
    at_k_g = 1.0 + nrm((DEPTH, AT_HD), 0.02)
    ml_gate_b = jnp.concatenate(
        [nrm((DEPTH, 2, ML_HEADS), 0.1),
         jnp.linspace(3.0, 6.0, ML_HEADS, dtype=f32)[None, None] + nrm((DEPTH, 2, ML_HEADS), 0.1)], axis=1)
    ml_norm_g = 1.0 + nrm((DEPTH, BRANCH), 0.02)
    w_branch = nrm((DEPTH, N_BRANCH, BRANCH, D), BRANCH ** -0.5)
    w_out = nrm((DEPTH, D, D), D ** -0.5)
    final_g = 1.0 + nrm((D,), 0.02)
    return {"x": x, "c": c, "ctx": ctx, "c_ctx": c_ctx, "norm_g": norm_g, "w_ada": w_ada,
            "b_ada": b_ada, "w_in": w_in, "shift_mu": shift_mu, "rw_w_up": rw_w_up, "rw_w0": rw_w0,
            "rw_a_up": rw_a_up, "rw_a0": rw_a0, "rw_k_k": rw_k_k, "rw_k_a": rw_k_a, "rw_r_k": rw_r_k,
            "rw_ln_w": rw_ln_w, "rw_ln_b": rw_ln_b, "at_q_g": at_q_g, "at_k_g": at_k_g,
            "ml_gate_b": ml_gate_b, "ml_norm_g": ml_norm_g, "w_branch": w_branch, "w_out": w_out,
            "final_g": final_g}


def reference(x, c, ctx, c_ctx, norm_g, w_ada, b_ada, w_in, shift_mu, rw_w_up, rw_w0, rw_a_up,
              rw_a0, rw_k_k, rw_k_a, rw_r_k, rw_ln_w, rw_ln_b, at_q_g, at_k_g, ml_gate_b, ml_norm_g,
              w_branch, w_out, final_g):
    f32 = jnp.float32
    n_ctx = ctx.shape[1]
    n_lat = x.shape[1]
    rows = n_lat // GRID_W
    row = jnp.repeat(jnp.arange(rows), GRID_W).astype(f32)
    col = jnp.tile(jnp.arange(GRID_W), rows).astype(f32)
    inv_freq = ROPE_THETA ** (-jnp.arange(0, AT_HD // 2, 2, dtype=f32) / (AT_HD // 2))
    ang_lat = jnp.stack([row[:, None] * inv_freq, col[:, None] * inv_freq], axis=1)
    ang = jnp.concatenate([jnp.zeros((n_ctx, 2, AT_HD // 4), f32), ang_lat], axis=0)
    cos, sin = jnp.cos(ang), jnp.sin(ang)
    bwd = jnp.concatenate([jnp.arange(n_ctx)[::-1], n_ctx + jnp.arange(n_lat)[::-1]])

    z = jnp.concatenate([ctx, x], axis=1)
    for l in range(DEPTH):
        mod_l = jax.nn.silu(c) @ w_ada[l] + b_ada[l]
        mod_c = jax.nn.silu(c_ctx) @ w_ada[l] + b_ada[l]
        z = hybrid_layer(z, mod_c, mod_l, norm_g[l], w_in[l], shift_mu[l], rw_w_up[l], rw_w0[l],
                         rw_a_up[l], rw_a0[l], rw_k_k[l], rw_k_a[l], rw_r_k[l], rw_ln_w[l], rw_ln_b[l],
                         at_q_g[l], at_k_g[l], ml_gate_b[l], ml_norm_g[l], w_branch[l], w_out[l],
                         cos, sin, bwd, n_ctx)
    return rms_norm(z[:, n_ctx:], final_g)
```

```python
import functools
import math

import numpy as np
import jax
import jax.numpy as jnp
from jax import lax
from jax.experimental import pallas as pl
from jax.experimental.pallas import tpu as pltpu

F32 = jnp.float32
BF16 = jnp.bfloat16
HI = lax.Precision.HIGHEST

D = 2048
BR = 1024
EPS = 1e-6
GRID_W = 64

RW_H, RW_N, RW_LORA = 16, 64, 64
RW_GN_EPS = 64e-5
RW_T = 64

AT_H, AT_KV, AT_HD = 8, 2, 128
ROPE_THETA = 10000.0

ML_H, ML_DK, ML_DV, ML_T = 4, 128, 256, 128
GATE_CAP = 15.0

_PERM = (
    ("merge", 11024, 6144), ("r", 0, 1024), ("k", 1024, 1024), ("v", 2048, 1024),
    ("rw_g", 3328, 1024), ("at_q", 4352, 1024), ("at_g", 5888, 1024), ("ml_v", 7936, 1024),
    ("ml_o", 8960, 1024), ("ml_g", 10000, 1024), ("ml_q", 6912, 512), ("ml_k", 7424, 512),
    ("wdad", 3072, 256), ("at_k", 5376, 256), ("at_v", 5632, 256), ("ml_if", 9984, 16),
)
N_PROJ = 17408
COL = {}
_off = 0
for _name, _start, _width in _PERM:
    COL[_name] = _off
    _off += _width
assert _off == 17168 and COL["ml_if"] % 128 == 0

VMEM_LIMIT = 48 * 1024 * 1024


def _cparams(sem):
    return pltpu.CompilerParams(dimension_semantics=sem, vmem_limit_bytes=VMEM_LIMIT)


def _dot(a, b, **kw):
    return jnp.dot(a, b, preferred_element_type=F32, **kw)


def _dot_nt(a, b, **kw):
    return lax.dot_general(a, b, (((1,), (1,)), ((), ())), preferred_element_type=F32, **kw)


def _dot_tn(a, b, **kw):
    return lax.dot_general(a, b, (((0,), (0,)), ((), ())), preferred_element_type=F32, **kw)


def _split3(x):
    h = x.astype(BF16)
    r = x - h.astype(F32)
    m = r.astype(BF16)
    l = (r - m.astype(F32)).astype(BF16)
    return h, m, l


def _dot_const_rhs(x, c):
    h, m, l = _split3(x)
    return _dot(h, c) + _dot(m, c) + _dot(l, c)


def _dot_const_lhs(c, x):
    h, m, l = _split3(x)
    return _dot(c, h) + _dot(c, m) + _dot(c, l)


def _silu(x):
    return x * jax.nn.sigmoid(x)


def _mod_kernel(c_ref, w_ref, b_ref, o_ref):
    x = _silu(c_ref[...])
    w = w_ref[0]
    xh = x.astype(BF16)
    xl = (x - xh.astype(F32)).astype(BF16)
    wh = w.astype(BF16)
    wl = (w - wh.astype(F32)).astype(BF16)
    o_ref[0] = _dot(xh, wh) + _dot(xh, wl) + _dot(xl, wh) + b_ref[0]


def _modulation(cc, w_ada, b_ada):
    depth = w_ada.shape[0]
    tn = 512
    return pl.pallas_call(
        _mod_kernel,
        grid=(depth, 3 * D // tn),
        in_specs=[pl.BlockSpec((8, D), lambda l, j: (0, 0)),
                  pl.BlockSpec((1, D, tn), lambda l, j: (l, 0, j)),
                  pl.BlockSpec((1, 1, tn), lambda l, j: (l, 0, j))],
        out_specs=pl.BlockSpec((1, 8, tn), lambda l, j: (l, 0, j)),
        out_shape=jax.ShapeDtypeStruct((depth, 8, 3 * D), F32),
        compiler_params=_cparams(("parallel", "parallel")),
        name="adaln_modulation",
    )(cc, w_ada, b_ada.reshape(depth, 1, 3 * D))


def _norm_kernel(z_ref, sc_ref, sh_ref, o_ref):
    x = z_ref[0]
    y = x * lax.rsqrt(jnp.mean(x * x, axis=-1, keepdims=True) + EPS)
    o_ref[0] = (y * sc_ref[0, 0] + sh_ref[0, 0]).astype(o_ref.dtype)


def _norm_mod(z, scale, shift, n_ctx, out_dtype, tl=256):
    bsz, L, _ = z.shape
    nct = n_ctx // tl
    seg = lambda b, i: (b, (i >= nct).astype(jnp.int32), 0, 0)
    return pl.pallas_call(
        _norm_kernel,
        grid=(bsz, L // tl),
        in_specs=[pl.BlockSpec((1, tl, D), lambda b, i: (b, i, 0)),
                  pl.BlockSpec((1, 1, 1, D), seg),
                  pl.BlockSpec((1, 1, 1, D), seg)],
        out_specs=pl.BlockSpec((1, tl, D), lambda b, i: (b, i, 0)),
        out_shape=jax.ShapeDtypeStruct((bsz, L, D), out_dtype),
        compiler_params=_cparams(("parallel", "parallel")),
        name="rmsnorm_modulate",
    )(z, scale, shift)


def _mm_kernel(a_ref, w_ref, o_ref):
    o_ref[...] = _dot(a_ref[...], w_ref[...]).astype(o_ref.dtype)


def _matmul(a, w, out_dtype=F32, tm=512, tn=1024):
    m, k = a.shape
    n = w.shape[1]
    return pl.pallas_call(
        _mm_kernel,
        grid=(n // tn, m // tm),
        in_specs=[pl.BlockSpec((tm, k), lambda j, i: (i, 0)),
                  pl.BlockSpec((k, tn), lambda j, i: (0, j))],
        out_specs=pl.BlockSpec((tm, tn), lambda j, i: (i, j)),
        out_shape=jax.ShapeDtypeStruct((m, n), out_dtype),
        compiler_params=_cparams(("parallel", "parallel")),
        name="in_projection",
    )(a, w)


def _rwprep_kernel(r_ref, rp_ref, rn_ref, k_ref, kp_ref, kn_ref, v_ref, vp_ref, vn_ref,
                   w_ref, wp_ref, wn_ref, mur_ref, muk_ref, muv_ref, muw_ref,
                   wup_ref, w0_ref, aup_ref, a0_ref, kk_ref, ka_ref, rk_ref, bd_ref, tri_ref,
                   at_ref, rt_ref, bt_ref, kt_ref, vo_ref, bonus_ref, g_ref, *, tl, n_ctx, L):
    i = pl.program_id(1)
    start = i * tl
    has_prev = jnp.logical_and(start != 0, start != n_ctx).astype(F32)
    has_next = jnp.logical_and(start + tl != n_ctx, start + tl != L).astype(F32)

    def shift(x_ref, p_ref, n_ref, mu_ref):
        x = x_ref[0]
        row = lax.broadcasted_iota(jnp.int32, x.shape, 0)
        prev = jnp.where(row == 0, p_ref[0, 7:8, :] * has_prev, pltpu.roll(x, 1, axis=0))
        nxt = jnp.where(row == tl - 1, n_ref[0, 0:1, :] * has_next, pltpu.roll(x, tl - 1, axis=0))
        return x + mu_ref[0:1, :] * (prev - x) + mu_ref[1:2, :] * (nxt - x)

    r = shift(r_ref, rp_ref, rn_ref, mur_ref)
    k = shift(k_ref, kp_ref, kn_ref, muk_ref)
    v = shift(v_ref, vp_ref, vn_ref, muv_ref)
    wdad = shift(w_ref, wp_ref, wn_ref, muw_ref)
    bd = bd_ref[...]

    kk = k * kk_ref[...]
    kk = kk / jnp.maximum(jnp.sqrt(_dot_const_rhs(kk * kk, bd)), 1e-12)
    bonus_ref[0] = _dot_const_rhs(r * k * rk_ref[...], bd) * v
    vo_ref[0] = v

    nchunk = tl // RW_T
    for d in range(2):
        wd = wdad[:, d * RW_LORA:(d + 1) * RW_LORA]
        ad = wdad[:, 2 * RW_LORA + d * RW_LORA:2 * RW_LORA + (d + 1) * RW_LORA]
        u = w0_ref[d:d + 1, :] + _dot(jnp.tanh(wd), wup_ref[d], precision=HI)
        logw = -math.exp(-0.5) * jax.nn.sigmoid(u)
        a = jax.nn.sigmoid(a0_ref[d:d + 1, :] + _dot(ad, aup_ref[d], precision=HI))
        kd = k * (1.0 + (a - 1.0) * ka_ref[...])
        b = kk * a
        for c in range(nchunk):
            sl = slice(c * RW_T, (c + 1) * RW_T)
            lw = logw[sl]
            lg = _dot_const_lhs(tri_ref[d], lw)
            e_pos = jnp.exp(lg)
            e_neg = jnp.exp(-lg)
            at_ref[0, 0 + d, sl, :] = -kk[sl] * jnp.exp(lg - lw)
            rt_ref[0, 0 + d, sl, :] = r[sl] * e_pos
            bt_ref[0, 0 + d, sl, :] = b[sl] * e_neg
            kt_ref[0, 0 + d, sl, :] = kd[sl] * e_neg
            g_ref[0, d, c] = jnp.exp(jnp.sum(lw, axis=0, keepdims=True))


def _rw_prepare(proj, p, n_ctx, tl=256):
    bsz, L, _ = proj.shape
    h8 = tl // 8
    nblk8 = L // 8
    main = lambda w, cb: pl.BlockSpec((1, tl, w), lambda b, i: (b, i, cb))
    prev = lambda w, cb: pl.BlockSpec((1, 8, w), lambda b, i: (b, jnp.maximum(i * h8 - 1, 0), cb))
    nxt = lambda w, cb: pl.BlockSpec((1, 8, w), lambda b, i: (b, jnp.minimum((i + 1) * h8, nblk8 - 1), cb))
    full = lambda shape: pl.BlockSpec(shape, lambda b, i: (0,) * len(shape))
    in_specs = []
    args = []
    for name, w in (("r", 1024), ("k", 1024), ("v", 1024), ("wdad", 256)):
        cb = COL[name] // w
        in_specs += [main(w, cb), prev(w, cb), nxt(w, cb)]
        args += [proj, proj, proj]
    mu = p["shift_mu"]
    in_specs += [pl.BlockSpec((2, 1024), lambda b, i: (0, 0)), pl.BlockSpec((2, 1024), lambda b, i: (0, 1)),
                 pl.BlockSpec((2, 1024), lambda b, i: (0, 2)), pl.BlockSpec((2, 256), lambda b, i: (0, 12))]
    args += [mu, mu, mu, mu]
    in_specs += [full((2, RW_LORA, BR)), full((2, BR)), full((2, RW_LORA, BR)), full((2, BR)),
                 full((1, BR)), full((1, BR)), full((1, BR)), full((BR, BR)), full((2, RW_T, RW_T))]
    args += [p["rw_w_up"], p["rw_w0"], p["rw_a_up"], p["rw_a0"], p["rw_k_k"].reshape(1, BR),
             p["rw_k_a"].reshape(1, BR), p["rw_r_k"].reshape(1, BR), p["bd64"], p["tri"]]
    dir_spec = pl.BlockSpec((1, 2, tl, BR), lambda b, i: (b, 0, i, 0))
    tok_spec = pl.BlockSpec((1, tl, BR), lambda b, i: (b, i, 0))
    nchunk = tl // RW_T
    out_specs = [dir_spec, dir_spec, dir_spec, dir_spec, tok_spec, tok_spec,
                 pl.BlockSpec((1, 2, nchunk, 1, BR), lambda b, i: (b, 0, i, 0, 0))]
    dir_shape = jax.ShapeDtypeStruct((bsz, 2, L, BR), F32)
    tok_shape = jax.ShapeDtypeStruct((bsz, L, BR), F32)
    out_shape = [dir_shape, dir_shape, dir_shape, dir_shape, tok_shape, tok_shape,
                 jax.ShapeDtypeStruct((bsz, 2, L // RW_T, 1, BR), F32)]
    return pl.pallas_call(
        functools.partial(_rwprep_kernel, tl=tl, n_ctx=n_ctx, L=L),
        grid=(bsz, L // tl),
        in_specs=in_specs, out_specs=out_specs, out_shape=out_shape,
        compiler_params=_cparams(("parallel", "parallel")),
        name="rwkv_prepare",
    )(*args)


def _scan_chunk(d, j, nc_ctx, nc):
    bwd = jnp.where(j < nc_ctx, nc_ctx - 1 - j, nc - 1 - j + nc_ctx)
    return jnp.where(d == 0, j, bwd)


def _rwscan_kernel(at_ref, rt_ref, bt_ref, kt_ref, v_ref, g_ref, y_ref, s_ref):
    d = pl.program_id(1)
    j = pl.program_id(3)
    T = RW_T

    @pl.when(j == 0)
    def _():
        s_ref[...] = jnp.zeros_like(s_ref)

    lane = lax.broadcasted_iota(jnp.int32, (2 * T, 128), 1)
    row = lax.broadcasted_iota(jnp.int32, (2 * T, 128), 0)
    own = (lane // RW_N) == (row // T)

    def stack(x):
        return jnp.where(own, jnp.concatenate([x, x], axis=0), 0.0)

    At, Rt = stack(at_ref[0, 0]), stack(rt_ref[0, 0])
    Bt, Kt = stack(bt_ref[0, 0]), stack(kt_ref[0, 0])
    V = stack(v_ref[0])
    AR = jnp.concatenate([At, Rt], axis=0)
    BK = jnp.concatenate([Bt, Kt], axis=0)
    sc = _dot_nt(AR, BK, precision=HI)

    r2 = lax.broadcasted_iota(jnp.int32, (2 * T, 2 * T), 0)
    c2 = lax.broadcasted_iota(jnp.int32, (2 * T, 2 * T), 1)
    same = (r2 // T) == (c2 // T)
    order = (r2 - c2) * (1 - 2 * d)
    strict = jnp.logical_and(same, order > 0)
    incl = jnp.logical_and(same, order >= 0)
    Aab = jnp.where(strict, sc[:2 * T, :2 * T], 0.0)
    Aak = jnp.where(strict, sc[:2 * T, 2 * T:], 0.0)
    Mrb = jnp.where(incl, sc[2 * T:, :2 * T], 0.0)
    Mrk = jnp.where(incl, sc[2 * T:, 2 * T:], 0.0)

    N = Aab
    P = jnp.where(r2 == c2, 1.0, 0.0) + N
    for _ in range(int(math.log2(T)) - 1):
        N = _dot(N, N, precision=HI)
        P = P + _dot(P, N, precision=HI)

    S = s_ref[...]
    XS = _dot_nt(AR, S, precision=HI)
    W1 = XS[:2 * T] + _dot(Aak, V, precision=HI)
    U = _dot(P, W1, precision=HI)
    UV = jnp.concatenate([U, V], axis=0)
    Ys = XS[2 * T:] + _dot(jnp.concatenate([Mrb, Mrk], axis=1), UV, precision=HI)
    y_ref[0, 0] = Ys[:T] + Ys[T:]
    s_ref[...] = (S + _dot_tn(UV, BK, precision=HI)) * g_ref[0, 0, 0]


def _rw_scan(at, rt, bt, kt, v, g, n_ctx):
    bsz, _, L, _ = at.shape
    nc, nc_ctx = L // RW_T, n_ctx // RW_T
    chunk = lambda d, j: _scan_chunk(d, j, nc_ctx, nc)
    dspec = pl.BlockSpec((1, 1, RW_T, 128), lambda b, d, p, j: (b, d, chunk(d, j), p))
    return pl.pallas_call(
        _rwscan_kernel,
        grid=(bsz, 2, BR // 128, nc),
        in_specs=[dspec, dspec, dspec, dspec,
                  pl.BlockSpec((1, RW_T, 128), lambda b, d, p, j: (b, chunk(d, j), p)),
                  pl.BlockSpec((1, 1, 1, 1, 128), lambda b, d, p, j: (b, d, chunk(d, j), 0, p))],
        out_specs=dspec,
        out_shape=jax.ShapeDtypeStruct((bsz, 2, L, BR), F32),
        scratch_shapes=[pltpu.VMEM((128, 128), F32)],
        compiler_params=_cparams(("parallel", "parallel", "parallel", "arbitrary")),
        name="rwkv_scan",
    )(at, rt, bt, kt, v, g)


def _atprep_kernel(q_ref, k_ref, v_ref, cos_ref, sin_ref, qg_ref, kg_ref, qo_ref, ko_ref, vo_ref):
    cos = cos_ref[...]
    sin = sin_ref[...]
    lane = lax.broadcasted_iota(jnp.int32, cos.shape, 1)
    first_half = (lane % 64) < 32

    def norm_rope(x, g):
        y = x * lax.rsqrt(jnp.mean(x * x, axis=-1, keepdims=True) + EPS) * g
        partner = jnp.where(first_half, pltpu.roll(y, 96, axis=1), pltpu.roll(y, 32, axis=1))
        return y * cos + partner * sin

    for h in range(AT_H):
        sl = slice(h * AT_HD, (h + 1) * AT_HD)
        qo_ref[0, :, sl] = (norm_rope(q_ref[0, :, sl], qg_ref[...]) * (AT_HD ** -0.5)).astype(qo_ref.dtype)
    for h in range(AT_KV):
        sl = slice(h * AT_HD, (h + 1) * AT_HD)
        ko_ref[0, :, sl] = norm_rope(k_ref[0, :, sl], kg_ref[...]).astype(ko_ref.dtype)
    vo_ref[0] = v_ref[0].astype(vo_ref.dtype)


def _at_prepare(proj, cos, sin, q_g, k_g, tl=256):
    bsz, L, _ = proj.shape
    kvw = AT_KV * AT_HD
    tok = lambda w: pl.BlockSpec((1, tl, w), lambda b, i: (b, i, 0))
    return pl.pallas_call(
        _atprep_kernel,
        grid=(bsz, L // tl),
        in_specs=[pl.BlockSpec((1, tl, BR), lambda b, i: (b, i, COL["at_q"] // BR)),
                  pl.BlockSpec((1, tl, kvw), lambda b, i: (b, i, COL["at_k"] // kvw)),
                  pl.BlockSpec((1, tl, kvw), lambda b, i: (b, i, COL["at_v"] // kvw)),
                  pl.BlockSpec((tl, AT_HD), lambda b, i: (i, 0)),
                  pl.BlockSpec((tl, AT_HD), lambda b, i: (i, 0)),
                  pl.BlockSpec((1, AT_HD), lambda b, i: (0, 0)),
                  pl.BlockSpec((1, AT_HD), lambda b, i: (0, 0))],
        out_specs=[tok(BR), tok(kvw), tok(kvw)],
        out_shape=[jax.ShapeDtypeStruct((bsz, L, BR), BF16),
                   jax.ShapeDtypeStruct((bsz, L, kvw), BF16),
                   jax.ShapeDtypeStruct((bsz, L, kvw), BF16)],
        compiler_params=_cparams(("parallel", "parallel")),
        name="gqa_prepare",
    )(proj, proj, proj, cos, sin, q_g.reshape(1, AT_HD), k_g.reshape(1, AT_HD))


def _attn_kernel(q_ref, k_ref, v_ref, o_ref, *, tq, n_ctx):
    i = pl.program_id(2)
    k = k_ref[0]
    v = v_ref[0]
    L = k.shape[0]
    col = lax.broadcasted_iota(jnp.int32, (tq, L), 1)
    hidden = jnp.logical_and(col >= n_ctx, i * tq < n_ctx)
    grp = AT_H // AT_KV
    for r in range(grp):
        sl = slice(r * AT_HD, (r + 1) * AT_HD)
        s = _dot_nt(q_ref[0, :, sl], k)
        s = jnp.where(hidden, -jnp.inf, s)
        m = jnp.max(s, axis=-1, keepdims=True)
        p = jnp.exp(s - m)
        den = jnp.sum(p, axis=-1, keepdims=True)
        o_ref[0, :, sl] = _dot(p.astype(BF16), v) / den


def _attention(q, k, v, n_ctx, tq=128):
    bsz, L, _ = q.shape
    gw = (AT_H // AT_KV) * AT_HD
    return pl.pallas_call(
        functools.partial(_attn_kernel, tq=tq, n_ctx=n_ctx),
        grid=(bsz, AT_KV, L // tq),
        in_specs=[pl.BlockSpec((1, tq, gw), lambda b, g, i: (b, i, g)),
                  pl.BlockSpec((1, L, AT_HD), lambda b, g, i: (b, 0, g)),
                  pl.BlockSpec((1, L, AT_HD), lambda b, g, i: (b, 0, g))],
        out_specs=pl.BlockSpec((1, tq, gw), lambda b, g, i: (b, i, g)),
        out_shape=jax.ShapeDtypeStruct((bsz, L, BR), F32),
        compiler_params=_cparams(("parallel", "parallel", "parallel")),
        name="gqa_attention",
    )(q, k, v)


def _cap_gates(pre):
    return GATE_CAP * jnp.tanh(pre / GATE_CAP)


def _log_sigmoid(x):
    return jnp.minimum(x, 0.0) - jnp.log1p(jnp.exp(-jnp.abs(x)))


def _mlstm_kernel(qf_ref, kf_ref, vf_ref, gcf_ref, grf_ref, qb_ref, kb_ref, vb_ref, gcb_ref, grb_ref,
                  bc_ref, br_ref, hf_ref, hb_ref, c_ref, n_ref, m_ref):
    j = pl.program_id(1)
    T = ML_T

    @pl.when(j == 0)
    def _():
        c_ref[...] = jnp.zeros_like(c_ref)
        n_ref[...] = jnp.zeros_like(n_ref)
        m_ref[...] = jnp.zeros_like(m_ref)

    r2 = lax.broadcasted_iota(jnp.int32, (T, T), 0)
    c2 = lax.broadcasted_iota(jnp.int32, (T, T), 1)
    lane16 = lax.broadcasted_iota(jnp.int32, (T, 16), 1)
    sub16 = lax.broadcasted_iota(jnp.int32, (16, T), 0)

    for d, (q_ref, k_ref, v_ref, gc_ref, gr_ref, h_ref) in enumerate(
            ((qf_ref, kf_ref, vf_ref, gcf_ref, grf_ref, hf_ref),
             (qb_ref, kb_ref, vb_ref, gcb_ref, grb_ref, hb_ref))):
        seen = (c2 <= r2) if d == 0 else (c2 >= r2)
        seen_t = (r2 <= c2) if d == 0 else (r2 >= c2)
        gc = _cap_gates(gc_ref[0][:, :16] + bc_ref[...])
        gr = _cap_gates(gr_ref[0] + br_ref[...])
        for h in range(ML_H):
            ci, cf = d * ML_H + h, (2 + d) * ML_H + h
            li_row = jnp.sum(jnp.where(sub16 == ci, gr, 0.0), axis=0, keepdims=True)
            lf_row = jnp.sum(jnp.where(sub16 == cf, _log_sigmoid(gr), 0.0), axis=0, keepdims=True)
            li_col = jnp.sum(jnp.where(lane16 == ci, gc, 0.0), axis=1, keepdims=True)
            lf_col = jnp.sum(jnp.where(lane16 == cf, _log_sigmoid(gc), 0.0), axis=1, keepdims=True)
            b_col = jnp.sum(jnp.where(seen, lf_row, 0.0), axis=1, keepdims=True)
            b_row = jnp.sum(jnp.where(seen_t, lf_col, 0.0), axis=0, keepdims=True)
            g = jnp.sum(lf_col, axis=0, keepdims=True)
            m_prev = m_ref[d, h]
            q = q_ref[0, :, h * ML_DK:(h + 1) * ML_DK] * (ML_DK ** -0.5)
            k = k_ref[0, :, h * ML_DK:(h + 1) * ML_DK]
            v = v_ref[0, :, h * ML_DV:(h + 1) * ML_DV]
            C = c_ref[d, h]
            n = n_ref[d, h]

            dmat = jnp.where(seen, b_col - b_row + li_row, -jnp.inf)
            m_inter = b_col + m_prev
            m_t = jnp.maximum(m_inter, jnp.max(dmat, axis=-1, keepdims=True))
            w_inter = jnp.exp(m_inter - m_t)
            qb16 = q.astype(BF16)
            s = _dot_nt(qb16, k.astype(BF16)) * jnp.exp(dmat - m_t)
            num = w_inter * _dot(qb16, C.astype(BF16)) + _dot(s.astype(BF16), v.astype(BF16))
            den = w_inter * jnp.sum(q * n, axis=-1, keepdims=True) + jnp.sum(s, axis=-1, keepdims=True)
            h_ref[0, :, h * ML_DV:(h + 1) * ML_DV] = num / jnp.maximum(jnp.abs(den), jnp.exp(-m_t))

            loga = g - b_col + li_col
            m_new = jnp.maximum(g + m_prev, jnp.max(loga, axis=0, keepdims=True))
            carry = jnp.exp(g + m_prev - m_new)
            wk = jnp.exp(loga - m_new) * k
            c_ref[d, h] = carry * C + _dot_tn(wk.astype(BF16), v.astype(BF16))
            n_ref[d, h] = carry * n + jnp.sum(wk, axis=0, keepdims=True)
            m_ref[d, h] = m_new


def _mlstm(proj, gates_row, bias_col, bias_row, n_ctx):
    bsz, L, _ = proj.shape
    nc, nc_ctx = L // ML_T, n_ctx // ML_T
    qw, vw = ML_H * ML_DK, ML_H * ML_DV

    def dir_specs(d):
        ch = lambda j: _scan_chunk(d, j, nc_ctx, nc)
        return [pl.BlockSpec((1, ML_T, qw), lambda b, j: (b, ch(j), COL["ml_q"] // qw)),
                pl.BlockSpec((1, ML_T, qw), lambda b, j: (b, ch(j), COL["ml_k"] // qw)),
                pl.BlockSpec((1, ML_T, vw), lambda b, j: (b, ch(j), COL["ml_v"] // vw)),
                pl.BlockSpec((1, ML_T, 128), lambda b, j: (b, ch(j), COL["ml_if"] // 128)),
                pl.BlockSpec((1, 16, ML_T), lambda b, j: (b, 0, ch(j)))]

    def out_spec(d):
        ch = lambda j: _scan_chunk(d, j, nc_ctx, nc)
        return pl.BlockSpec((1, ML_T, vw), lambda b, j: (b, ch(j), 0))

    shape = jax.ShapeDtypeStruct((bsz, L, vw), F32)
    return pl.pallas_call(
        _mlstm_kernel,
        grid=(bsz, nc),
        in_specs=dir_specs(0) + dir_specs(1) + [pl.BlockSpec((1, 16), lambda b, j: (0, 0)),
                                                pl.BlockSpec((16, 1), lambda b, j: (0, 0))],
        out_specs=[out_spec(0), out_spec(1)],
        out_shape=[shape, shape],
        scratch_shapes=[pltpu.VMEM((2, ML_H, ML_DK, ML_DV), F32),
                        pltpu.VMEM((2, ML_H, 1, ML_DK), F32),
                        pltpu.VMEM((2, ML_H, 1, 1), F32)],
        compiler_params=_cparams(("parallel", "arbitrary")),
        name="mlstm_scan",
    )(proj, proj, proj, proj, gates_row, proj, proj, proj, proj, gates_row, bias_col, bias_row)


def _epilogue_kernel(yf_ref, yb_ref, bonus_ref, rwg_ref, att_ref, atg_ref, hf_ref, hb_ref, mlo_ref, mlg_ref,
                     lnw_ref, lnb_ref, mng_ref, m64_ref, m256_ref, o_ref):
    y = yf_ref[0, 0] + yb_ref[0, 0]
    mu = _dot_const_rhs(y, m64_ref[...])
    yc = y - mu
    var = _dot_const_rhs(yc * yc, m64_ref[...])
    ya = yc * lax.rsqrt(var + RW_GN_EPS) * lnw_ref[...] + lnb_ref[...] + bonus_ref[0]
    o_ref[0, 0] = (ya * _silu(rwg_ref[0])).astype(o_ref.dtype)

    o_ref[1, 0] = (att_ref[0] * _silu(atg_ref[0])).astype(o_ref.dtype)

    hh = hf_ref[0] + hb_ref[0]
    ms = _dot_const_rhs(hh * hh, m256_ref[...])
    hn = hh * lax.rsqrt(ms + EPS) * mng_ref[...]
    o_ref[2, 0] = (jax.nn.sigmoid(mlo_ref[0]) * hn * _silu(mlg_ref[0])).astype(o_ref.dtype)


def _epilogue(proj, y_rw, bonus, att, h_f, h_b, p, tl=256):
    bsz, L, _ = proj.shape
    tok = pl.BlockSpec((1, tl, BR), lambda b, i: (b, i, 0))
    pc = lambda name: pl.BlockSpec((1, tl, BR), lambda b, i: (b, i, COL[name] // BR))
    vec = pl.BlockSpec((1, BR), lambda b, i: (0, 0))
    mat = pl.BlockSpec((BR, BR), lambda b, i: (0, 0))
    return pl.pallas_call(
        _epilogue_kernel,
        grid=(bsz, L // tl),
        in_specs=[pl.BlockSpec((1, 1, tl, BR), lambda b, i: (b, 0, i, 0)),
                  pl.BlockSpec((1, 1, tl, BR), lambda b, i: (b, 1, i, 0)),
                  tok, pc("rw_g"), tok, pc("at_g"), tok, tok, pc("ml_o"), pc("ml_g"),
                  vec, vec, vec, mat, mat],
        out_specs=pl.BlockSpec((3, 1, tl, BR), lambda b, i: (0, b, i, 0)),
        out_shape=jax.ShapeDtypeStruct((3, bsz, L, BR), BF16),
        compiler_params=_cparams(("parallel", "parallel")),
        name="branch_epilogue",
    )(y_rw, y_rw, bonus, proj, att, proj, h_f, h_b, proj, proj,
      p["rw_ln_w"].reshape(1, BR), p["rw_ln_b"].reshape(1, BR), p["ml_norm_g"].reshape(1, BR),
      p["mean64"], p["mean256"])


def _merge_kernel(y_ref, w_ref, g0_ref, g1_ref, g2_ref, o_ref):
    acc = jax.nn.sigmoid(g0_ref[...]) * _dot(y_ref[0], w_ref[0])
    acc += jax.nn.sigmoid(g1_ref[...]) * _dot(y_ref[1], w_ref[1])
    acc += jax.nn.sigmoid(g2_ref[...]) * _dot(y_ref[2], w_ref[2])
    o_ref[...] = acc.astype(o_ref.dtype)


def _merge(ys, w_branch, proj2d, tm=512, tn=512):
    _, m, _ = ys.shape
    nb = D // tn
    gate = lambda n: pl.BlockSpec((tm, tn), lambda j, i: (i, n * nb + j))
    return pl.pallas_call(
        _merge_kernel,
        grid=(D // tn, m // tm),
        in_specs=[pl.BlockSpec((3, tm, BR), lambda j, i: (0, i, 0)),
                  pl.BlockSpec((3, BR, tn), lambda j, i: (0, 0, j)),
                  gate(0), gate(1), gate(2)],
        out_specs=pl.BlockSpec((tm, tn), lambda j, i: (i, j)),
        out_shape=jax.ShapeDtypeStruct((m, D), BF16),
        compiler_params=_cparams(("parallel", "parallel")),
        name="branch_merge",
    )(ys, w_branch, proj2d, proj2d, proj2d)


def _outproj_kernel(a_ref, w_ref, z_ref, gt_ref, o_ref):
    o_ref[0] = z_ref[0] + gt_ref[0, 0] * _dot(a_ref[0], w_ref[...])


def _out_projection(mixed, w_out, z, gate, n_ctx, tl=256, tn=1024):
    bsz, L, _ = z.shape
    nct = n_ctx // tl
    return pl.pallas_call(
        _outproj_kernel,
        grid=(D // tn, bsz, L // tl),
        in_specs=[pl.BlockSpec((1, tl, D), lambda j, b, i: (b, i, 0)),
                  pl.BlockSpec((D, tn), lambda j, b, i: (0, j)),
                  pl.BlockSpec((1, tl, tn), lambda j, b, i: (b, i, j)),
                  pl.BlockSpec((1, 1, 1, tn), lambda j, b, i: (b, (i >= nct).astype(jnp.int32), 0, j))],
        out_specs=pl.BlockSpec((1, tl, tn), lambda j, b, i: (b, i, j)),
        out_shape=jax.ShapeDtypeStruct((bsz, L, D), F32),
        compiler_params=_cparams(("parallel", "parallel", "parallel")),
        name="out_projection",
    )(mixed, w_out, z, gate)


def _permute_w_in(w):
    cols = [w[:, s:s + n] for _, s, n in _PERM]
    cols.append(jnp.zeros((w.shape[0], N_PROJ - 17168), w.dtype))
    return jnp.concatenate(cols, axis=1).astype(BF16)


def _rope_tables(n_ctx, n_lat):
    rows = n_lat // GRID_W
    row = jnp.repeat(jnp.arange(rows), GRID_W).astype(F32)
    col = jnp.tile(jnp.arange(GRID_W), rows).astype(F32)
    inv_freq = ROPE_THETA ** (-jnp.arange(0, AT_HD // 2, 2, dtype=F32) / (AT_HD // 2))
    ang_lat = jnp.stack([row[:, None] * inv_freq, col[:, None] * inv_freq], axis=1)
    ang = jnp.concatenate([jnp.zeros((n_ctx, 2, AT_HD // 4), F32), ang_lat], axis=0)
    cos, sin = jnp.cos(ang), jnp.sin(ang)
    cos_t = jnp.concatenate([cos[:, 0], cos[:, 0], cos[:, 1], cos[:, 1]], axis=-1)
    sin_t = jnp.concatenate([-sin[:, 0], sin[:, 0], -sin[:, 1], sin[:, 1]], axis=-1)
    return cos_t, sin_t


def _block_diag_const(width, value):
    idx = np.arange(BR) // width
    return jnp.asarray(np.where(idx[:, None] == idx[None, :], value, 0.0), dtype=BF16)


def kernel(x, c, ctx, c_ctx, norm_g, w_ada, b_ada, w_in, shift_mu, rw_w_up, rw_w0, rw_a_up, rw_a0, rw_k_k, rw_k_a, rw_r_k, rw_ln_w, rw_ln_b, at_q_g, at_k_g, ml_gate_b, ml_norm_g, w_branch, w_out, final_g):
    bsz, n_lat, _ = x.shape
    n_ctx = ctx.shape[1]
    L = n_ctx + n_lat
    depth = w_in.shape[0]

    cos_t, sin_t = _rope_tables(n_ctx, n_lat)
    tri = np.tril(np.ones((RW_T, RW_T), np.float32))
    consts = {
        "bd64": _block_diag_const(RW_N, 1.0),
        "mean64": _block_diag_const(RW_N, 1.0 / RW_N),
        "mean256": _block_diag_const(ML_DV, 1.0 / ML_DV),
        "tri": jnp.asarray(np.stack([tri, tri.T]), dtype=BF16),
    }

    cc = jnp.concatenate([c, c_ctx[None], jnp.zeros((8 - bsz - 1, D), F32)], axis=0)
    mod = _modulation(cc, w_ada, b_ada)

    z = jnp.concatenate([ctx, x], axis=1)
    for l in range(depth):
        sh, sc, gt = mod[l, :, :D], mod[l, :, D:2 * D], mod[l, :, 2 * D:]
        pick = lambda t: jnp.stack([jnp.broadcast_to(t[bsz], (bsz, D)), t[:bsz]], axis=1)[:, :, None, :]
        scale = pick((1.0 + sc) * norm_g[l])
        shift = pick(sh)
        gate = pick(gt)

        h = _norm_mod(z, scale, shift, n_ctx, BF16)
        proj2d = _matmul(h.reshape(bsz * L, D), _permute_w_in(w_in[l]))
        proj = proj2d.reshape(bsz, L, N_PROJ)

        p = dict(consts, shift_mu=shift_mu[l], rw_w_up=rw_w_up[l], rw_w0=rw_w0[l], rw_a_up=rw_a_up[l],
                 rw_a0=rw_a0[l], rw_k_k=rw_k_k[l], rw_k_a=rw_k_a[l], rw_r_k=rw_r_k[l], rw_ln_w=rw_ln_w[l],
                 rw_ln_b=rw_ln_b[l], ml_norm_g=ml_norm_g[l])
        at, rt, bt, kt, v_rw, bonus, g_rw = _rw_prepare(proj, p, n_ctx)
        y_rw = _rw_scan(at, rt, bt, kt, v_rw, g_rw, n_ctx)

        qn, kn, vn = _at_prepare(proj, cos_t, sin_t, at_q_g[l], at_k_g[l])
        att = _attention(qn, kn, vn, n_ctx)

        gates_row = jnp.swapaxes(proj[:, :, COL["ml_if"]:COL["ml_if"] + 16], 1, 2)
        bias = ml_gate_b[l].reshape(16)
        h_f, h_b = _mlstm(proj, gates_row, bias.reshape(1, 16), bias.reshape(16, 1), n_ctx)

        ys = _epilogue(proj, y_rw, bonus, att, h_f, h_b, p)
        mixed = _merge(ys.reshape(3, bsz * L, BR), w_branch[l].astype(BF16), proj2d)
        z = _out_projection(mixed.reshape(bsz, L, D), w_out[l].astype(BF16), z, gate, n_ctx)

    ones = jnp.ones((bsz, 2, 1, D), F32) * final_g
    zeros = jnp.zeros((bsz, 2, 1, D), F32)
    return _norm_mod(z, ones, zeros, n_ctx, F32)[:, n_ctx:]
```

```python
import functools
import math

import numpy as np
import jax
import jax.numpy as jnp
from jax import lax
from jax.experimental import pallas as pl
from jax.experimental.pallas import tpu as pltpu

F32 = jnp.float32
BF16 = jnp.bfloat16
HI = lax.Precision.HIGHEST

D = 2048
BR = 1024
EPS = 1e-6
GRID_W = 64

RW_H, RW_N, RW_LORA = 16, 64, 64
RW_GN_EPS = 64e-5
RW_T = 64
RW_INV_BASE = 8

AT_H, AT_KV, AT_HD = 8, 2, 128
ROPE_THETA = 10000.0

ML_H, ML_DK, ML_DV, ML_T = 4, 128, 256, 128
GATE_CAP = 15.0

_PERM = (
    ("merge", 11024, 6144), ("r", 0, 1024), ("k", 1024, 1024), ("v", 2048, 1024),
    ("rw_g", 3328, 1024), ("at_q", 4352, 1024), ("at_g", 5888, 1024), ("ml_v", 7936, 1024),
    ("ml_o", 8960, 1024), ("ml_g", 10000, 1024), ("ml_q", 6912, 512), ("ml_k", 7424, 512),
    ("wdad", 3072, 256), ("at_k", 5376, 256), ("at_v", 5632, 256), ("ml_if", 9984, 16),
)
N_PROJ = 17408
COL = {}
_off = 0
for _name, _start, _width in _PERM:
    COL[_name] = _off
    _off += _width
assert _off == 17168 and COL["ml_if"] % 128 == 0

VMEM_LIMIT = 48 * 1024 * 1024


def _cparams(sem):
    return pltpu.CompilerParams(dimension_semantics=sem, vmem_limit_bytes=VMEM_LIMIT)


def _dot(a, b, **kw):
    return jnp.dot(a, b, preferred_element_type=F32, **kw)


def _dot_nt(a, b, **kw):
    return lax.dot_general(a, b, (((1,), (1,)), ((), ())), preferred_element_type=F32, **kw)


def _dot_tn(a, b, **kw):
    return lax.dot_general(a, b, (((0,), (0,)), ((), ())), preferred_element_type=F32, **kw)


def _split3(x):
    h = x.astype(BF16)
    r = x - h.astype(F32)
    m = r.astype(BF16)
    l = (r - m.astype(F32)).astype(BF16)
    return h, m, l


def _dot_const_rhs(x, c):
    h, m, l = _split3(x)
    return _dot(h, c) + _dot(m, c) + _dot(l, c)


def _dot_const_lhs(c, x):
    h, m, l = _split3(x)
    return _dot(c, h) + _dot(c, m) + _dot(c, l)


def _silu(x):
    return x * jax.nn.sigmoid(x)


def _mod_kernel(c_ref, w_ref, b_ref, o_ref):
    x = _silu(c_ref[...])
    w = w_ref[0]
    xh = x.astype(BF16)
    xl = (x - xh.astype(F32)).astype(BF16)
    wh = w.astype(BF16)
    wl = (w - wh.astype(F32)).astype(BF16)
    o_ref[0] = _dot(xh, wh) + _dot(xh, wl) + _dot(xl, wh) + b_ref[0]


def _modulation(cc, w_ada, b_ada):
    depth = w_ada.shape[0]
    tn = 512
    return pl.pallas_call(
        _mod_kernel,
        grid=(depth, 3 * D // tn),
        in_specs=[pl.BlockSpec((8, D), lambda l, j: (0, 0)),
                  pl.BlockSpec((1, D, tn), lambda l, j: (l, 0, j)),
                  pl.BlockSpec((1, 1, tn), lambda l, j: (l, 0, j))],
        out_specs=pl.BlockSpec((1, 8, tn), lambda l, j: (l, 0, j)),
        out_shape=jax.ShapeDtypeStruct((depth, 8, 3 * D), F32),
        compiler_params=_cparams(("parallel", "parallel")),
        name="adaln_modulation",
    )(cc, w_ada, b_ada.reshape(depth, 1, 3 * D))


def _norm_kernel(z_ref, sc_ref, sh_ref, o_ref):
    x = z_ref[0]
    y = x * lax.rsqrt(jnp.mean(x * x, axis=-1, keepdims=True) + EPS)
    o_ref[0] = (y * sc_ref[0, 0] + sh_ref[0, 0]).astype(o_ref.dtype)


def _norm_mod(z, scale, shift, n_ctx, out_dtype, tl=256):
    bsz, L, _ = z.shape
    nct = n_ctx // tl
    seg = lambda b, i: (b, (i >= nct).astype(jnp.int32), 0, 0)
    return pl.pallas_call(
        _norm_kernel,
        grid=(bsz, L // tl),
        in_specs=[pl.BlockSpec((1, tl, D), lambda b, i: (b, i, 0)),
                  pl.BlockSpec((1, 1, 1, D), seg),
                  pl.BlockSpec((1, 1, 1, D), seg)],
        out_specs=pl.BlockSpec((1, tl, D), lambda b, i: (b, i, 0)),
        out_shape=jax.ShapeDtypeStruct((bsz, L, D), out_dtype),
        compiler_params=_cparams(("parallel", "parallel")),
        name="rmsnorm_modulate",
    )(z, scale, shift)


def _mm_kernel(a_ref, w_ref, o_ref):
    o_ref[...] = _dot(a_ref[...], w_ref[...]).astype(o_ref.dtype)


def _matmul(a, w, out_dtype=F32, tm=512, tn=1024):
    m, k = a.shape
    n = w.shape[1]
    return pl.pallas_call(
        _mm_kernel,
        grid=(n // tn, m // tm),
        in_specs=[pl.BlockSpec((tm, k), lambda j, i: (i, 0)),
                  pl.BlockSpec((k, tn), lambda j, i: (0, j))],
        out_specs=pl.BlockSpec((tm, tn), lambda j, i: (i, j)),
        out_shape=jax.ShapeDtypeStruct((m, n), out_dtype),
        compiler_params=_cparams(("parallel", "parallel")),
        name="in_projection",
    )(a, w)


def _rwprep_kernel(r_ref, rp_ref, rn_ref, k_ref, kp_ref, kn_ref, v_ref, vp_ref, vn_ref,
                   w_ref, wp_ref, wn_ref, mur_ref, muk_ref, muv_ref, muw_ref,
                   wup_ref, w0_ref, aup_ref, a0_ref, kk_ref, ka_ref, rk_ref, bd_ref, tri_ref,
                   at_ref, rt_ref, bt_ref, kt_ref, vo_ref, bonus_ref, g_ref, *, tl, n_ctx, L):
    i = pl.program_id(1)
    start = i * tl
    has_prev = jnp.logical_and(start != 0, start != n_ctx).astype(F32)
    has_next = jnp.logical_and(start + tl != n_ctx, start + tl != L).astype(F32)

    def shift(x_ref, p_ref, n_ref, mu_ref):
        x = x_ref[0]
        row = lax.broadcasted_iota(jnp.int32, x.shape, 0)
        prev = jnp.where(row == 0, p_ref[0, 7:8, :] * has_prev, pltpu.roll(x, 1, axis=0))
        nxt = jnp.where(row == tl - 1, n_ref[0, 0:1, :] * has_next, pltpu.roll(x, tl - 1, axis=0))
        return x + mu_ref[0:1, :] * (prev - x) + mu_ref[1:2, :] * (nxt - x)

    r = shift(r_ref, rp_ref, rn_ref, mur_ref)
    k = shift(k_ref, kp_ref, kn_ref, muk_ref)
    v = shift(v_ref, vp_ref, vn_ref, muv_ref)
    wdad = shift(w_ref, wp_ref, wn_ref, muw_ref)
    bd = bd_ref[...]

    kk = k * kk_ref[...]
    kk = kk / jnp.maximum(jnp.sqrt(_dot_const_rhs(kk * kk, bd)), 1e-12)
    bonus_ref[0] = _dot_const_rhs(r * k * rk_ref[...], bd) * v
    vo_ref[0] = v

    nchunk = tl // RW_T
    for d in range(2):
        wd = wdad[:, d * RW_LORA:(d + 1) * RW_LORA]
        ad = wdad[:, 2 * RW_LORA + d * RW_LORA:2 * RW_LORA + (d + 1) * RW_LORA]
        u = w0_ref[d:d + 1, :] + _dot(jnp.tanh(wd), wup_ref[d], precision=HI)
        logw = -math.exp(-0.5) * jax.nn.sigmoid(u)
        a = jax.nn.sigmoid(a0_ref[d:d + 1, :] + _dot(ad, aup_ref[d], precision=HI))
        kd = k * (1.0 + (a - 1.0) * ka_ref[...])
        b = kk * a
        for c in range(nchunk):
            sl = slice(c * RW_T, (c + 1) * RW_T)
            lw = logw[sl]
            lg = _dot_const_lhs(tri_ref[d], lw)
            e_pos = jnp.exp(lg)
            e_neg = jnp.exp(-lg)
            at_ref[0, 0 + d, sl, :] = -kk[sl] * jnp.exp(lg - lw)
            rt_ref[0, 0 + d, sl, :] = r[sl] * e_pos
            bt_ref[0, 0 + d, sl, :] = b[sl] * e_neg
            kt_ref[0, 0 + d, sl, :] = kd[sl] * e_neg
            g_ref[0, d, c] = jnp.exp(jnp.sum(lw, axis=0, keepdims=True))


def _rw_prepare(proj, p, n_ctx, tl=256):
    bsz, L, _ = proj.shape
    h8 = tl // 8
    nblk8 = L // 8
    main = lambda w, cb: pl.BlockSpec((1, tl, w), lambda b, i: (b, i, cb))
    prev = lambda w, cb: pl.BlockSpec((1, 8, w), lambda b, i: (b, jnp.maximum(i * h8 - 1, 0), cb))
    nxt = lambda w, cb: pl.BlockSpec((1, 8, w), lambda b, i: (b, jnp.minimum((i + 1) * h8, nblk8 - 1), cb))
    full = lambda shape: pl.BlockSpec(shape, lambda b, i: (0,) * len(shape))
    in_specs = []
    args = []
    for name, w in (("r", 1024), ("k", 1024), ("v", 1024), ("wdad", 256)):
        cb = COL[name] // w
        in_specs += [main(w, cb), prev(w, cb), nxt(w, cb)]
        args += [proj, proj, proj]
    mu = p["shift_mu"]
    in_specs += [pl.BlockSpec((2, 1024), lambda b, i: (0, 0)), pl.BlockSpec((2, 1024), lambda b, i: (0, 1)),
                 pl.BlockSpec((2, 1024), lambda b, i: (0, 2)), pl.BlockSpec((2, 256), lambda b, i: (0, 12))]
    args += [mu, mu, mu, mu]
    in_specs += [full((2, RW_LORA, BR)), full((2, BR)), full((2, RW_LORA, BR)), full((2, BR)),
                 full((1, BR)), full((1, BR)), full((1, BR)), full((BR, BR)), full((2, RW_T, RW_T))]
    args += [p["rw_w_up"], p["rw_w0"], p["rw_a_up"], p["rw_a0"], p["rw_k_k"].reshape(1, BR),
             p["rw_k_a"].reshape(1, BR), p["rw_r_k"].reshape(1, BR), p["bd64"], p["tri"]]
    dir_spec = pl.BlockSpec((1, 2, tl, BR), lambda b, i: (b, 0, i, 0))
    tok_spec = pl.BlockSpec((1, tl, BR), lambda b, i: (b, i, 0))
    nchunk = tl // RW_T
    out_specs = [dir_spec, dir_spec, dir_spec, dir_spec, tok_spec, tok_spec,
                 pl.BlockSpec((1, 2, nchunk, 1, BR), lambda b, i: (b, 0, i, 0, 0))]
    dir_shape = jax.ShapeDtypeStruct((bsz, 2, L, BR), F32)
    tok_shape = jax.ShapeDtypeStruct((bsz, L, BR), F32)
    out_shape = [dir_shape, dir_shape, dir_shape, dir_shape, tok_shape, tok_shape,
                 jax.ShapeDtypeStruct((bsz, 2, L // RW_T, 1, BR), F32)]
    return pl.pallas_call(
        functools.partial(_rwprep_kernel, tl=tl, n_ctx=n_ctx, L=L),
        grid=(bsz, L // tl),
        in_specs=in_specs, out_specs=out_specs, out_shape=out_shape,
        compiler_params=_cparams(("parallel", "parallel")),
        name="rwkv_prepare",
    )(*args)


def _scan_chunk(d, j, nc_ctx, nc):
    bwd = jnp.where(j < nc_ctx, nc_ctx - 1 - j, nc - 1 - j + nc_ctx)
    return jnp.where(d == 0, j, bwd)


RW_PREC = {"scores": "bf16", "inv": "bf16", "xs": "bf16", "av": "bf16", "u": "bf16", "y": "bf16", "state": "bf16"}


def _mmx(a, b, mode, dims="nn"):
    f = {"nn": _dot, "nt": _dot_nt, "tn": _dot_tn}[dims]
    if mode == "hi":
        return f(a, b, precision=HI)
    ah, bh = a.astype(BF16), b.astype(BF16)
    if mode == "bf16":
        return f(ah, bh)
    al = (a - ah.astype(F32)).astype(BF16)
    bl = (b - bh.astype(F32)).astype(BF16)
    return f(ah, bh) + f(ah, bl) + f(al, bh)


def _rwscan_kernel(at_ref, rt_ref, bt_ref, kt_ref, v_ref, g_ref, y_ref, s_ref):
    d = pl.program_id(1)
    j = pl.program_id(2)
    T = RW_T

    @pl.when(j == 0)
    def _():
        s_ref[...] = jnp.zeros_like(s_ref)

    lane = lax.broadcasted_iota(jnp.int32, (2 * T, 128), 1)
    row = lax.broadcasted_iota(jnp.int32, (2 * T, 128), 0)
    own = (lane // RW_N) == (row // T)
    r2 = lax.broadcasted_iota(jnp.int32, (2 * T, 2 * T), 0)
    c2 = lax.broadcasted_iota(jnp.int32, (2 * T, 2 * T), 1)
    same = (r2 // T) == (c2 // T)
    order = (r2 - c2) * (1 - 2 * d)
    strict = jnp.logical_and(same, order > 0)
    incl = jnp.logical_and(same, order >= 0)
    eye = jnp.where(r2 == c2, 1.0, 0.0)
    base_blk = (r2 // RW_INV_BASE) == (c2 // RW_INV_BASE)
    merge_blks = []
    s = RW_INV_BASE
    while s < T:
        merge_blks.append(jnp.logical_and((r2 // (2 * s)) == (c2 // (2 * s)), (r2 // s) != (c2 // s)))
        s *= 2

    def stack(x):
        return jnp.where(own, jnp.concatenate([x, x], axis=0), 0.0)

    pairs = range(BR // 128)
    sls = [slice(p * 128, (p + 1) * 128) for p in pairs]
    each = lambda fn, *lists: [fn(*xs) for xs in zip(*lists)]
    prec = RW_PREC
    AR = [jnp.concatenate([stack(at_ref[0, 0, :, sl]), stack(rt_ref[0, 0, :, sl])], axis=0) for sl in sls]
    BK = [jnp.concatenate([stack(bt_ref[0, 0, :, sl]), stack(kt_ref[0, 0, :, sl])], axis=0) for sl in sls]
    V = [stack(v_ref[0, :, sl]) for sl in sls]
    sc = each(lambda a, b: _mmx(a, b, prec["scores"], "nt"), AR, BK)
    Aab = [jnp.where(strict, x[:2 * T, :2 * T], 0.0) for x in sc]
    Aak = [jnp.where(strict, x[:2 * T, 2 * T:], 0.0) for x in sc]
    Mrbk = [jnp.concatenate([jnp.where(incl, x[2 * T:, :2 * T], 0.0),
                             jnp.where(incl, x[2 * T:, 2 * T:], 0.0)], axis=1) for x in sc]

    inv = lambda a, b: _mmx(a, b, prec["inv"])
    N = [jnp.where(base_blk, x, 0.0) for x in Aab]
    P = [eye + x for x in N]
    N = each(inv, N, N)
    NP = each(lambda n, q: inv(jnp.concatenate([n, q], axis=0), n), N, P)
    P = each(lambda q, x: q + x[2 * T:], P, NP)
    P = each(lambda q, x: q + inv(q, x[:2 * T]), P, NP)
    for off_blk in merge_blks:
        CP = each(lambda a, q: inv(jnp.where(off_blk, a, 0.0), q), Aab, P)
        P = each(lambda q, x: q + inv(q, x), P, CP)

    AV = each(lambda a, v: _mmx(a, v, prec["av"]), Aak, V)
    S = [s_ref[p] for p in pairs]
    XS = each(lambda a, s_: _mmx(a, s_, prec["xs"], "nt"), AR, S)
    U = each(lambda q, x, w: _mmx(q, x[:2 * T] + w, prec["u"]), P, XS, AV)
    UV = each(lambda u, v: jnp.concatenate([u, v], axis=0), U, V)
    Ys = each(lambda x, m, uv: x[2 * T:] + _mmx(m, uv, prec["y"]), XS, Mrbk, UV)
    dS = each(lambda uv, bk: _mmx(uv, bk, prec["state"], "tn"), UV, BK)
    for p in pairs:
        y_ref[0, 0, :, sls[p]] = Ys[p][:T] + Ys[p][T:]
        s_ref[p] = (S[p] + dS[p]) * g_ref[0, 0, 0, :, sls[p]]


def _rw_scan(at, rt, bt, kt, v, g, n_ctx):
    bsz, _, L, _ = at.shape
    nc, nc_ctx = L // RW_T, n_ctx // RW_T
    chunk = lambda d, j: _scan_chunk(d, j, nc_ctx, nc)
    dspec = pl.BlockSpec((1, 1, RW_T, BR), lambda b, d, j: (b, d, chunk(d, j), 0))
    return pl.pallas_call(
        _rwscan_kernel,
        grid=(bsz, 2, nc),
        in_specs=[dspec, dspec, dspec, dspec,
                  pl.BlockSpec((1, RW_T, BR), lambda b, d, j: (b, chunk(d, j), 0)),
                  pl.BlockSpec((1, 1, 1, 1, BR), lambda b, d, j: (b, d, chunk(d, j), 0, 0))],
        out_specs=dspec,
        out_shape=jax.ShapeDtypeStruct((bsz, 2, L, BR), F32),
        scratch_shapes=[pltpu.VMEM((BR // 128, 128, 128), F32)],
        compiler_params=_cparams(("parallel", "parallel", "arbitrary")),
        name="rwkv_scan",
    )(at, rt, bt, kt, v, g)


def _atprep_kernel(q_ref, k_ref, v_ref, cos_ref, sin_ref, qg_ref, kg_ref, qo_ref, ko_ref, vo_ref):
    cos = cos_ref[...]
    sin = sin_ref[...]
    lane = lax.broadcasted_iota(jnp.int32, cos.shape, 1)
    first_half = (lane % 64) < 32

    def norm_rope(x, g):
        y = x * lax.rsqrt(jnp.mean(x * x, axis=-1, keepdims=True) + EPS) * g
        partner = jnp.where(first_half, pltpu.roll(y, 96, axis=1), pltpu.roll(y, 32, axis=1))
        return y * cos + partner * sin

    for h in range(AT_H):
        sl = slice(h * AT_HD, (h + 1) * AT_HD)
        qo_ref[0, :, sl] = (norm_rope(q_ref[0, :, sl], qg_ref[...]) * (AT_HD ** -0.5)).astype(qo_ref.dtype)
    for h in range(AT_KV):
        sl = slice(h * AT_HD, (h + 1) * AT_HD)
        ko_ref[0, :, sl] = norm_rope(k_ref[0, :, sl], kg_ref[...]).astype(ko_ref.dtype)
    vo_ref[0] = v_ref[0].astype(vo_ref.dtype)


def _at_prepare(proj, cos, sin, q_g, k_g, tl=256):
    bsz, L, _ = proj.shape
    kvw = AT_KV * AT_HD
    tok = lambda w: pl.BlockSpec((1, tl, w), lambda b, i: (b, i, 0))
    return pl.pallas_call(
        _atprep_kernel,
        grid=(bsz, L // tl),
        in_specs=[pl.BlockSpec((1, tl, BR), lambda b, i: (b, i, COL["at_q"] // BR)),
                  pl.BlockSpec((1, tl, kvw), lambda b, i: (b, i, COL["at_k"] // kvw)),
                  pl.BlockSpec((1, tl, kvw), lambda b, i: (b, i, COL["at_v"] // kvw)),
                  pl.BlockSpec((tl, AT_HD), lambda b, i: (i, 0)),
                  pl.BlockSpec((tl, AT_HD), lambda b, i: (i, 0)),
                  pl.BlockSpec((1, AT_HD), lambda b, i: (0, 0)),
                  pl.BlockSpec((1, AT_HD), lambda b, i: (0, 0))],
        out_specs=[tok(BR), tok(kvw), tok(kvw)],
        out_shape=[jax.ShapeDtypeStruct((bsz, L, BR), BF16),
                   jax.ShapeDtypeStruct((bsz, L, kvw), BF16),
                   jax.ShapeDtypeStruct((bsz, L, kvw), BF16)],
        compiler_params=_cparams(("parallel", "parallel")),
        name="gqa_prepare",
    )(proj, proj, proj, cos, sin, q_g.reshape(1, AT_HD), k_g.reshape(1, AT_HD))


def _attn_kernel(q_ref, k_ref, v_ref, o_ref, *, tq, n_ctx):
    i = pl.program_id(2)
    k = k_ref[0]
    v = v_ref[0]
    L = k.shape[0]
    col = lax.broadcasted_iota(jnp.int32, (tq, L), 1)
    hidden = jnp.logical_and(col >= n_ctx, i * tq < n_ctx)
    grp = AT_H // AT_KV
    for r in range(grp):
        sl = slice(r * AT_HD, (r + 1) * AT_HD)
        s = _dot_nt(q_ref[0, :, sl], k)
        s = jnp.where(hidden, -jnp.inf, s)
        m = jnp.max(s, axis=-1, keepdims=True)
        p = jnp.exp(s - m)
        den = jnp.sum(p, axis=-1, keepdims=True)
        o_ref[0, :, sl] = _dot(p.astype(BF16), v) / den


def _attention(q, k, v, n_ctx, tq=128):
    bsz, L, _ = q.shape
    gw = (AT_H // AT_KV) * AT_HD
    return pl.pallas_call(
        functools.partial(_attn_kernel, tq=tq, n_ctx=n_ctx),
        grid=(bsz, AT_KV, L // tq),
        in_specs=[pl.BlockSpec((1, tq, gw), lambda b, g, i: (b, i, g)),
                  pl.BlockSpec((1, L, AT_HD), lambda b, g, i: (b, 0, g)),
                  pl.BlockSpec((1, L, AT_HD), lambda b, g, i: (b, 0, g))],
        out_specs=pl.BlockSpec((1, tq, gw), lambda b, g, i: (b, i, g)),
        out_shape=jax.ShapeDtypeStruct((bsz, L, BR), F32),
        compiler_params=_cparams(("parallel", "parallel", "parallel")),
        name="gqa_attention",
    )(q, k, v)


def _cap_gates(pre):
    return GATE_CAP * jnp.tanh(pre / GATE_CAP)


def _log_sigmoid(x):
    return jnp.minimum(x, 0.0) - jnp.log1p(jnp.exp(-jnp.abs(x)))


def _mlstm_kernel(qf_ref, kf_ref, vf_ref, gcf_ref, grf_ref, qb_ref, kb_ref, vb_ref, gcb_ref, grb_ref,
                  bc_ref, br_ref, hf_ref, hb_ref, c_ref, n_ref, m_ref):
    j = pl.program_id(1)
    T = ML_T

    @pl.when(j == 0)
    def _():
        c_ref[...] = jnp.zeros_like(c_ref)
        n_ref[...] = jnp.zeros_like(n_ref)
        m_ref[...] = jnp.zeros_like(m_ref)

    r2 = lax.broadcasted_iota(jnp.int32, (T, T), 0)
    c2 = lax.broadcasted_iota(jnp.int32, (T, T), 1)
    lane16 = lax.broadcasted_iota(jnp.int32, (T, 16), 1)
    sub16 = lax.broadcasted_iota(jnp.int32, (16, T), 0)

    for d, (q_ref, k_ref, v_ref, gc_ref, gr_ref, h_ref) in enumerate(
            ((qf_ref, kf_ref, vf_ref, gcf_ref, grf_ref, hf_ref),
             (qb_ref, kb_ref, vb_ref, gcb_ref, grb_ref, hb_ref))):
        seen = (c2 <= r2) if d == 0 else (c2 >= r2)
        seen_t = (r2 <= c2) if d == 0 else (r2 >= c2)
        gc = _cap_gates(gc_ref[0][:, :16] + bc_ref[...])
        gr = _cap_gates(gr_ref[0] + br_ref[...])
        for h in range(ML_H):
            ci, cf = d * ML_H + h, (2 + d) * ML_H + h
            li_row = jnp.sum(jnp.where(sub16 == ci, gr, 0.0), axis=0, keepdims=True)
            lf_row = jnp.sum(jnp.where(sub16 == cf, _log_sigmoid(gr), 0.0), axis=0, keepdims=True)
            li_col = jnp.sum(jnp.where(lane16 == ci, gc, 0.0), axis=1, keepdims=True)
            lf_col = jnp.sum(jnp.where(lane16 == cf, _log_sigmoid(gc), 0.0), axis=1, keepdims=True)
            b_col = jnp.sum(jnp.where(seen, lf_row, 0.0), axis=1, keepdims=True)
            b_row = jnp.sum(jnp.where(seen_t, lf_col, 0.0), axis=0, keepdims=True)
            g = jnp.sum(lf_col, axis=0, keepdims=True)
            m_prev = m_ref[d, h]
            q = q_ref[0, :, h * ML_DK:(h + 1) * ML_DK] * (ML_DK ** -0.5)
            k = k_ref[0, :, h * ML_DK:(h + 1) * ML_DK]
            v = v_ref[0, :, h * ML_DV:(h + 1) * ML_DV]
            C = c_ref[d, h]
            n = n_ref[d, h]

            dmat = jnp.where(seen, b_col - b_row + li_row, -jnp.inf)
            m_inter = b_col + m_prev
            m_t = jnp.maximum(m_inter, jnp.max(dmat, axis=-1, keepdims=True))
            w_inter = jnp.exp(m_inter - m_t)
            qb16 = q.astype(BF16)
            s = _dot_nt(qb16, k.astype(BF16)) * jnp.exp(dmat - m_t)
            num = w_inter * _dot(qb16, C.astype(BF16)) + _dot(s.astype(BF16), v.astype(BF16))
            den = w_inter * jnp.sum(q * n, axis=-1, keepdims=True) + jnp.sum(s, axis=-1, keepdims=True)
            h_ref[0, :, h * ML_DV:(h + 1) * ML_DV] = num / jnp.maximum(jnp.abs(den), jnp.exp(-m_t))

            loga = g - b_col + li_col
            m_new = jnp.maximum(g + m_prev, jnp.max(loga, axis=0, keepdims=True))
            carry = jnp.exp(g + m_prev - m_new)
            wk = jnp.exp(loga - m_new) * k
            c_ref[d, h] = carry * C + _dot_tn(wk.astype(BF16), v.astype(BF16))
            n_ref[d, h] = carry * n + jnp.sum(wk, axis=0, keepdims=True)
            m_ref[d, h] = m_new


def _mlstm(proj, gates_row, bias_col, bias_row, n_ctx):
    bsz, L, _ = proj.shape
    nc, nc_ctx = L // ML_T, n_ctx // ML_T
    qw, vw = ML_H * ML_DK, ML_H * ML_DV

    def dir_specs(d):
        ch = lambda j: _scan_chunk(d, j, nc_ctx, nc)
        return [pl.BlockSpec((1, ML_T, qw), lambda b, j: (b, ch(j), COL["ml_q"] // qw)),
                pl.BlockSpec((1, ML_T, qw), lambda b, j: (b, ch(j), COL["ml_k"] // qw)),
                pl.BlockSpec((1, ML_T, vw), lambda b, j: (b, ch(j), COL["ml_v"] // vw)),
                pl.BlockSpec((1, ML_T, 128), lambda b, j: (b, ch(j), COL["ml_if"] // 128)),
                pl.BlockSpec((1, 16, ML_T), lambda b, j: (b, 0, ch(j)))]

    def out_spec(d):
        ch = lambda j: _scan_chunk(d, j, nc_ctx, nc)
        return pl.BlockSpec((1, ML_T, vw), lambda b, j: (b, ch(j), 0))

    shape = jax.ShapeDtypeStruct((bsz, L, vw), F32)
    return pl.pallas_call(
        _mlstm_kernel,
        grid=(bsz, nc),
        in_specs=dir_specs(0) + dir_specs(1) + [pl.BlockSpec((1, 16), lambda b, j: (0, 0)),
                                                pl.BlockSpec((16, 1), lambda b, j: (0, 0))],
        out_specs=[out_spec(0), out_spec(1)],
        out_shape=[shape, shape],
        scratch_shapes=[pltpu.VMEM((2, ML_H, ML_DK, ML_DV), F32),
                        pltpu.VMEM((2, ML_H, 1, ML_DK), F32),
                        pltpu.VMEM((2, ML_H, 1, 1), F32)],
        compiler_params=_cparams(("parallel", "arbitrary")),
        name="mlstm_scan",
    )(proj, proj, proj, proj, gates_row, proj, proj, proj, proj, gates_row, bias_col, bias_row)


def _epilogue_kernel(yf_ref, yb_ref, bonus_ref, rwg_ref, att_ref, atg_ref, hf_ref, hb_ref, mlo_ref, mlg_ref,
                     lnw_ref, lnb_ref, mng_ref, m64_ref, m256_ref, o_ref):
    y = yf_ref[0, 0] + yb_ref[0, 0]
    mu = _dot_const_rhs(y, m64_ref[...])
    yc = y - mu
    var = _dot_const_rhs(yc * yc, m64_ref[...])
    ya = yc * lax.rsqrt(var + RW_GN_EPS) * lnw_ref[...] + lnb_ref[...] + bonus_ref[0]
    o_ref[0, 0] = (ya * _silu(rwg_ref[0])).astype(o_ref.dtype)

    o_ref[1, 0] = (att_ref[0] * _silu(atg_ref[0])).astype(o_ref.dtype)

    hh = hf_ref[0] + hb_ref[0]
    ms = _dot_const_rhs(hh * hh, m256_ref[...])
    hn = hh * lax.rsqrt(ms + EPS) * mng_ref[...]
    o_ref[2, 0] = (jax.nn.sigmoid(mlo_ref[0]) * hn * _silu(mlg_ref[0])).astype(o_ref.dtype)


def _epilogue(proj, y_rw, bonus, att, h_f, h_b, p, tl=256):
    bsz, L, _ = proj.shape
    tok = pl.BlockSpec((1, tl, BR), lambda b, i: (b, i, 0))
    pc = lambda name: pl.BlockSpec((1, tl, BR), lambda b, i: (b, i, COL[name] // BR))
    vec = pl.BlockSpec((1, BR), lambda b, i: (0, 0))
    mat = pl.BlockSpec((BR, BR), lambda b, i: (0, 0))
    return pl.pallas_call(
        _epilogue_kernel,
        grid=(bsz, L // tl),
        in_specs=[pl.BlockSpec((1, 1, tl, BR), lambda b, i: (b, 0, i, 0)),
                  pl.BlockSpec((1, 1, tl, BR), lambda b, i: (b, 1, i, 0)),
                  tok, pc("rw_g"), tok, pc("at_g"), tok, tok, pc("ml_o"), pc("ml_g"),
                  vec, vec, vec, mat, mat],
        out_specs=pl.BlockSpec((3, 1, tl, BR), lambda b, i: (0, b, i, 0)),
        out_shape=jax.ShapeDtypeStruct((3, bsz, L, BR), BF16),
        compiler_params=_cparams(("parallel", "parallel")),
        name="branch_epilogue",
    )(y_rw, y_rw, bonus, proj, att, proj, h_f, h_b, proj, proj,
      p["rw_ln_w"].reshape(1, BR), p["rw_ln_b"].reshape(1, BR), p["ml_norm_g"].reshape(1, BR),
      p["mean64"], p["mean256"])


def _merge_kernel(y_ref, w_ref, g0_ref, g1_ref, g2_ref, o_ref):
    acc = jax.nn.sigmoid(g0_ref[...]) * _dot(y_ref[0], w_ref[0])
    acc += jax.nn.sigmoid(g1_ref[...]) * _dot(y_ref[1], w_ref[1])
    acc += jax.nn.sigmoid(g2_ref[...]) * _dot(y_ref[2], w_ref[2])
    o_ref[...] = acc.astype(o_ref.dtype)


def _merge(ys, w_branch, proj2d, tm=512, tn=512):
    _, m, _ = ys.shape
    nb = D // tn
    gate = lambda n: pl.BlockSpec((tm, tn), lambda j, i: (i, n * nb + j))
    return pl.pallas_call(
        _merge_kernel,
        grid=(D // tn, m // tm),
        in_specs=[pl.BlockSpec((3, tm, BR), lambda j, i: (0, i, 0)),
                  pl.BlockSpec((3, BR, tn), lambda j, i: (0, 0, j)),
                  gate(0), gate(1), gate(2)],
        out_specs=pl.BlockSpec((tm, tn), lambda j, i: (i, j)),
        out_shape=jax.ShapeDtypeStruct((m, D), BF16),
        compiler_params=_cparams(("parallel", "parallel")),
        name="branch_merge",
    )(ys, w_branch, proj2d, proj2d, proj2d)


def _outproj_kernel(a_ref, w_ref, z_ref, gt_ref, o_ref):
    o_ref[0] = z_ref[0] + gt_ref[0, 0] * _dot(a_ref[0], w_ref[...])


def _out_projection(mixed, w_out, z, gate, n_ctx, tl=256, tn=1024):
    bsz, L, _ = z.shape
    nct = n_ctx // tl
    return pl.pallas_call(
        _outproj_kernel,
        grid=(D // tn, bsz, L // tl),
        in_specs=[pl.BlockSpec((1, tl, D), lambda j, b, i: (b, i, 0)),
                  pl.BlockSpec((D, tn), lambda j, b, i: (0, j)),
                  pl.BlockSpec((1, tl, tn), lambda j, b, i: (b, i, j)),
                  pl.BlockSpec((1, 1, 1, tn), lambda j, b, i: (b, (i >= nct).astype(jnp.int32), 0, j))],
        out_specs=pl.BlockSpec((1, tl, tn), lambda j, b, i: (b, i, j)),
        out_shape=jax.ShapeDtypeStruct((bsz, L, D), F32),
        compiler_params=_cparams(("parallel", "parallel", "parallel")),
        name="out_projection",
    )(mixed, w_out, z, gate)


def _permute_w_in(w):
    cols = [w[:, s:s + n] for _, s, n in _PERM]
    cols.append(jnp.zeros((w.shape[0], N_PROJ - 17168), w.dtype))
    return jnp.concatenate(cols, axis=1).astype(BF16)


def _rope_tables(n_ctx, n_lat):
    rows = n_lat // GRID_W
    row = jnp.repeat(jnp.arange(rows), GRID_W).astype(F32)
    col = jnp.tile(jnp.arange(GRID_W), rows).astype(F32)
    inv_freq = ROPE_THETA ** (-jnp.arange(0, AT_HD // 2, 2, dtype=F32) / (AT_HD // 2))
    ang_lat = jnp.stack([row[:, None] * inv_freq, col[:, None] * inv_freq], axis=1)
    ang = jnp.concatenate([jnp.zeros((n_ctx, 2, AT_HD // 4), F32), ang_lat], axis=0)
    cos, sin = jnp.cos(ang), jnp.sin(ang)
    cos_t = jnp.concatenate([cos[:, 0], cos[:, 0], cos[:, 1], cos[:, 1]], axis=-1)
    sin_t = jnp.concatenate([-sin[:, 0], sin[:, 0], -sin[:, 1], sin[:, 1]], axis=-1)
    return cos_t, sin_t


def _block_diag_const(width, value):
    idx = np.arange(BR) // width
    return jnp.asarray(np.where(idx[:, None] == idx[None, :], value, 0.0), dtype=BF16)


def kernel(x, c, ctx, c_ctx, norm_g, w_ada, b_ada, w_in, shift_mu, rw_w_up, rw_w0, rw_a_up, rw_a0, rw_k_k, rw_k_a, rw_r_k, rw_ln_w, rw_ln_b, at_q_g, at_k_g, ml_gate_b, ml_norm_g, w_branch, w_out, final_g):
    bsz, n_lat, _ = x.shape
    n_ctx = ctx.shape[1]
    L = n_ctx + n_lat
    depth = w_in.shape[0]

    cos_t, sin_t = _rope_tables(n_ctx, n_lat)
    tri = np.tril(np.ones((RW_T, RW_T), np.float32))
    consts = {
        "bd64": _block_diag_const(RW_N, 1.0),
        "mean64": _block_diag_const(RW_N, 1.0 / RW_N),
        "mean256": _block_diag_const(ML_DV, 1.0 / ML_DV),
        "tri": jnp.asarray(np.stack([tri, tri.T]), dtype=BF16),
    }

    cc = jnp.concatenate([c, c_ctx[None], jnp.zeros((8 - bsz - 1, D), F32)], axis=0)
    mod = _modulation(cc, w_ada, b_ada)

    z = jnp.concatenate([ctx, x], axis=1)
    for l in range(depth):
        sh, sc, gt = mod[l, :, :D], mod[l, :, D:2 * D], mod[l, :, 2 * D:]
        pick = lambda t: jnp.stack([jnp.broadcast_to(t[bsz], (bsz, D)), t[:bsz]], axis=1)[:, :, None, :]
        scale = pick((1.0 + sc) * norm_g[l])
        shift = pick(sh)
        gate = pick(gt)

        h = _norm_mod(z, scale, shift, n_ctx, BF16)
        proj2d = _matmul(h.reshape(bsz * L, D), _permute_w_in(w_in[l]))
        proj = proj2d.reshape(bsz, L, N_PROJ)

        p = dict(consts, shift_mu=shift_mu[l], rw_w_up=rw_w_up[l], rw_w0=rw_w0[l], rw_a_up=rw_a_up[l],
                 rw_a0=rw_a0[l], rw_k_k=rw_k_k[l], rw_k_a=rw_k_a[l], rw_r_k=rw_r_k[l], rw_ln_w=rw_ln_w[l],
                 rw_ln_b=rw_ln_b[l], ml_norm_g=ml_norm_g[l])
        at, rt, bt, kt, v_rw, bonus, g_rw = _rw_prepare(proj, p, n_ctx)
        y_rw = _rw_scan(at, rt, bt, kt, v_rw, g_rw, n_ctx)

        qn, kn, vn = _at_prepare(proj, cos_t, sin_t, at_q_g[l], at_k_g[l])
        att = _attention(qn, kn, vn, n_ctx)

        gates_row = jnp.swapaxes(proj[:, :, COL["ml_if"]:COL["ml_if"] + 16], 1, 2)
        bias = ml_gate_b[l].reshape(16)
        h_f, h_b = _mlstm(proj, gates_row, bias.reshape(1, 16), bias.reshape(16, 1), n_ctx)

        ys = _epilogue(proj, y_rw, bonus, att, h_f, h_b, p)
        mixed = _merge(ys.reshape(3, bsz * L, BR), w_branch[l].astype(BF16), proj2d)
        z = _out_projection(mixed.reshape(bsz, L, D), w_out[l].astype(BF16), z, gate, n_ctx)

    ones = jnp.ones((bsz, 2, 1, D), F32) * final_g
    zeros = jnp.zeros((bsz, 2, 1, D), F32)
    return _norm_mod(z, ones, zeros, n_ctx, F32)[:, n_ctx:]
```

```python
import functools
import math

import numpy as np
import jax
import jax.numpy as jnp
from jax import lax
from jax.experimental import pallas as pl
from jax.experimental.pallas import tpu as pltpu

F32 = jnp.float32
BF16 = jnp.bfloat16
HI = lax.Precision.HIGHEST

D = 2048
BR = 1024
EPS = 1e-6
GRID_W = 64

RW_H, RW_N, RW_LORA = 16, 64, 64
RW_GN_EPS = 64e-5
RW_T = 64
RW_INV_BASE = 8

AT_H, AT_KV, AT_HD = 8, 2, 128
ROPE_THETA = 10000.0

ML_H, ML_DK, ML_DV, ML_T = 4, 128, 256, 128
GATE_CAP = 15.0

_PERM = (
    ("merge", 11024, 6144), ("r", 0, 1024), ("k", 1024, 1024), ("v", 2048, 1024),
    ("rw_g", 3328, 1024), ("at_q", 4352, 1024), ("at_g", 5888, 1024), ("ml_v", 7936, 1024),
    ("ml_o", 8960, 1024), ("ml_g", 10000, 1024), ("ml_q", 6912, 512), ("ml_k", 7424, 512),
    ("wdad", 3072, 256), ("at_k", 5376, 256), ("at_v", 5632, 256), ("ml_if", 9984, 16),
)
N_PROJ = 17408
COL = {}
_off = 0
for _name, _start, _width in _PERM:
    COL[_name] = _off
    _off += _width
assert _off == 17168 and COL["ml_if"] % 128 == 0

VMEM_LIMIT = 48 * 1024 * 1024


def _cparams(sem):
    return pltpu.CompilerParams(dimension_semantics=sem, vmem_limit_bytes=VMEM_LIMIT)


def _dot(a, b, **kw):
    return jnp.dot(a, b, preferred_element_type=F32, **kw)


def _dot_nt(a, b, **kw):
    return lax.dot_general(a, b, (((1,), (1,)), ((), ())), preferred_element_type=F32, **kw)


def _dot_tn(a, b, **kw):
    return lax.dot_general(a, b, (((0,), (0,)), ((), ())), preferred_element_type=F32, **kw)


def _split3(x):
    h = x.astype(BF16)
    r = x - h.astype(F32)
    m = r.astype(BF16)
    l = (r - m.astype(F32)).astype(BF16)
    return h, m, l


def _split2(x):
    h = x.astype(BF16)
    return h, (x - h.astype(F32)).astype(BF16)


def _seg_reduce(x, gather, scatter):
    h, l = _split2(x)
    sh, sl = _split2(_dot(h, gather) + _dot(l, gather))
    return _dot(sh, scatter) + _dot(sl, scatter)


def _dot_const_lhs(c, x):
    h, m, l = _split3(x)
    return _dot(c, h) + _dot(c, m) + _dot(c, l)


def _silu(x):
    return x * jax.nn.sigmoid(x)


def _mod_kernel(c_ref, w_ref, b_ref, o_ref):
    x = _silu(c_ref[...])
    w = w_ref[0]
    xh = x.astype(BF16)
    xl = (x - xh.astype(F32)).astype(BF16)
    wh = w.astype(BF16)
    wl = (w - wh.astype(F32)).astype(BF16)
    o_ref[0] = _dot(xh, wh) + _dot(xh, wl) + _dot(xl, wh) + b_ref[0]


def _modulation(cc, w_ada, b_ada):
    depth = w_ada.shape[0]
    tn = 512
    return pl.pallas_call(
        _mod_kernel,
        grid=(depth, 3 * D // tn),
        in_specs=[pl.BlockSpec((8, D), lambda l, j: (0, 0)),
                  pl.BlockSpec((1, D, tn), lambda l, j: (l, 0, j)),
                  pl.BlockSpec((1, 1, tn), lambda l, j: (l, 0, j))],
        out_specs=pl.BlockSpec((1, 8, tn), lambda l, j: (l, 0, j)),
        out_shape=jax.ShapeDtypeStruct((depth, 8, 3 * D), F32),
        compiler_params=_cparams(("parallel", "parallel")),
        name="adaln_modulation",
    )(cc, w_ada, b_ada.reshape(depth, 1, 3 * D))


def _norm_kernel(z_ref, sc_ref, sh_ref, o_ref):
    x = z_ref[0]
    y = x * lax.rsqrt(jnp.mean(x * x, axis=-1, keepdims=True) + EPS)
    o_ref[0] = (y * sc_ref[0, 0] + sh_ref[0, 0]).astype(o_ref.dtype)


def _norm_mod(z, scale, shift, n_ctx, out_dtype, latent_only=False, tl=256):
    bsz, L, _ = z.shape
    nct = n_ctx // tl
    first = nct if latent_only else 0
    seg = lambda b, i: (b, (i + first >= nct).astype(jnp.int32), 0, 0)
    return pl.pallas_call(
        _norm_kernel,
        grid=(bsz, L // tl - first),
        in_specs=[pl.BlockSpec((1, tl, D), lambda b, i: (b, i + first, 0)),
                  pl.BlockSpec((1, 1, 1, D), seg),
                  pl.BlockSpec((1, 1, 1, D), seg)],
        out_specs=pl.BlockSpec((1, tl, D), lambda b, i: (b, i, 0)),
        out_shape=jax.ShapeDtypeStruct((bsz, L - first * tl, D), out_dtype),
        compiler_params=_cparams(("parallel", "parallel")),
        name="rmsnorm_modulate",
    )(z, scale, shift)


def _mm_kernel(a_ref, w_ref, o_ref):
    o_ref[...] = _dot(a_ref[...], w_ref[...]).astype(o_ref.dtype)


def _matmul(a, w, out_dtype=F32, tm=512, tn=1024):
    m, k = a.shape
    n = w.shape[1]
    return pl.pallas_call(
        _mm_kernel,
        grid=(n // tn, m // tm),
        in_specs=[pl.BlockSpec((tm, k), lambda j, i: (i, 0)),
                  pl.BlockSpec((k, tn), lambda j, i: (0, j))],
        out_specs=pl.BlockSpec((tm, tn), lambda j, i: (i, j)),
        out_shape=jax.ShapeDtypeStruct((m, n), out_dtype),
        compiler_params=_cparams(("parallel", "parallel")),
        name="in_projection",
    )(a, w)


def _rwprep_kernel(r_ref, rp_ref, rn_ref, k_ref, kp_ref, kn_ref, v_ref, vp_ref, vn_ref,
                   w_ref, wp_ref, wn_ref, mur_ref, muk_ref, muv_ref, muw_ref,
                   wup_ref, w0_ref, aup_ref, a0_ref, kk_ref, ka_ref, rk_ref, gat_ref, sct_ref, tri_ref,
                   at_ref, rt_ref, bt_ref, kt_ref, vo_ref, bonus_ref, g_ref, *, tl, n_ctx, L):
    i = pl.program_id(1)
    start = i * tl
    has_prev = jnp.logical_and(start != 0, start != n_ctx).astype(F32)
    has_next = jnp.logical_and(start + tl != n_ctx, start + tl != L).astype(F32)

    def shift(x_ref, p_ref, n_ref, mu_ref):
        x = x_ref[0]
        row = lax.broadcasted_iota(jnp.int32, x.shape, 0)
        prev = jnp.where(row == 0, p_ref[0, 7:8, :] * has_prev, pltpu.roll(x, 1, axis=0))
        nxt = jnp.where(row == tl - 1, n_ref[0, 0:1, :] * has_next, pltpu.roll(x, tl - 1, axis=0))
        return x + mu_ref[0:1, :] * (prev - x) + mu_ref[1:2, :] * (nxt - x)

    r = shift(r_ref, rp_ref, rn_ref, mur_ref)
    k = shift(k_ref, kp_ref, kn_ref, muk_ref)
    v = shift(v_ref, vp_ref, vn_ref, muv_ref)
    wdad = shift(w_ref, wp_ref, wn_ref, muw_ref)
    gat, sct = gat_ref[...], sct_ref[...]

    kk = k * kk_ref[...]
    kk = kk / jnp.maximum(jnp.sqrt(_seg_reduce(kk * kk, gat, sct)), 1e-12)
    bonus_ref[0] = _seg_reduce(r * k * rk_ref[...], gat, sct) * v
    vo_ref[0] = v

    for d in range(2):
        wd = wdad[:, d * RW_LORA:(d + 1) * RW_LORA]
        ad = wdad[:, 2 * RW_LORA + d * RW_LORA:2 * RW_LORA + (d + 1) * RW_LORA]
        u = w0_ref[d:d + 1, :] + _mmx(jnp.tanh(wd), wup_ref[d], "x3")
        logw = -math.exp(-0.5) * jax.nn.sigmoid(u)
        a = jax.nn.sigmoid(a0_ref[d:d + 1, :] + _mmx(ad, aup_ref[d], "x3"))
        kd = k * (1.0 + (a - 1.0) * ka_ref[...])
        lg = _dot_const_lhs(tri_ref[d], logw)
        e_pos = jnp.exp(lg)
        e_neg = jnp.exp(-lg)
        at_ref[0, d] = -kk * jnp.exp(lg - logw)
        rt_ref[0, d] = r * e_pos
        bt_ref[0, d] = kk * a * e_neg
        kt_ref[0, d] = kd * e_neg
        for c in range(tl // RW_T):
            g_ref[0, d, c] = jnp.exp(jnp.sum(logw[c * RW_T:(c + 1) * RW_T], axis=0, keepdims=True))


def _rw_prepare(proj, p, n_ctx, tl=256):
    bsz, L, _ = proj.shape
    h8 = tl // 8
    nblk8 = L // 8
    main = lambda w, cb: pl.BlockSpec((1, tl, w), lambda b, i: (b, i, cb))
    prev = lambda w, cb: pl.BlockSpec((1, 8, w), lambda b, i: (b, jnp.maximum(i * h8 - 1, 0), cb))
    nxt = lambda w, cb: pl.BlockSpec((1, 8, w), lambda b, i: (b, jnp.minimum((i + 1) * h8, nblk8 - 1), cb))
    full = lambda shape: pl.BlockSpec(shape, lambda b, i: (0,) * len(shape))
    in_specs = []
    args = []
    for name, w in (("r", 1024), ("k", 1024), ("v", 1024), ("wdad", 256)):
        cb = COL[name] // w
        in_specs += [main(w, cb), prev(w, cb), nxt(w, cb)]
        args += [proj, proj, proj]
    mu = p["shift_mu"]
    in_specs += [pl.BlockSpec((2, 1024), lambda b, i: (0, 0)), pl.BlockSpec((2, 1024), lambda b, i: (0, 1)),
                 pl.BlockSpec((2, 1024), lambda b, i: (0, 2)), pl.BlockSpec((2, 256), lambda b, i: (0, 12))]
    args += [mu, mu, mu, mu]
    in_specs += [full((2, RW_LORA, BR)), full((2, BR)), full((2, RW_LORA, BR)), full((2, BR)),
                 full((1, BR)), full((1, BR)), full((1, BR)), full((BR, 128)), full((128, BR)), full((2, tl, tl))]
    args += [p["rw_w_up"], p["rw_w0"], p["rw_a_up"], p["rw_a0"], p["rw_k_k"].reshape(1, BR),
             p["rw_k_a"].reshape(1, BR), p["rw_r_k"].reshape(1, BR), p["sum64"], p["bcast64"], p["tri"]]
    dir_spec = pl.BlockSpec((1, 2, tl, BR), lambda b, i: (b, 0, i, 0))
    tok_spec = pl.BlockSpec((1, tl, BR), lambda b, i: (b, i, 0))
    nchunk = tl // RW_T
    out_specs = [dir_spec, dir_spec, dir_spec, dir_spec, tok_spec, tok_spec,
                 pl.BlockSpec((1, 2, nchunk, 1, BR), lambda b, i: (b, 0, i, 0, 0))]
    dir_shape = jax.ShapeDtypeStruct((bsz, 2, L, BR), F32)
    tok_shape = jax.ShapeDtypeStruct((bsz, L, BR), F32)
    out_shape = [dir_shape, dir_shape, dir_shape, dir_shape, tok_shape, tok_shape,
                 jax.ShapeDtypeStruct((bsz, 2, L // RW_T, 1, BR), F32)]
    return pl.pallas_call(
        functools.partial(_rwprep_kernel, tl=tl, n_ctx=n_ctx, L=L),
        grid=(bsz, L // tl),
        in_specs=in_specs, out_specs=out_specs, out_shape=out_shape,
        compiler_params=_cparams(("parallel", "parallel")),
        name="rwkv_prepare",
    )(*args)


def _scan_chunk(d, j, nc_ctx, nc):
    bwd = jnp.where(j < nc_ctx, nc_ctx - 1 - j, nc - 1 - j + nc_ctx)
    return jnp.where(d == 0, j, bwd)


RW_PREC = {"scores": "bf16", "inv": "bf16", "xs": "bf16", "av": "bf16", "u": "bf16", "y": "bf16", "state": "bf16"}


def _mmx(a, b, mode, dims="nn"):
    f = {"nn": _dot, "nt": _dot_nt, "tn": _dot_tn}[dims]
    if mode == "hi":
        return f(a, b, precision=HI)
    ah, bh = a.astype(BF16), b.astype(BF16)
    if mode == "bf16":
        return f(ah, bh)
    al = (a - ah.astype(F32)).astype(BF16)
    bl = (b - bh.astype(F32)).astype(BF16)
    return f(ah, bh) + f(ah, bl) + f(al, bh)


def _rwscan_kernel(atf_ref, rtf_ref, btf_ref, ktf_ref, vf_ref, gf_ref,
                   atb_ref, rtb_ref, btb_ref, ktb_ref, vb_ref, gb_ref, yf_ref, yb_ref, s_ref):
    j = pl.program_id(1)
    T = RW_T

    @pl.when(j == 0)
    def _():
        s_ref[...] = jnp.zeros_like(s_ref)

    lane = lax.broadcasted_iota(jnp.int32, (2 * T, 128), 1)
    row = lax.broadcasted_iota(jnp.int32, (2 * T, 128), 0)
    own = (lane // RW_N) == (row // T)
    r2 = lax.broadcasted_iota(jnp.int32, (2 * T, 2 * T), 0)
    c2 = lax.broadcasted_iota(jnp.int32, (2 * T, 2 * T), 1)
    same = (r2 // T) == (c2 // T)
    strict = [jnp.logical_and(same, r2 > c2), jnp.logical_and(same, r2 < c2)]
    incl = [jnp.logical_and(same, r2 >= c2), jnp.logical_and(same, r2 <= c2)]
    eye = jnp.where(r2 == c2, 1.0, 0.0)
    base_blk = (r2 // RW_INV_BASE) == (c2 // RW_INV_BASE)
    merge_blks = []
    s = RW_INV_BASE
    while s < T:
        merge_blks.append(jnp.logical_and((r2 // (2 * s)) == (c2 // (2 * s)), (r2 // s) != (c2 // s)))
        s *= 2

    def stack(x):
        return jnp.where(own, jnp.concatenate([x, x], axis=0), 0.0)

    refs = ((atf_ref, rtf_ref, btf_ref, ktf_ref, vf_ref, gf_ref, yf_ref),
            (atb_ref, rtb_ref, btb_ref, ktb_ref, vb_ref, gb_ref, yb_ref))
    chains = [(d, p) for d in range(2) for p in range(BR // 128)]
    dirs = [d for d, _ in chains]
    sls = [slice(p * 128, (p + 1) * 128) for _, p in chains]
    each = lambda fn, *lists: [fn(*xs) for xs in zip(*lists)]
    prec = RW_PREC
    AR = [jnp.concatenate([stack(refs[d][0][0, 0, :, sl]), stack(refs[d][1][0, 0, :, sl])], axis=0)
          for d, sl in zip(dirs, sls)]
    BK = [jnp.concatenate([stack(refs[d][2][0, 0, :, sl]), stack(refs[d][3][0, 0, :, sl])], axis=0)
          for d, sl in zip(dirs, sls)]
    V = [stack(refs[d][4][0, :, sl]) for d, sl in zip(dirs, sls)]
    sc = each(lambda a, b: _mmx(a, b, prec["scores"], "nt"), AR, BK)
    Aab = [jnp.where(strict[d], x[:2 * T, :2 * T], 0.0) for d, x in zip(dirs, sc)]
    Aak = [jnp.where(strict[d], x[:2 * T, 2 * T:], 0.0) for d, x in zip(dirs, sc)]
    Mrbk = [jnp.concatenate([jnp.where(incl[d], x[2 * T:, :2 * T], 0.0),
                             jnp.where(incl[d], x[2 * T:, 2 * T:], 0.0)], axis=1) for d, x in zip(dirs, sc)]

    inv = lambda a, b: _mmx(a, b, prec["inv"])
    N = [jnp.where(base_blk, x, 0.0) for x in Aab]
    P = [eye + x for x in N]
    N = each(inv, N, N)
    NP = each(lambda n, q: inv(jnp.concatenate([n, q], axis=0), n), N, P)
    P = each(lambda q, x: q + x[2 * T:], P, NP)
    P = each(lambda q, x: q + inv(q, x[:2 * T]), P, NP)
    for off_blk in merge_blks:
        CP = each(lambda a, q: inv(jnp.where(off_blk, a, 0.0), q), Aab, P)
        P = each(lambda q, x: q + inv(q, x), P, CP)

    AV = each(lambda a, v: _mmx(a, v, prec["av"]), Aak, V)
    S = [s_ref[d, p] for d, p in chains]
    XS = each(lambda a, s_: _mmx(a, s_, prec["xs"], "nt"), AR, S)
    U = each(lambda q, x, w: _mmx(q, x[:2 * T] + w, prec["u"]), P, XS, AV)
    UV = each(lambda u, v: jnp.concatenate([u, v], axis=0), U, V)
    Ys = each(lambda x, m, uv: x[2 * T:] + _mmx(m, uv, prec["y"]), XS, Mrbk, UV)
    dS = each(lambda uv, bk: _mmx(uv, bk, prec["state"], "tn"), UV, BK)
    for c, (d, p) in enumerate(chains):
        refs[d][6][0, :, sls[c]] = Ys[c][:T] + Ys[c][T:]
        s_ref[d, p] = (S[c] + dS[c]) * refs[d][5][0, 0, 0, :, sls[c]]


def _rw_scan(at, rt, bt, kt, v, g, n_ctx):
    bsz, _, L, _ = at.shape
    nc, nc_ctx = L // RW_T, n_ctx // RW_T

    def specs(d):
        ch = lambda j: _scan_chunk(d, j, nc_ctx, nc)
        dspec = pl.BlockSpec((1, 1, RW_T, BR), lambda b, j: (b, d, ch(j), 0))
        tspec = pl.BlockSpec((1, RW_T, BR), lambda b, j: (b, ch(j), 0))
        return tspec, [dspec, dspec, dspec, dspec, tspec,
                       pl.BlockSpec((1, 1, 1, 1, BR), lambda b, j: (b, d, ch(j), 0, 0))]

    (out_f, in_f), (out_b, in_b) = specs(0), specs(1)
    shape = jax.ShapeDtypeStruct((bsz, L, BR), F32)
    return pl.pallas_call(
        _rwscan_kernel,
        grid=(bsz, nc),
        in_specs=in_f + in_b,
        out_specs=[out_f, out_b],
        out_shape=[shape, shape],
        scratch_shapes=[pltpu.VMEM((2, BR // 128, 128, 128), F32)],
        compiler_params=_cparams(("parallel", "arbitrary")),
        name="rwkv_scan",
    )(at, rt, bt, kt, v, g, at, rt, bt, kt, v, g)


def _atprep_kernel(q_ref, k_ref, v_ref, cos_ref, sin_ref, qg_ref, kg_ref, qo_ref, ko_ref, vo_ref):
    cos = cos_ref[...]
    sin = sin_ref[...]
    lane = lax.broadcasted_iota(jnp.int32, cos.shape, 1)
    first_half = (lane % 64) < 32

    def norm_rope(x, g):
        y = x * lax.rsqrt(jnp.mean(x * x, axis=-1, keepdims=True) + EPS) * g
        partner = jnp.where(first_half, pltpu.roll(y, 96, axis=1), pltpu.roll(y, 32, axis=1))
        return y * cos + partner * sin

    for h in range(AT_H):
        sl = slice(h * AT_HD, (h + 1) * AT_HD)
        q_scale = AT_HD ** -0.5 * math.log2(math.e)
        qo_ref[0, :, sl] = (norm_rope(q_ref[0, :, sl], qg_ref[...]) * q_scale).astype(qo_ref.dtype)
    for h in range(AT_KV):
        sl = slice(h * AT_HD, (h + 1) * AT_HD)
        ko_ref[0, :, sl] = norm_rope(k_ref[0, :, sl], kg_ref[...]).astype(ko_ref.dtype)
    vo_ref[0] = v_ref[0].astype(vo_ref.dtype)


def _at_prepare(proj, cos, sin, q_g, k_g, tl=256):
    bsz, L, _ = proj.shape
    kvw = AT_KV * AT_HD
    tok = lambda w: pl.BlockSpec((1, tl, w), lambda b, i: (b, i, 0))
    return pl.pallas_call(
        _atprep_kernel,
        grid=(bsz, L // tl),
        in_specs=[pl.BlockSpec((1, tl, BR), lambda b, i: (b, i, COL["at_q"] // BR)),
                  pl.BlockSpec((1, tl, kvw), lambda b, i: (b, i, COL["at_k"] // kvw)),
                  pl.BlockSpec((1, tl, kvw), lambda b, i: (b, i, COL["at_v"] // kvw)),
                  pl.BlockSpec((tl, AT_HD), lambda b, i: (i, 0)),
                  pl.BlockSpec((tl, AT_HD), lambda b, i: (i, 0)),
                  pl.BlockSpec((1, AT_HD), lambda b, i: (0, 0)),
                  pl.BlockSpec((1, AT_HD), lambda b, i: (0, 0))],
        out_specs=[tok(BR), tok(kvw), tok(kvw)],
        out_shape=[jax.ShapeDtypeStruct((bsz, L, BR), BF16),
                   jax.ShapeDtypeStruct((bsz, L, kvw), BF16),
                   jax.ShapeDtypeStruct((bsz, L, kvw), BF16)],
        compiler_params=_cparams(("parallel", "parallel")),
        name="gqa_prepare",
    )(proj, proj, proj, cos, sin, q_g.reshape(1, AT_HD), k_g.reshape(1, AT_HD))


def _attn_kernel(q_ref, k_ref, v_ref, o_ref, s_ref, p_ref, l_ref, *, tq, n_ctx, kb, rows):
    i = pl.program_id(2)
    grp = AT_H // AT_KV
    def attend(n_keys):
        chunks = [slice(c * kb, (c + 1) * kb) for c in range(n_keys // kb)]

        def scores(r):
            s_ref[r, :, :n_keys] = _dot_nt(q_ref[0, :, r * AT_HD:(r + 1) * AT_HD], k_ref[0, :n_keys, :])

        def softmax(r):
            for rb in range(tq // rows):
                rs = slice(rb * rows, (rb + 1) * rows)
                mx = s_ref[r, rs, chunks[0]]
                for ch in chunks[1:]:
                    mx = jnp.maximum(mx, s_ref[r, rs, ch])
                m = jnp.broadcast_to(jnp.max(mx, axis=-1, keepdims=True), mx.shape)
                tot = jnp.zeros_like(mx)
                for ch in chunks:
                    p = jnp.exp2(s_ref[r, rs, ch] - m)
                    tot = tot + p
                    p_ref[r, rs, ch] = p.astype(BF16)
                l_ref[r, rs, :] = jnp.broadcast_to(jnp.sum(tot, axis=-1, keepdims=True), (rows, AT_HD))

        def values(r):
            o_ref[0, :, r * AT_HD:(r + 1) * AT_HD] = _dot(p_ref[r, :, :n_keys], v_ref[0, :n_keys, :]) / l_ref[r]

        stages = (scores, softmax, values)
        for t in range(grp + len(stages) - 1):
            for st, fn in enumerate(stages):
                if 0 <= t - st < grp:
                    fn(t - st)

    @pl.when(i * tq < n_ctx)
    def _():
        attend(n_ctx)

    @pl.when(i * tq >= n_ctx)
    def _():
        attend(k_ref.shape[1])


def _attention(q, k, v, n_ctx, tq=256, kb=256, rows=32):
    bsz, L, _ = q.shape
    gw = (AT_H // AT_KV) * AT_HD
    return pl.pallas_call(
        functools.partial(_attn_kernel, tq=tq, n_ctx=n_ctx, kb=kb, rows=rows),
        scratch_shapes=[pltpu.VMEM((gw // AT_HD, tq, L), F32), pltpu.VMEM((gw // AT_HD, tq, L), BF16),
                        pltpu.VMEM((gw // AT_HD, tq, AT_HD), F32)],
        grid=(bsz, AT_KV, L // tq),
        in_specs=[pl.BlockSpec((1, tq, gw), lambda b, g, i: (b, i, g)),
                  pl.BlockSpec((1, L, AT_HD), lambda b, g, i: (b, 0, g)),
                  pl.BlockSpec((1, L, AT_HD), lambda b, g, i: (b, 0, g))],
        out_specs=pl.BlockSpec((1, tq, gw), lambda b, g, i: (b, i, g)),
        out_shape=jax.ShapeDtypeStruct((bsz, L, BR), F32),
        compiler_params=_cparams(("parallel", "parallel", "parallel")),
        name="gqa_attention",
    )(q, k, v)


def _cap_gates(pre):
    return GATE_CAP * jnp.tanh(pre / GATE_CAP)


def _log_sigmoid(x):
    return jnp.minimum(x, 0.0) - jnp.log1p(jnp.exp(-jnp.abs(x)))


def _mlstm_kernel(qf_ref, kf_ref, vf_ref, gcf_ref, grf_ref, qb_ref, kb_ref, vb_ref, gcb_ref, grb_ref,
                  bc_ref, br_ref, hf_ref, hb_ref, c_ref, n_ref, m_ref):
    j = pl.program_id(1)
    T = ML_T

    @pl.when(j == 0)
    def _():
        c_ref[...] = jnp.zeros_like(c_ref)
        n_ref[...] = jnp.zeros_like(n_ref)
        m_ref[...] = jnp.zeros_like(m_ref)

    r2 = lax.broadcasted_iota(jnp.int32, (T, T), 0)
    c2 = lax.broadcasted_iota(jnp.int32, (T, T), 1)
    lane16 = lax.broadcasted_iota(jnp.int32, (T, 16), 1)
    sub16 = lax.broadcasted_iota(jnp.int32, (16, T), 0)

    refs = ((qf_ref, kf_ref, vf_ref, gcf_ref, grf_ref, hf_ref), (qb_ref, kb_ref, vb_ref, gcb_ref, grb_ref, hb_ref))
    seen = [c2 <= r2, c2 >= r2]
    seen_t = [r2 <= c2, r2 >= c2]
    gc = [_cap_gates(refs[d][3][0][:, :16] + bc_ref[...]) for d in range(2)]
    gr = [_cap_gates(refs[d][4][0] + br_ref[...]) for d in range(2)]
    lsc = [_log_sigmoid(x) for x in gc]
    lsr = [_log_sigmoid(x) for x in gr]

    chains = [(d, h) for d in range(2) for h in range(ML_H)]
    each = lambda fn, *lists: [fn(*xs) for xs in zip(*lists)]
    pick_row = lambda x, idx: jnp.sum(jnp.where(sub16 == idx, x, 0.0), axis=0, keepdims=True)
    pick_col = lambda x, idx: jnp.sum(jnp.where(lane16 == idx, x, 0.0), axis=1, keepdims=True)
    li_row = [pick_row(gr[d], d * ML_H + h) for d, h in chains]
    lf_row = [pick_row(lsr[d], (2 + d) * ML_H + h) for d, h in chains]
    li_col = [pick_col(gc[d], d * ML_H + h) for d, h in chains]
    lf_col = [pick_col(lsc[d], (2 + d) * ML_H + h) for d, h in chains]
    b_col = [jnp.sum(jnp.where(seen[d], x, 0.0), axis=1, keepdims=True) for (d, _), x in zip(chains, lf_row)]
    b_row = [jnp.sum(jnp.where(seen_t[d], x, 0.0), axis=0, keepdims=True) for (d, _), x in zip(chains, lf_col)]
    g = [jnp.sum(x, axis=0, keepdims=True) for x in lf_col]
    m_prev = [m_ref[d, h] for d, h in chains]
    q = [refs[d][0][0, :, h * ML_DK:(h + 1) * ML_DK] * (ML_DK ** -0.5) for d, h in chains]
    k = [refs[d][1][0, :, h * ML_DK:(h + 1) * ML_DK] for d, h in chains]
    vb = [refs[d][2][0, :, h * ML_DV:(h + 1) * ML_DV].astype(BF16) for d, h in chains]
    C = [c_ref[d, h] for d, h in chains]
    n = [n_ref[d, h] for d, h in chains]
    qb = [x.astype(BF16) for x in q]

    dmat = [jnp.where(seen[d], bc - br + li, -jnp.inf) for (d, _), bc, br, li in zip(chains, b_col, b_row, li_row)]
    m_inter = each(lambda bc, m: bc + m, b_col, m_prev)
    m_t = each(lambda mi, dm: jnp.maximum(mi, jnp.max(dm, axis=-1, keepdims=True)), m_inter, dmat)
    w_inter = each(lambda mi, mt: jnp.exp(mi - mt), m_inter, m_t)
    qk = each(lambda a, b: _dot_nt(a, b.astype(BF16)), qb, k)
    qc = each(lambda a, b: _dot(a, b.astype(BF16)), qb, C)
    s = each(lambda x, dm, mt: x * jnp.exp(dm - mt), qk, dmat, m_t)
    sv = each(lambda a, b: _dot(a.astype(BF16), b), s, vb)
    qn = each(lambda a, b: jnp.sum(a * b, axis=-1, keepdims=True), q, n)
    den = each(lambda w, a, x: w * a + jnp.sum(x, axis=-1, keepdims=True), w_inter, qn, s)
    for (d, h), w, a, b, dn, mt in zip(chains, w_inter, qc, sv, den, m_t):
        refs[d][5][0, :, h * ML_DV:(h + 1) * ML_DV] = (w * a + b) / jnp.maximum(jnp.abs(dn), jnp.exp(-mt))

    loga = each(lambda g_, bc, li: g_ - bc + li, g, b_col, li_col)
    m_new = each(lambda g_, m, la: jnp.maximum(g_ + m, jnp.max(la, axis=0, keepdims=True)), g, m_prev, loga)
    carry = each(lambda g_, m, mn: jnp.exp(g_ + m - mn), g, m_prev, m_new)
    wk = each(lambda la, mn, k_: jnp.exp(la - mn) * k_, loga, m_new, k)
    kv = each(lambda a, b: _dot_tn(a.astype(BF16), b), wk, vb)
    for (d, h), cr, c_, kv_, n_, wk_, mn in zip(chains, carry, C, kv, n, wk, m_new):
        c_ref[d, h] = cr * c_ + kv_
        n_ref[d, h] = cr * n_ + jnp.sum(wk_, axis=0, keepdims=True)
        m_ref[d, h] = mn


def _mlstm(proj, gates_row, bias_col, bias_row, n_ctx):
    bsz, L, _ = proj.shape
    nc, nc_ctx = L // ML_T, n_ctx // ML_T
    qw, vw = ML_H * ML_DK, ML_H * ML_DV

    def dir_specs(d):
        ch = lambda j: _scan_chunk(d, j, nc_ctx, nc)
        return [pl.BlockSpec((1, ML_T, qw), lambda b, j: (b, ch(j), COL["ml_q"] // qw)),
                pl.BlockSpec((1, ML_T, qw), lambda b, j: (b, ch(j), COL["ml_k"] // qw)),
                pl.BlockSpec((1, ML_T, vw), lambda b, j: (b, ch(j), COL["ml_v"] // vw)),
                pl.BlockSpec((1, ML_T, 128), lambda b, j: (b, ch(j), COL["ml_if"] // 128)),
                pl.BlockSpec((1, 16, ML_T), lambda b, j: (b, 0, ch(j)))]

    def out_spec(d):
        ch = lambda j: _scan_chunk(d, j, nc_ctx, nc)
        return pl.BlockSpec((1, ML_T, vw), lambda b, j: (b, ch(j), 0))

    shape = jax.ShapeDtypeStruct((bsz, L, vw), F32)
    return pl.pallas_call(
        _mlstm_kernel,
        grid=(bsz, nc),
        in_specs=dir_specs(0) + dir_specs(1) + [pl.BlockSpec((1, 16), lambda b, j: (0, 0)),
                                                pl.BlockSpec((16, 1), lambda b, j: (0, 0))],
        out_specs=[out_spec(0), out_spec(1)],
        out_shape=[shape, shape],
        scratch_shapes=[pltpu.VMEM((2, ML_H, ML_DK, ML_DV), F32),
                        pltpu.VMEM((2, ML_H, 1, ML_DK), F32),
                        pltpu.VMEM((2, ML_H, 1, 1), F32)],
        compiler_params=_cparams(("parallel", "arbitrary")),
        name="mlstm_scan",
    )(proj, proj, proj, proj, gates_row, proj, proj, proj, proj, gates_row, bias_col, bias_row)


def _epilogue_kernel(yf_ref, yb_ref, bonus_ref, rwg_ref, att_ref, atg_ref, hf_ref, hb_ref, mlo_ref, mlg_ref,
                     lnw_ref, lnb_ref, mng_ref, m64_ref, b64_ref, m256_ref, b256_ref, o_ref):
    y = yf_ref[0] + yb_ref[0]
    mu = _seg_reduce(y, m64_ref[...], b64_ref[...])
    yc = y - mu
    var = _seg_reduce(yc * yc, m64_ref[...], b64_ref[...])
    ya = yc * lax.rsqrt(var + RW_GN_EPS) * lnw_ref[...] + lnb_ref[...] + bonus_ref[0]
    o_ref[0, 0] = (ya * _silu(rwg_ref[0])).astype(o_ref.dtype)

    o_ref[1, 0] = (att_ref[0] * _silu(atg_ref[0])).astype(o_ref.dtype)

    hh = hf_ref[0] + hb_ref[0]
    ms = _seg_reduce(hh * hh, m256_ref[...], b256_ref[...])
    hn = hh * lax.rsqrt(ms + EPS) * mng_ref[...]
    o_ref[2, 0] = (jax.nn.sigmoid(mlo_ref[0]) * hn * _silu(mlg_ref[0])).astype(o_ref.dtype)


def _epilogue(proj, y_f, y_b, bonus, att, h_f, h_b, p, tl=256):
    bsz, L, _ = proj.shape
    tok = pl.BlockSpec((1, tl, BR), lambda b, i: (b, i, 0))
    pc = lambda name: pl.BlockSpec((1, tl, BR), lambda b, i: (b, i, COL[name] // BR))
    vec = pl.BlockSpec((1, BR), lambda b, i: (0, 0))
    gat = pl.BlockSpec((BR, 128), lambda b, i: (0, 0))
    sct = pl.BlockSpec((128, BR), lambda b, i: (0, 0))
    return pl.pallas_call(
        _epilogue_kernel,
        grid=(bsz, L // tl),
        in_specs=[tok, tok, tok, pc("rw_g"), tok, pc("at_g"), tok, tok, pc("ml_o"), pc("ml_g"),
                  vec, vec, vec, gat, sct, gat, sct],
        out_specs=pl.BlockSpec((3, 1, tl, BR), lambda b, i: (0, b, i, 0)),
        out_shape=jax.ShapeDtypeStruct((3, bsz, L, BR), BF16),
        compiler_params=_cparams(("parallel", "parallel")),
        name="branch_epilogue",
    )(y_f, y_b, bonus, proj, att, proj, h_f, h_b, proj, proj,
      p["rw_ln_w"].reshape(1, BR), p["rw_ln_b"].reshape(1, BR), p["ml_norm_g"].reshape(1, BR),
      p["mean64"], p["bcast64"], p["mean256"], p["bcast256"])


def _merge_kernel(y_ref, w_ref, g0_ref, g1_ref, g2_ref, o_ref):
    acc = jax.nn.sigmoid(g0_ref[...]) * _dot(y_ref[0], w_ref[0])
    acc += jax.nn.sigmoid(g1_ref[...]) * _dot(y_ref[1], w_ref[1])
    acc += jax.nn.sigmoid(g2_ref[...]) * _dot(y_ref[2], w_ref[2])
    o_ref[...] = acc.astype(o_ref.dtype)


def _merge(ys, w_branch, proj2d, tm=512, tn=1024):
    _, m, _ = ys.shape
    nb = D // tn
    gate = lambda n: pl.BlockSpec((tm, tn), lambda j, i: (i, n * nb + j))
    return pl.pallas_call(
        _merge_kernel,
        grid=(D // tn, m // tm),
        in_specs=[pl.BlockSpec((3, tm, BR), lambda j, i: (0, i, 0)),
                  pl.BlockSpec((3, BR, tn), lambda j, i: (0, 0, j)),
                  gate(0), gate(1), gate(2)],
        out_specs=pl.BlockSpec((tm, tn), lambda j, i: (i, j)),
        out_shape=jax.ShapeDtypeStruct((m, D), BF16),
        compiler_params=_cparams(("parallel", "parallel")),
        name="branch_merge",
    )(ys, w_branch, proj2d, proj2d, proj2d)


def _outproj_kernel(a_ref, w_ref, z_ref, gt_ref, o_ref):
    o_ref[0] = z_ref[0] + gt_ref[0, 0] * _dot(a_ref[0], w_ref[...])


def _out_projection(mixed, w_out, z, gate, n_ctx, tl=256, tn=1024):
    bsz, L, _ = z.shape
    nct = n_ctx // tl
    return pl.pallas_call(
        _outproj_kernel,
        grid=(D // tn, bsz, L // tl),
        in_specs=[pl.BlockSpec((1, tl, D), lambda j, b, i: (b, i, 0)),
                  pl.BlockSpec((D, tn), lambda j, b, i: (0, j)),
                  pl.BlockSpec((1, tl, tn), lambda j, b, i: (b, i, j)),
                  pl.BlockSpec((1, 1, 1, tn), lambda j, b, i: (b, (i >= nct).astype(jnp.int32), 0, j))],
        out_specs=pl.BlockSpec((1, tl, tn), lambda j, b, i: (b, i, j)),
        out_shape=jax.ShapeDtypeStruct((bsz, L, D), F32),
        compiler_params=_cparams(("parallel", "parallel", "parallel")),
        name="out_projection",
    )(mixed, w_out, z, gate)


def _permute_w_in(w):
    cols = [w[:, s:s + n] for _, s, n in _PERM]
    cols.append(jnp.zeros((w.shape[0], N_PROJ - 17168), w.dtype))
    return jnp.concatenate(cols, axis=1).astype(BF16)


def _rope_tables(n_ctx, n_lat):
    rows = n_lat // GRID_W
    row = jnp.repeat(jnp.arange(rows), GRID_W).astype(F32)
    col = jnp.tile(jnp.arange(GRID_W), rows).astype(F32)
    inv_freq = ROPE_THETA ** (-jnp.arange(0, AT_HD // 2, 2, dtype=F32) / (AT_HD // 2))
    ang_lat = jnp.stack([row[:, None] * inv_freq, col[:, None] * inv_freq], axis=1)
    ang = jnp.concatenate([jnp.zeros((n_ctx, 2, AT_HD // 4), F32), ang_lat], axis=0)
    cos, sin = jnp.cos(ang), jnp.sin(ang)
    cos_t = jnp.concatenate([cos[:, 0], cos[:, 0], cos[:, 1], cos[:, 1]], axis=-1)
    sin_t = jnp.concatenate([-sin[:, 0], sin[:, 0], -sin[:, 1], sin[:, 1]], axis=-1)
    return cos_t, sin_t


def _group_consts(width, value):
    member = (np.arange(BR)[:, None] // width) == np.arange(128)[None, :]
    return (jnp.asarray(np.where(member, value, 0.0), dtype=BF16),
            jnp.asarray(np.where(member.T, 1.0, 0.0), dtype=BF16))


def _chunk_tri(tl):
    t = np.arange(tl)
    same = (t[:, None] // RW_T) == (t[None, :] // RW_T)
    fwd = same & (t[None, :] <= t[:, None])
    bwd = same & (t[None, :] >= t[:, None])
    return jnp.asarray(np.stack([fwd, bwd]).astype(np.float32), dtype=BF16)


def kernel(x, c, ctx, c_ctx, norm_g, w_ada, b_ada, w_in, shift_mu, rw_w_up, rw_w0, rw_a_up, rw_a0, rw_k_k, rw_k_a, rw_r_k, rw_ln_w, rw_ln_b, at_q_g, at_k_g, ml_gate_b, ml_norm_g, w_branch, w_out, final_g):
    bsz, n_lat, _ = x.shape
    n_ctx = ctx.shape[1]
    L = n_ctx + n_lat
    depth = w_in.shape[0]

    cos_t, sin_t = _rope_tables(n_ctx, n_lat)
    sum64, bcast64 = _group_consts(RW_N, 1.0)
    mean64, _ = _group_consts(RW_N, 1.0 / RW_N)
    mean256, bcast256 = _group_consts(ML_DV, 1.0 / ML_DV)
    consts = {"sum64": sum64, "bcast64": bcast64, "mean64": mean64, "mean256": mean256, "bcast256": bcast256,
              "tri": _chunk_tri(256)}

    cc = jnp.concatenate([c, c_ctx[None], jnp.zeros((8 - bsz - 1, D), F32)], axis=0)
    mod = _modulation(cc, w_ada, b_ada)

    z = jnp.concatenate([ctx, x], axis=1)
    for l in range(depth):
        sh, sc, gt = mod[l, :, :D], mod[l, :, D:2 * D], mod[l, :, 2 * D:]
        pick = lambda t: jnp.stack([jnp.broadcast_to(t[bsz], (bsz, D)), t[:bsz]], axis=1)[:, :, None, :]
        scale = pick((1.0 + sc) * norm_g[l])
        shift = pick(sh)
        gate = pick(gt)

        h = _norm_mod(z, scale, shift, n_ctx, BF16)
        proj2d = _matmul(h.reshape(bsz * L, D), _permute_w_in(w_in[l]))
        proj = proj2d.reshape(bsz, L, N_PROJ)

        p = dict(consts, shift_mu=shift_mu[l], rw_w_up=rw_w_up[l], rw_w0=rw_w0[l], rw_a_up=rw_a_up[l],
                 rw_a0=rw_a0[l], rw_k_k=rw_k_k[l], rw_k_a=rw_k_a[l], rw_r_k=rw_r_k[l], rw_ln_w=rw_ln_w[l],
                 rw_ln_b=rw_ln_b[l], ml_norm_g=ml_norm_g[l])
        at, rt, bt, kt, v_rw, bonus, g_rw = _rw_prepare(proj, p, n_ctx)
        y_f, y_b = _rw_scan(at, rt, bt, kt, v_rw, g_rw, n_ctx)

        qn, kn, vn = _at_prepare(proj, cos_t, sin_t, at_q_g[l], at_k_g[l])
        att = _attention(qn, kn, vn, n_ctx)

        gates_row = jnp.swapaxes(proj[:, :, COL["ml_if"]:COL["ml_if"] + 16], 1, 2)
        bias = ml_gate_b[l].reshape(16)
        h_f, h_b = _mlstm(proj, gates_row, bias.reshape(1, 16), bias.reshape(16, 1), n_ctx)

        ys = _epilogue(proj, y_f, y_b, bonus, att, h_f, h_b, p)
        mixed = _merge(ys.reshape(3, bsz * L, BR), w_branch[l].astype(BF16), proj2d)
        z = _out_projection(mixed.reshape(bsz, L, D), w_out[l].astype(BF16), z, gate, n_ctx)

    ones = jnp.ones((bsz, 2, 1, D), F32) * final_g
    zeros = jnp.zeros((bsz, 2, 1, D), F32)
    return _norm_mod(z, ones, zeros, n_ctx, F32, latent_only=True)
```

```python
import functools
import math

import numpy as np
import jax
import jax.numpy as jnp
from jax import lax
from jax.experimental import pallas as pl
from jax.experimental.pallas import tpu as pltpu

F32 = jnp.float32
BF16 = jnp.bfloat16
HI = lax.Precision.HIGHEST

D = 2048
BR = 1024
EPS = 1e-6
GRID_W = 64

RW_H, RW_N, RW_LORA = 16, 64, 64
RW_GN_EPS = 64e-5
RW_T = 64
RW_INV_BASE = 8

AT_H, AT_KV, AT_HD = 8, 2, 128
ROPE_THETA = 10000.0

ML_H, ML_DK, ML_DV, ML_T = 4, 128, 256, 128
GATE_CAP = 15.0

_PERM = (
    ("merge", 11024, 6144), ("r", 0, 1024), ("k", 1024, 1024), ("v", 2048, 1024),
    ("rw_g", 3328, 1024), ("at_q", 4352, 1024), ("at_g", 5888, 1024), ("ml_v", 7936, 1024),
    ("ml_o", 8960, 1024), ("ml_g", 10000, 1024), ("ml_q", 6912, 512), ("ml_k", 7424, 512),
    ("wdad", 3072, 256), ("at_k", 5376, 256), ("at_v", 5632, 256), ("ml_if", 9984, 16),
)
N_PROJ = 17408
COL = {}
_off = 0
for _name, _start, _width in _PERM:
    COL[_name] = _off
    _off += _width
assert _off == 17168 and COL["ml_if"] % 128 == 0

VMEM_LIMIT = 48 * 1024 * 1024


def _cparams(sem):
    return pltpu.CompilerParams(dimension_semantics=sem, vmem_limit_bytes=VMEM_LIMIT)


def _dot(a, b, **kw):
    return jnp.dot(a, b, preferred_element_type=F32, **kw)


def _dot_nt(a, b, **kw):
    return lax.dot_general(a, b, (((1,), (1,)), ((), ())), preferred_element_type=F32, **kw)


def _dot_tn(a, b, **kw):
    return lax.dot_general(a, b, (((0,), (0,)), ((), ())), preferred_element_type=F32, **kw)


def _split3(x):
    h = x.astype(BF16)
    r = x - h.astype(F32)
    m = r.astype(BF16)
    l = (r - m.astype(F32)).astype(BF16)
    return h, m, l


def _split2(x):
    h = x.astype(BF16)
    return h, (x - h.astype(F32)).astype(BF16)


def _seg_reduce(x, gather, scatter):
    h, l = _split2(x)
    sh, sl = _split2(_dot(h, gather) + _dot(l, gather))
    return _dot(sh, scatter) + _dot(sl, scatter)


def _dot_const_lhs(c, x):
    h, m, l = _split3(x)
    return _dot(c, h) + _dot(c, m) + _dot(c, l)


def _silu(x):
    return x * jax.nn.sigmoid(x)


def _mod_kernel(c_ref, w_ref, b_ref, o_ref):
    @pl.when(pl.program_id(1) == 0)
    def _():
        o_ref[0] = jnp.broadcast_to(b_ref[0], o_ref.shape[1:])

    xh, xl = _split2(_silu(c_ref[...]))
    wh, wl = _split2(w_ref[0])
    o_ref[0] += _dot(xh, wh) + _dot(xh, wl) + _dot(xl, wh)


def _modulation(cc, w_ada, b_ada, tk=256):
    depth = w_ada.shape[0]
    return pl.pallas_call(
        _mod_kernel,
        grid=(depth, D // tk),
        in_specs=[pl.BlockSpec((8, tk), lambda l, k: (0, k)),
                  pl.BlockSpec((1, tk, 3 * D), lambda l, k: (l, k, 0)),
                  pl.BlockSpec((1, 1, 3 * D), lambda l, k: (l, 0, 0))],
        out_specs=pl.BlockSpec((1, 8, 3 * D), lambda l, k: (l, 0, 0)),
        out_shape=jax.ShapeDtypeStruct((depth, 8, 3 * D), F32),
        compiler_params=_cparams(("parallel", "arbitrary")),
        name="adaln_modulation",
    )(cc, w_ada, b_ada.reshape(depth, 1, 3 * D))


def _stream_specs(z, tl, nct, block, index):
    if not isinstance(z, tuple):
        return [pl.BlockSpec(block, lambda *g: index(*g))], [z]

    def ctx_index(*g):
        b, i, j = index(*g)
        return b, jnp.minimum(i, nct - 1), j

    def lat_index(*g):
        b, i, j = index(*g)
        return b, jnp.maximum(i - nct, 0), j

    return [pl.BlockSpec(block, ctx_index), pl.BlockSpec(block, lat_index)], list(z)


def _stream_tile(z_refs, is_ctx):
    if len(z_refs) == 1:
        return z_refs[0][0]
    return jnp.where(is_ctx, z_refs[0][0], z_refs[1][0])


def _norm_kernel(*refs, nct, first):
    *z_refs, sc_ref, sh_ref, o_ref = refs
    x = _stream_tile(z_refs, pl.program_id(1) + first < nct)
    y = x * lax.rsqrt(jnp.mean(x * x, axis=-1, keepdims=True) + EPS)
    o_ref[0] = (y * sc_ref[0, 0] + sh_ref[0, 0]).astype(o_ref.dtype)


def _norm_mod(z, scale, shift, n_ctx, L, out_dtype, latent_only=False, tl=256):
    bsz = scale.shape[0]
    nct = n_ctx // tl
    first = nct if latent_only else 0
    seg = lambda b, i: (b, (i + first >= nct).astype(jnp.int32), 0, 0)
    z_specs, z_args = _stream_specs(z, tl, nct, (1, tl, D), lambda b, i: (b, i + first, 0))
    return pl.pallas_call(
        functools.partial(_norm_kernel, nct=nct, first=first),
        grid=(bsz, L // tl - first),
        in_specs=z_specs + [pl.BlockSpec((1, 1, 1, D), seg), pl.BlockSpec((1, 1, 1, D), seg)],
        out_specs=pl.BlockSpec((1, tl, D), lambda b, i: (b, i, 0)),
        out_shape=jax.ShapeDtypeStruct((bsz, L - first * tl, D), out_dtype),
        compiler_params=_cparams(("parallel", "parallel")),
        name="rmsnorm_modulate",
    )(*z_args, scale, shift)


def _mm_kernel(a_ref, w_ref, o_ref):
    o_ref[...] = _dot(a_ref[...], w_ref[...]).astype(o_ref.dtype)


def _matmul(a, w, out_dtype=F32, tm=2176, tn=512):
    m, k = a.shape
    n = w.shape[1]
    tm = math.gcd(m, tm)
    return pl.pallas_call(
        _mm_kernel,
        grid=(n // tn, m // tm),
        in_specs=[pl.BlockSpec((tm, k), lambda j, i: (i, 0)),
                  pl.BlockSpec((k, tn), lambda j, i: (0, j))],
        out_specs=pl.BlockSpec((tm, tn), lambda j, i: (i, j)),
        out_shape=jax.ShapeDtypeStruct((m, n), out_dtype),
        compiler_params=_cparams(("parallel", "parallel")),
        name="in_projection",
    )(a, w)


def _rwprep_kernel(r_ref, rp_ref, rn_ref, k_ref, kp_ref, kn_ref, v_ref, vp_ref, vn_ref,
                   w_ref, wp_ref, wn_ref, mur_ref, muk_ref, muv_ref, muw_ref,
                   wup_ref, w0_ref, aup_ref, a0_ref, kk_ref, ka_ref, rk_ref, gat_ref, sct_ref, tri_ref,
                   at_ref, rt_ref, bt_ref, kt_ref, vo_ref, bonus_ref, g_ref, *, tl, n_ctx, L):
    i = pl.program_id(1)
    start = i * tl
    has_prev = jnp.logical_and(start != 0, start != n_ctx).astype(F32)
    has_next = jnp.logical_and(start + tl != n_ctx, start + tl != L).astype(F32)

    def shift(x_ref, p_ref, n_ref, mu_ref):
        x = x_ref[0]
        row = lax.broadcasted_iota(jnp.int32, x.shape, 0)
        prev = jnp.where(row == 0, p_ref[0, 7:8, :] * has_prev, pltpu.roll(x, 1, axis=0))
        nxt = jnp.where(row == tl - 1, n_ref[0, 0:1, :] * has_next, pltpu.roll(x, tl - 1, axis=0))
        return x + mu_ref[0:1, :] * (prev - x) + mu_ref[1:2, :] * (nxt - x)

    r = shift(r_ref, rp_ref, rn_ref, mur_ref)
    k = shift(k_ref, kp_ref, kn_ref, muk_ref)
    v = shift(v_ref, vp_ref, vn_ref, muv_ref)
    wdad = shift(w_ref, wp_ref, wn_ref, muw_ref)
    gat, sct = gat_ref[...], sct_ref[...]

    kk = k * kk_ref[...]
    kk = kk / jnp.maximum(jnp.sqrt(_seg_reduce(kk * kk, gat, sct)), 1e-12)
    bonus_ref[0] = _seg_reduce(r * k * rk_ref[...], gat, sct) * v
    vo_ref[0] = v

    for d in range(2):
        wd = wdad[:, d * RW_LORA:(d + 1) * RW_LORA]
        ad = wdad[:, 2 * RW_LORA + d * RW_LORA:2 * RW_LORA + (d + 1) * RW_LORA]
        u = w0_ref[d:d + 1, :] + _mmx(jnp.tanh(wd), wup_ref[d], "x3")
        logw = -math.exp(-0.5) * jax.nn.sigmoid(u)
        a = jax.nn.sigmoid(a0_ref[d:d + 1, :] + _mmx(ad, aup_ref[d], "x3"))
        kd = k * (1.0 + (a - 1.0) * ka_ref[...])
        lg = _dot_const_lhs(tri_ref[d], logw)
        e_pos = jnp.exp(lg)
        e_neg = jnp.exp(-lg)
        at_ref[0, d] = -kk * jnp.exp(lg - logw)
        rt_ref[0, d] = r * e_pos
        bt_ref[0, d] = kk * a * e_neg
        kt_ref[0, d] = kd * e_neg
        for c in range(tl // RW_T):
            g_ref[0, d, c] = jnp.exp(jnp.sum(logw[c * RW_T:(c + 1) * RW_T], axis=0, keepdims=True))


def _rw_prepare(proj, p, n_ctx, tl=256):
    bsz, L, _ = proj.shape
    h8 = tl // 8
    nblk8 = L // 8
    main = lambda w, cb: pl.BlockSpec((1, tl, w), lambda b, i: (b, i, cb))
    prev = lambda w, cb: pl.BlockSpec((1, 8, w), lambda b, i: (b, jnp.maximum(i * h8 - 1, 0), cb))
    nxt = lambda w, cb: pl.BlockSpec((1, 8, w), lambda b, i: (b, jnp.minimum((i + 1) * h8, nblk8 - 1), cb))
    full = lambda shape: pl.BlockSpec(shape, lambda b, i: (0,) * len(shape))
    in_specs = []
    args = []
    for name, w in (("r", 1024), ("k", 1024), ("v", 1024), ("wdad", 256)):
        cb = COL[name] // w
        in_specs += [main(w, cb), prev(w, cb), nxt(w, cb)]
        args += [proj, proj, proj]
    mu = p["shift_mu"]
    in_specs += [pl.BlockSpec((2, 1024), lambda b, i: (0, 0)), pl.BlockSpec((2, 1024), lambda b, i: (0, 1)),
                 pl.BlockSpec((2, 1024), lambda b, i: (0, 2)), pl.BlockSpec((2, 256), lambda b, i: (0, 12))]
    args += [mu, mu, mu, mu]
    in_specs += [full((2, RW_LORA, BR)), full((2, BR)), full((2, RW_LORA, BR)), full((2, BR)),
                 full((1, BR)), full((1, BR)), full((1, BR)), full((BR, 128)), full((128, BR)), full((2, tl, tl))]
    args += [p["rw_w_up"], p["rw_w0"], p["rw_a_up"], p["rw_a0"], p["rw_k_k"].reshape(1, BR),
             p["rw_k_a"].reshape(1, BR), p["rw_r_k"].reshape(1, BR), p["sum64"], p["bcast64"], p["tri"]]
    dir_spec = pl.BlockSpec((1, 2, tl, BR), lambda b, i: (b, 0, i, 0))
    tok_spec = pl.BlockSpec((1, tl, BR), lambda b, i: (b, i, 0))
    nchunk = tl // RW_T
    out_specs = [dir_spec, dir_spec, dir_spec, dir_spec, tok_spec, tok_spec,
                 pl.BlockSpec((1, 2, nchunk, 1, BR), lambda b, i: (b, 0, i, 0, 0))]
    dir_shape = jax.ShapeDtypeStruct((bsz, 2, L, BR), F32)
    tok_shape = jax.ShapeDtypeStruct((bsz, L, BR), F32)
    out_shape = [dir_shape, dir_shape, dir_shape, dir_shape, tok_shape, tok_shape,
                 jax.ShapeDtypeStruct((bsz, 2, L // RW_T, 1, BR), F32)]
    return pl.pallas_call(
        functools.partial(_rwprep_kernel, tl=tl, n_ctx=n_ctx, L=L),
        grid=(bsz, L // tl),
        in_specs=in_specs, out_specs=out_specs, out_shape=out_shape,
        compiler_params=_cparams(("parallel", "parallel")),
        name="rwkv_prepare",
    )(*args)


def _scan_chunk(d, j, nc_ctx, nc):
    bwd = jnp.where(j < nc_ctx, nc_ctx - 1 - j, nc - 1 - j + nc_ctx)
    return jnp.where(d == 0, j, bwd)


RW_PREC = {"scores": "bf16", "inv": "bf16", "xs": "bf16", "av": "bf16", "u": "bf16", "y": "bf16", "state": "bf16"}


def _mmx(a, b, mode, dims="nn"):
    f = {"nn": _dot, "nt": _dot_nt, "tn": _dot_tn}[dims]
    if mode == "hi":
        return f(a, b, precision=HI)
    ah, bh = a.astype(BF16), b.astype(BF16)
    if mode == "bf16":
        return f(ah, bh)
    al = (a - ah.astype(F32)).astype(BF16)
    bl = (b - bh.astype(F32)).astype(BF16)
    return f(ah, bh) + f(ah, bl) + f(al, bh)


def _rwscan_kernel(atf_ref, rtf_ref, btf_ref, ktf_ref, vf_ref, gf_ref,
                   atb_ref, rtb_ref, btb_ref, ktb_ref, vb_ref, gb_ref, yf_ref, yb_ref, s_ref):
    j = pl.program_id(1)
    T = RW_T

    @pl.when(j == 0)
    def _():
        s_ref[...] = jnp.zeros_like(s_ref)

    lane = lax.broadcasted_iota(jnp.int32, (2 * T, 128), 1)
    row = lax.broadcasted_iota(jnp.int32, (2 * T, 128), 0)
    own = (lane // RW_N) == (row // T)
    r2 = lax.broadcasted_iota(jnp.int32, (2 * T, 2 * T), 0)
    c2 = lax.broadcasted_iota(jnp.int32, (2 * T, 2 * T), 1)
    same = (r2 // T) == (c2 // T)
    strict = [jnp.logical_and(same, r2 > c2), jnp.logical_and(same, r2 < c2)]
    incl = [jnp.logical_and(same, r2 >= c2), jnp.logical_and(same, r2 <= c2)]
    eye = jnp.where(r2 == c2, 1.0, 0.0)
    base_blk = (r2 // RW_INV_BASE) == (c2 // RW_INV_BASE)
    merge_blks = []
    s = RW_INV_BASE
    while s < T:
        merge_blks.append(jnp.logical_and((r2 // (2 * s)) == (c2 // (2 * s)), (r2 // s) != (c2 // s)))
        s *= 2

    def stack(x):
        return jnp.where(own, jnp.concatenate([x, x], axis=0), 0.0)

    refs = ((atf_ref, rtf_ref, btf_ref, ktf_ref, vf_ref, gf_ref, yf_ref),
            (atb_ref, rtb_ref, btb_ref, ktb_ref, vb_ref, gb_ref, yb_ref))
    chains = [(d, p) for d in range(2) for p in range(BR // 128)]
    dirs = [d for d, _ in chains]
    sls = [slice(p * 128, (p + 1) * 128) for _, p in chains]
    each = lambda fn, *lists: [fn(*xs) for xs in zip(*lists)]
    prec = RW_PREC
    AR = [jnp.concatenate([stack(refs[d][0][0, 0, :, sl]), stack(refs[d][1][0, 0, :, sl])], axis=0)
          for d, sl in zip(dirs, sls)]
    BK = [jnp.concatenate([stack(refs[d][2][0, 0, :, sl]), stack(refs[d][3][0, 0, :, sl])], axis=0)
          for d, sl in zip(dirs, sls)]
    V = [stack(refs[d][4][0, :, sl]) for d, sl in zip(dirs, sls)]
    sc = each(lambda a, b: _mmx(a, b, prec["scores"], "nt"), AR, BK)
    Aab = [jnp.where(strict[d], x[:2 * T, :2 * T], 0.0) for d, x in zip(dirs, sc)]
    Aak = [jnp.where(strict[d], x[:2 * T, 2 * T:], 0.0) for d, x in zip(dirs, sc)]
    Mrbk = [jnp.concatenate([jnp.where(incl[d], x[2 * T:, :2 * T], 0.0),
                             jnp.where(incl[d], x[2 * T:, 2 * T:], 0.0)], axis=1) for d, x in zip(dirs, sc)]

    inv = lambda a, b: _mmx(a, b, prec["inv"])
    N = [jnp.where(base_blk, x, 0.0) for x in Aab]
    P = [eye + x for x in N]
    N = each(inv, N, N)
    NP = each(lambda n, q: inv(jnp.concatenate([n, q], axis=0), n), N, P)
    P = each(lambda q, x: q + x[2 * T:], P, NP)
    P = each(lambda q, x: q + inv(q, x[:2 * T]), P, NP)
    for off_blk in merge_blks:
        CP = each(lambda a, q: inv(jnp.where(off_blk, a, 0.0), q), Aab, P)
        P = each(lambda q, x: q + inv(q, x), P, CP)

    AV = each(lambda a, v: _mmx(a, v, prec["av"]), Aak, V)
    S = [s_ref[d, p] for d, p in chains]
    XS = each(lambda a, s_: _mmx(a, s_, prec["xs"], "nt"), AR, S)
    U = each(lambda q, x, w: _mmx(q, x[:2 * T] + w, prec["u"]), P, XS, AV)
    UV = each(lambda u, v: jnp.concatenate([u, v], axis=0), U, V)
    Ys = each(lambda x, m, uv: x[2 * T:] + _mmx(m, uv, prec["y"]), XS, Mrbk, UV)
    dS = each(lambda uv, bk: _mmx(uv, bk, prec["state"], "tn"), UV, BK)
    for c, (d, p) in enumerate(chains):
        refs[d][6][0, :, sls[c]] = Ys[c][:T] + Ys[c][T:]
        s_ref[d, p] = (S[c] + dS[c]) * refs[d][5][0, 0, 0, :, sls[c]]


def _rw_scan(at, rt, bt, kt, v, g, n_ctx):
    bsz, _, L, _ = at.shape
    nc, nc_ctx = L // RW_T, n_ctx // RW_T

    def specs(d):
        ch = lambda j: _scan_chunk(d, j, nc_ctx, nc)
        dspec = pl.BlockSpec((1, 1, RW_T, BR), lambda b, j: (b, d, ch(j), 0))
        tspec = pl.BlockSpec((1, RW_T, BR), lambda b, j: (b, ch(j), 0))
        return tspec, [dspec, dspec, dspec, dspec, tspec,
                       pl.BlockSpec((1, 1, 1, 1, BR), lambda b, j: (b, d, ch(j), 0, 0))]

    (out_f, in_f), (out_b, in_b) = specs(0), specs(1)
    shape = jax.ShapeDtypeStruct((bsz, L, BR), F32)
    return pl.pallas_call(
        _rwscan_kernel,
        grid=(bsz, nc),
        in_specs=in_f + in_b,
        out_specs=[out_f, out_b],
        out_shape=[shape, shape],
        scratch_shapes=[pltpu.VMEM((2, BR // 128, 128, 128), F32)],
        compiler_params=_cparams(("parallel", "arbitrary")),
        name="rwkv_scan",
    )(at, rt, bt, kt, v, g, at, rt, bt, kt, v, g)


def _atprep_kernel(q_ref, k_ref, v_ref, cos_ref, sin_ref, qg_ref, kg_ref, qo_ref, ko_ref, vo_ref):
    cos = cos_ref[...]
    sin = sin_ref[...]
    lane = lax.broadcasted_iota(jnp.int32, cos.shape, 1)
    first_half = (lane % 64) < 32

    def norm_rope(x, g):
        y = x * lax.rsqrt(jnp.mean(x * x, axis=-1, keepdims=True) + EPS) * g
        partner = jnp.where(first_half, pltpu.roll(y, 96, axis=1), pltpu.roll(y, 32, axis=1))
        return y * cos + partner * sin

    for h in range(AT_H):
        sl = slice(h * AT_HD, (h + 1) * AT_HD)
        q_scale = AT_HD ** -0.5 * math.log2(math.e)
        qo_ref[0, :, sl] = (norm_rope(q_ref[0, :, sl], qg_ref[...]) * q_scale).astype(qo_ref.dtype)
    for h in range(AT_KV):
        sl = slice(h * AT_HD, (h + 1) * AT_HD)
        ko_ref[0, :, sl] = norm_rope(k_ref[0, :, sl], kg_ref[...]).astype(ko_ref.dtype)
    vo_ref[0] = v_ref[0].astype(vo_ref.dtype)


def _at_prepare(proj, cos, sin, q_g, k_g, tl=256):
    bsz, L, _ = proj.shape
    kvw = AT_KV * AT_HD
    tok = lambda w: pl.BlockSpec((1, tl, w), lambda b, i: (b, i, 0))
    return pl.pallas_call(
        _atprep_kernel,
        grid=(bsz, L // tl),
        in_specs=[pl.BlockSpec((1, tl, BR), lambda b, i: (b, i, COL["at_q"] // BR)),
                  pl.BlockSpec((1, tl, kvw), lambda b, i: (b, i, COL["at_k"] // kvw)),
                  pl.BlockSpec((1, tl, kvw), lambda b, i: (b, i, COL["at_v"] // kvw)),
                  pl.BlockSpec((tl, AT_HD), lambda b, i: (i, 0)),
                  pl.BlockSpec((tl, AT_HD), lambda b, i: (i, 0)),
                  pl.BlockSpec((1, AT_HD), lambda b, i: (0, 0)),
                  pl.BlockSpec((1, AT_HD), lambda b, i: (0, 0))],
        out_specs=[tok(BR), tok(kvw), tok(kvw)],
        out_shape=[jax.ShapeDtypeStruct((bsz, L, BR), BF16),
                   jax.ShapeDtypeStruct((bsz, L, kvw), BF16),
                   jax.ShapeDtypeStruct((bsz, L, kvw), BF16)],
        compiler_params=_cparams(("parallel", "parallel")),
        name="gqa_prepare",
    )(proj, proj, proj, cos, sin, q_g.reshape(1, AT_HD), k_g.reshape(1, AT_HD))


def _attn_kernel(q_ref, k_ref, v_ref, o_ref, s_ref, p_ref, l_ref, *, tq, n_ctx, kb, rows):
    i = pl.program_id(2)
    grp = AT_H // AT_KV
    def attend(n_keys):
        chunks = [slice(c * kb, (c + 1) * kb) for c in range(n_keys // kb)]

        def scores(r):
            s_ref[r, :, :n_keys] = _dot_nt(q_ref[0, :, r * AT_HD:(r + 1) * AT_HD], k_ref[0, :n_keys, :])

        def softmax(r):
            for rb in range(tq // rows):
                rs = slice(rb * rows, (rb + 1) * rows)
                mx = s_ref[r, rs, chunks[0]]
                for ch in chunks[1:]:
                    mx = jnp.maximum(mx, s_ref[r, rs, ch])
                m = jnp.broadcast_to(jnp.max(mx, axis=-1, keepdims=True), mx.shape)
                tot = jnp.zeros_like(mx)
                for ch in chunks:
                    p = jnp.exp2(s_ref[r, rs, ch] - m)
                    tot = tot + p
                    p_ref[r, rs, ch] = p.astype(BF16)
                l_ref[r, rs, :] = jnp.broadcast_to(jnp.sum(tot, axis=-1, keepdims=True), (rows, AT_HD))

        def values(r):
            o_ref[0, :, r * AT_HD:(r + 1) * AT_HD] = _dot(p_ref[r, :, :n_keys], v_ref[0, :n_keys, :]) / l_ref[r]

        stages = (scores, softmax, values)
        for t in range(grp + len(stages) - 1):
            for st, fn in enumerate(stages):
                if 0 <= t - st < grp:
                    fn(t - st)

    @pl.when(i * tq < n_ctx)
    def _():
        attend(n_ctx)

    @pl.when(i * tq >= n_ctx)
    def _():
        attend(k_ref.shape[1])


def _attention(q, k, v, n_ctx, tq=256, kb=256, rows=32):
    bsz, L, _ = q.shape
    gw = (AT_H // AT_KV) * AT_HD
    return pl.pallas_call(
        functools.partial(_attn_kernel, tq=tq, n_ctx=n_ctx, kb=kb, rows=rows),
        scratch_shapes=[pltpu.VMEM((gw // AT_HD, tq, L), F32), pltpu.VMEM((gw // AT_HD, tq, L), BF16),
                        pltpu.VMEM((gw // AT_HD, tq, AT_HD), F32)],
        grid=(bsz, AT_KV, L // tq),
        in_specs=[pl.BlockSpec((1, tq, gw), lambda b, g, i: (b, i, g)),
                  pl.BlockSpec((1, L, AT_HD), lambda b, g, i: (b, 0, g)),
                  pl.BlockSpec((1, L, AT_HD), lambda b, g, i: (b, 0, g))],
        out_specs=pl.BlockSpec((1, tq, gw), lambda b, g, i: (b, i, g)),
        out_shape=jax.ShapeDtypeStruct((bsz, L, BR), F32),
        compiler_params=_cparams(("parallel", "parallel", "parallel")),
        name="gqa_attention",
    )(q, k, v)


def _cap_gates(pre):
    return GATE_CAP * jnp.tanh(pre / GATE_CAP)


def _log_sigmoid(x):
    return jnp.minimum(x, 0.0) - jnp.log1p(jnp.exp(-jnp.abs(x)))


def _mlstm_kernel(qf_ref, kf_ref, vf_ref, gcf_ref, qb_ref, kb_ref, vb_ref, gcb_ref,
                  bc_ref, br_ref, hf_ref, hb_ref, c_ref, n_ref, m_ref):
    j = pl.program_id(1)
    T = ML_T

    @pl.when(j == 0)
    def _():
        c_ref[...] = jnp.zeros_like(c_ref)
        n_ref[...] = jnp.zeros_like(n_ref)
        m_ref[...] = jnp.zeros_like(m_ref)

    r2 = lax.broadcasted_iota(jnp.int32, (T, T), 0)
    c2 = lax.broadcasted_iota(jnp.int32, (T, T), 1)
    lane16 = lax.broadcasted_iota(jnp.int32, (T, 16), 1)
    sub16 = lax.broadcasted_iota(jnp.int32, (16, T), 0)

    refs = ((qf_ref, kf_ref, vf_ref, gcf_ref, None, hf_ref), (qb_ref, kb_ref, vb_ref, gcb_ref, None, hb_ref))
    seen = [c2 <= r2, c2 >= r2]
    seen_t = [r2 <= c2, r2 >= c2]
    gc = [_cap_gates(refs[d][3][0][:, :16] + bc_ref[...]) for d in range(2)]
    gr = [_cap_gates(refs[d][3][0].T[:16, :] + br_ref[...]) for d in range(2)]
    lsc = [_log_sigmoid(x) for x in gc]
    lsr = [_log_sigmoid(x) for x in gr]

    chains = [(d, h) for d in range(2) for h in range(ML_H)]
    each = lambda fn, *lists: [fn(*xs) for xs in zip(*lists)]
    pick_row = lambda x, idx: jnp.sum(jnp.where(sub16 == idx, x, 0.0), axis=0, keepdims=True)
    pick_col = lambda x, idx: jnp.sum(jnp.where(lane16 == idx, x, 0.0), axis=1, keepdims=True)
    li_row = [pick_row(gr[d], d * ML_H + h) for d, h in chains]
    lf_row = [pick_row(lsr[d], (2 + d) * ML_H + h) for d, h in chains]
    li_col = [pick_col(gc[d], d * ML_H + h) for d, h in chains]
    lf_col = [pick_col(lsc[d], (2 + d) * ML_H + h) for d, h in chains]
    b_col = [jnp.sum(jnp.where(seen[d], x, 0.0), axis=1, keepdims=True) for (d, _), x in zip(chains, lf_row)]
    b_row = [jnp.sum(jnp.where(seen_t[d], x, 0.0), axis=0, keepdims=True) for (d, _), x in zip(chains, lf_col)]
    g = [jnp.sum(x, axis=0, keepdims=True) for x in lf_col]
    m_prev = [m_ref[d, h] for d, h in chains]
    q = [refs[d][0][0, :, h * ML_DK:(h + 1) * ML_DK] * (ML_DK ** -0.5) for d, h in chains]
    k = [refs[d][1][0, :, h * ML_DK:(h + 1) * ML_DK] for d, h in chains]
    vb = [refs[d][2][0, :, h * ML_DV:(h + 1) * ML_DV].astype(BF16) for d, h in chains]
    C = [c_ref[d, h] for d, h in chains]
    n = [n_ref[d, h] for d, h in chains]
    qb = [x.astype(BF16) for x in q]

    dmat = [jnp.where(seen[d], bc - br + li, -jnp.inf) for (d, _), bc, br, li in zip(chains, b_col, b_row, li_row)]
    m_inter = each(lambda bc, m: bc + m, b_col, m_prev)
    m_t = each(lambda mi, dm: jnp.maximum(mi, jnp.max(dm, axis=-1, keepdims=True)), m_inter, dmat)
    w_inter = each(lambda mi, mt: jnp.exp(mi - mt), m_inter, m_t)
    qk = each(lambda a, b: _dot_nt(a, b.astype(BF16)), qb, k)
    qc = each(lambda a, b: _dot(a, b.astype(BF16)), qb, C)
    s = each(lambda x, dm, mt: x * jnp.exp(dm - mt), qk, dmat, m_t)
    sv = each(lambda a, b: _dot(a.astype(BF16), b), s, vb)
    qn = each(lambda a, b: jnp.sum(a * b, axis=-1, keepdims=True), q, n)
    den = each(lambda w, a, x: w * a + jnp.sum(x, axis=-1, keepdims=True), w_inter, qn, s)
    for (d, h), w, a, b, dn, mt in zip(chains, w_inter, qc, sv, den, m_t):
        refs[d][5][0, :, h * ML_DV:(h + 1) * ML_DV] = (w * a + b) / jnp.maximum(jnp.abs(dn), jnp.exp(-mt))

    loga = each(lambda g_, bc, li: g_ - bc + li, g, b_col, li_col)
    m_new = each(lambda g_, m, la: jnp.maximum(g_ + m, jnp.max(la, axis=0, keepdims=True)), g, m_prev, loga)
    carry = each(lambda g_, m, mn: jnp.exp(g_ + m - mn), g, m_prev, m_new)
    wk = each(lambda la, mn, k_: jnp.exp(la - mn) * k_, loga, m_new, k)
    kv = each(lambda a, b: _dot_tn(a.astype(BF16), b), wk, vb)
    for (d, h), cr, c_, kv_, n_, wk_, mn in zip(chains, carry, C, kv, n, wk, m_new):
        c_ref[d, h] = cr * c_ + kv_
        n_ref[d, h] = cr * n_ + jnp.sum(wk_, axis=0, keepdims=True)
        m_ref[d, h] = mn


def _mlstm(proj, bias_col, bias_row, n_ctx):
    bsz, L, _ = proj.shape
    nc, nc_ctx = L // ML_T, n_ctx // ML_T
    qw, vw = ML_H * ML_DK, ML_H * ML_DV

    def dir_specs(d):
        ch = lambda j: _scan_chunk(d, j, nc_ctx, nc)
        return [pl.BlockSpec((1, ML_T, qw), lambda b, j: (b, ch(j), COL["ml_q"] // qw)),
                pl.BlockSpec((1, ML_T, qw), lambda b, j: (b, ch(j), COL["ml_k"] // qw)),
                pl.BlockSpec((1, ML_T, vw), lambda b, j: (b, ch(j), COL["ml_v"] // vw)),
                pl.BlockSpec((1, ML_T, 128), lambda b, j: (b, ch(j), COL["ml_if"] // 128))]

    def out_spec(d):
        ch = lambda j: _scan_chunk(d, j, nc_ctx, nc)
        return pl.BlockSpec((1, ML_T, vw), lambda b, j: (b, ch(j), 0))

    shape = jax.ShapeDtypeStruct((bsz, L, vw), F32)
    return pl.pallas_call(
        _mlstm_kernel,
        grid=(bsz, nc),
        in_specs=dir_specs(0) + dir_specs(1) + [pl.BlockSpec((1, 16), lambda b, j: (0, 0)),
                                                pl.BlockSpec((16, 1), lambda b, j: (0, 0))],
        out_specs=[out_spec(0), out_spec(1)],
        out_shape=[shape, shape],
        scratch_shapes=[pltpu.VMEM((2, ML_H, ML_DK, ML_DV), F32),
                        pltpu.VMEM((2, ML_H, 1, ML_DK), F32),
                        pltpu.VMEM((2, ML_H, 1, 1), F32)],
        compiler_params=_cparams(("parallel", "arbitrary")),
        name="mlstm_scan",
    )(proj, proj, proj, proj, proj, proj, proj, proj, bias_col, bias_row)


def _epilogue_kernel(yf_ref, yb_ref, bonus_ref, rwg_ref, att_ref, atg_ref, hf_ref, hb_ref, mlo_ref, mlg_ref,
                     lnw_ref, lnb_ref, mng_ref, m64_ref, b64_ref, m256_ref, b256_ref, o_ref):
    y = yf_ref[0] + yb_ref[0]
    mu = _seg_reduce(y, m64_ref[...], b64_ref[...])
    yc = y - mu
    var = _seg_reduce(yc * yc, m64_ref[...], b64_ref[...])
    ya = yc * lax.rsqrt(var + RW_GN_EPS) * lnw_ref[...] + lnb_ref[...] + bonus_ref[0]
    o_ref[0, 0] = (ya * _silu(rwg_ref[0])).astype(o_ref.dtype)

    o_ref[1, 0] = (att_ref[0] * _silu(atg_ref[0])).astype(o_ref.dtype)

    hh = hf_ref[0] + hb_ref[0]
    ms = _seg_reduce(hh * hh, m256_ref[...], b256_ref[...])
    hn = hh * lax.rsqrt(ms + EPS) * mng_ref[...]
    o_ref[2, 0] = (jax.nn.sigmoid(mlo_ref[0]) * hn * _silu(mlg_ref[0])).astype(o_ref.dtype)


def _epilogue(proj, y_f, y_b, bonus, att, h_f, h_b, p, tl=256):
    bsz, L, _ = proj.shape
    tok = pl.BlockSpec((1, tl, BR), lambda b, i: (b, i, 0))
    pc = lambda name: pl.BlockSpec((1, tl, BR), lambda b, i: (b, i, COL[name] // BR))
    vec = pl.BlockSpec((1, BR), lambda b, i: (0, 0))
    gat = pl.BlockSpec((BR, 128), lambda b, i: (0, 0))
    sct = pl.BlockSpec((128, BR), lambda b, i: (0, 0))
    return pl.pallas_call(
        _epilogue_kernel,
        grid=(bsz, L // tl),
        in_specs=[tok, tok, tok, pc("rw_g"), tok, pc("at_g"), tok, tok, pc("ml_o"), pc("ml_g"),
                  vec, vec, vec, gat, sct, gat, sct],
        out_specs=pl.BlockSpec((3, 1, tl, BR), lambda b, i: (0, b, i, 0)),
        out_shape=jax.ShapeDtypeStruct((3, bsz, L, BR), BF16),
        compiler_params=_cparams(("parallel", "parallel")),
        name="branch_epilogue",
    )(y_f, y_b, bonus, proj, att, proj, h_f, h_b, proj, proj,
      p["rw_ln_w"].reshape(1, BR), p["rw_ln_b"].reshape(1, BR), p["ml_norm_g"].reshape(1, BR),
      p["mean64"], p["bcast64"], p["mean256"], p["bcast256"])


def _merge_kernel(y_ref, w_ref, g0_ref, g1_ref, g2_ref, o_ref):
    acc = jax.nn.sigmoid(g0_ref[...]) * _dot(y_ref[0], w_ref[0])
    acc += jax.nn.sigmoid(g1_ref[...]) * _dot(y_ref[1], w_ref[1])
    acc += jax.nn.sigmoid(g2_ref[...]) * _dot(y_ref[2], w_ref[2])
    o_ref[...] = acc.astype(o_ref.dtype)


def _merge(ys, w_branch, proj2d, tm=512, tn=1024):
    _, m, _ = ys.shape
    nb = D // tn
    gate = lambda n: pl.BlockSpec((tm, tn), lambda j, i: (i, n * nb + j))
    return pl.pallas_call(
        _merge_kernel,
        grid=(D // tn, m // tm),
        in_specs=[pl.BlockSpec((3, tm, BR), lambda j, i: (0, i, 0)),
                  pl.BlockSpec((3, BR, tn), lambda j, i: (0, 0, j)),
                  gate(0), gate(1), gate(2)],
        out_specs=pl.BlockSpec((tm, tn), lambda j, i: (i, j)),
        out_shape=jax.ShapeDtypeStruct((m, D), BF16),
        compiler_params=_cparams(("parallel", "parallel")),
        name="branch_merge",
    )(ys, w_branch, proj2d, proj2d, proj2d)


def _outproj_kernel(*refs, nct):
    a_ref, w_ref, gt_ref, *z_refs, o_ref = refs
    z = _stream_tile(z_refs, pl.program_id(2) < nct)
    o_ref[0] = z + gt_ref[0, 0] * _dot(a_ref[0], w_ref[...])


def _out_projection(mixed, w_out, z, gate, n_ctx, tl=256, tn=1024):
    bsz, L, _ = mixed.shape
    nct = n_ctx // tl
    z_specs, z_args = _stream_specs(z, tl, nct, (1, tl, tn), lambda j, b, i: (b, i, j))
    return pl.pallas_call(
        functools.partial(_outproj_kernel, nct=nct),
        grid=(D // tn, bsz, L // tl),
        in_specs=[pl.BlockSpec((1, tl, D), lambda j, b, i: (b, i, 0)),
                  pl.BlockSpec((D, tn), lambda j, b, i: (0, j)),
                  pl.BlockSpec((1, 1, 1, tn), lambda j, b, i: (b, (i >= nct).astype(jnp.int32), 0, j))] + z_specs,
        out_specs=pl.BlockSpec((1, tl, tn), lambda j, b, i: (b, i, j)),
        out_shape=jax.ShapeDtypeStruct((bsz, L, D), F32),
        compiler_params=_cparams(("parallel", "parallel", "parallel")),
        name="out_projection",
    )(mixed, w_out, gate, *z_args)


def _permute_w_in(w):
    cols = [w[:, s:s + n] for _, s, n in _PERM]
    cols.append(jnp.zeros((w.shape[0], N_PROJ - 17168), w.dtype))
    return jnp.concatenate(cols, axis=1).astype(BF16)


def _rope_tables(n_ctx, n_lat):
    rows = n_lat // GRID_W
    row = jnp.repeat(jnp.arange(rows), GRID_W).astype(F32)
    col = jnp.tile(jnp.arange(GRID_W), rows).astype(F32)
    inv_freq = ROPE_THETA ** (-jnp.arange(0, AT_HD // 2, 2, dtype=F32) / (AT_HD // 2))
    ang_lat = jnp.stack([row[:, None] * inv_freq, col[:, None] * inv_freq], axis=1)
    ang = jnp.concatenate([jnp.zeros((n_ctx, 2, AT_HD // 4), F32), ang_lat], axis=0)
    cos, sin = jnp.cos(ang), jnp.sin(ang)
    cos_t = jnp.concatenate([cos[:, 0], cos[:, 0], cos[:, 1], cos[:, 1]], axis=-1)
    sin_t = jnp.concatenate([-sin[:, 0], sin[:, 0], -sin[:, 1], sin[:, 1]], axis=-1)
    return cos_t, sin_t


def _group_consts(width, value):
    member = (np.arange(BR)[:, None] // width) == np.arange(128)[None, :]
    return (jnp.asarray(np.where(member, value, 0.0), dtype=BF16),
            jnp.asarray(np.where(member.T, 1.0, 0.0), dtype=BF16))


def _chunk_tri(tl):
    t = np.arange(tl)
    same = (t[:, None] // RW_T) == (t[None, :] // RW_T)
    fwd = same & (t[None, :] <= t[:, None])
    bwd = same & (t[None, :] >= t[:, None])
    return jnp.asarray(np.stack([fwd, bwd]).astype(np.float32), dtype=BF16)


def kernel(x, c, ctx, c_ctx, norm_g, w_ada, b_ada, w_in, shift_mu, rw_w_up, rw_w0, rw_a_up, rw_a0, rw_k_k, rw_k_a, rw_r_k, rw_ln_w, rw_ln_b, at_q_g, at_k_g, ml_gate_b, ml_norm_g, w_branch, w_out, final_g):
    bsz, n_lat, _ = x.shape
    n_ctx = ctx.shape[1]
    L = n_ctx + n_lat
    depth = w_in.shape[0]

    cos_t, sin_t = _rope_tables(n_ctx, n_lat)
    sum64, bcast64 = _group_consts(RW_N, 1.0)
    mean64, _ = _group_consts(RW_N, 1.0 / RW_N)
    mean256, bcast256 = _group_consts(ML_DV, 1.0 / ML_DV)
    consts = {"sum64": sum64, "bcast64": bcast64, "mean64": mean64, "mean256": mean256, "bcast256": bcast256,
              "tri": _chunk_tri(256)}

    cc = jnp.concatenate([c, c_ctx[None], jnp.zeros((8 - bsz - 1, D), F32)], axis=0)
    mod = _modulation(cc, w_ada, b_ada)

    z = (ctx, x)
    for l in range(depth):
        sh, sc, gt = mod[l, :, :D], mod[l, :, D:2 * D], mod[l, :, 2 * D:]
        pick = lambda t: jnp.stack([jnp.broadcast_to(t[bsz], (bsz, D)), t[:bsz]], axis=1)[:, :, None, :]
        scale = pick((1.0 + sc) * norm_g[l])
        shift = pick(sh)
        gate = pick(gt)

        h = _norm_mod(z, scale, shift, n_ctx, L, BF16)
        proj2d = _matmul(h.reshape(bsz * L, D), _permute_w_in(w_in[l]))
        proj = proj2d.reshape(bsz, L, N_PROJ)

        p = dict(consts, shift_mu=shift_mu[l], rw_w_up=rw_w_up[l], rw_w0=rw_w0[l], rw_a_up=rw_a_up[l],
                 rw_a0=rw_a0[l], rw_k_k=rw_k_k[l], rw_k_a=rw_k_a[l], rw_r_k=rw_r_k[l], rw_ln_w=rw_ln_w[l],
                 rw_ln_b=rw_ln_b[l], ml_norm_g=ml_norm_g[l])
        at, rt, bt, kt, v_rw, bonus, g_rw = _rw_prepare(proj, p, n_ctx)
        y_f, y_b = _rw_scan(at, rt, bt, kt, v_rw, g_rw, n_ctx)

        qn, kn, vn = _at_prepare(proj, cos_t, sin_t, at_q_g[l], at_k_g[l])
        att = _attention(qn, kn, vn, n_ctx)

        bias = ml_gate_b[l].reshape(16)
        h_f, h_b = _mlstm(proj, bias.reshape(1, 16), bias.reshape(16, 1), n_ctx)

        ys = _epilogue(proj, y_f, y_b, bonus, att, h_f, h_b, p)
        mixed = _merge(ys.reshape(3, bsz * L, BR), w_branch[l].astype(BF16), proj2d)
        z = _out_projection(mixed.reshape(bsz, L, D), w_out[l].astype(BF16), z, gate, n_ctx)

    ones = jnp.ones((bsz, 2, 1, D), F32) * final_g
    zeros = jnp.zeros((bsz, 2, 1, D), F32)
    return _norm_mod(z, ones, zeros, n_ctx, L, F32, latent_only=True)
```

```python
import functools
import math

import numpy as np
import jax
import jax.numpy as jnp
from jax import lax
from jax.experimental import pallas as pl
from jax.experimental.pallas import tpu as pltpu

F32 = jnp.float32
BF16 = jnp.bfloat16
HI = lax.Precision.HIGHEST

D = 2048
BR = 1024
EPS = 1e-6
GRID_W = 64

RW_H, RW_N, RW_LORA = 16, 64, 64
RW_GN_EPS = 64e-5
RW_T = 64
RW_INV_BASE = 8

AT_H, AT_KV, AT_HD = 8, 2, 128
ROPE_THETA = 10000.0

ML_H, ML_DK, ML_DV, ML_T = 4, 128, 256, 128
GATE_CAP = 15.0

D_IN = 17168
LANES = 128
PROJ_TN = 512
_PROJ_GROUPS = (
    ("merge", 11024, 6144), ("r", 0, 1024), ("k", 1024, 1024), ("v", 2048, 1024),
    ("rw_g", 3328, 1024), ("at_q", 4352, 1024), ("at_g", 5888, 1024), ("ml_v", 7936, 1024),
    ("ml_o", 8960, 1024), ("ml_g", 10000, 1024), ("ml_q", 6912, 512), ("ml_k", 7424, 512),
    ("at_k", 5376, 512), ("wdad", 3072, 512), ("ml_if", 9984, 512),
)
COL = {}
PROJ_SRC = []
for _name, _start, _width in _PROJ_GROUPS:
    COL[_name] = len(PROJ_SRC) * PROJ_TN
    PROJ_SRC += [_start + t * PROJ_TN for t in range(_width // PROJ_TN)]
COL["at_v"] = COL["at_k"] + 256
N_PROJ = len(PROJ_SRC) * PROJ_TN
PROJ_WIN = PROJ_TN + LANES
PROJ_SHIFT = 16
assert all(s % LANES in (0, PROJ_SHIFT) for s in PROJ_SRC)
for _name, _blk in (("merge", 2048), ("r", 1024), ("k", 1024), ("v", 1024), ("rw_g", 1024), ("at_q", 1024),
                    ("at_g", 1024), ("ml_v", 1024), ("ml_o", 1024), ("ml_g", 1024), ("ml_q", 512), ("ml_k", 512),
                    ("at_k", 256), ("at_v", 256), ("wdad", 256), ("ml_if", 128)):
    assert COL[_name] % _blk == 0, _name

VMEM_LIMIT = 48 * 1024 * 1024


def _cparams(sem):
    return pltpu.CompilerParams(dimension_semantics=sem, vmem_limit_bytes=VMEM_LIMIT)


def _dot(a, b, **kw):
    return jnp.dot(a, b, preferred_element_type=F32, **kw)


def _dot_nt(a, b, **kw):
    return lax.dot_general(a, b, (((1,), (1,)), ((), ())), preferred_element_type=F32, **kw)


def _dot_tn(a, b, **kw):
    return lax.dot_general(a, b, (((0,), (0,)), ((), ())), preferred_element_type=F32, **kw)


def _split3(x):
    h = x.astype(BF16)
    r = x - h.astype(F32)
    m = r.astype(BF16)
    l = (r - m.astype(F32)).astype(BF16)
    return h, m, l


def _split2(x):
    h = x.astype(BF16)
    return h, (x - h.astype(F32)).astype(BF16)


def _seg_reduce(x, gather, scatter):
    h, l = _split2(x)
    sh, sl = _split2(_dot(h, gather) + _dot(l, gather))
    return _dot(sh, scatter) + _dot(sl, scatter)


def _dot_const_lhs(c, x):
    h, m, l = _split3(x)
    return _dot(c, h) + _dot(c, m) + _dot(c, l)


def _silu(x):
    return x * jax.nn.sigmoid(x)


def _mod_kernel(c_ref, w_ref, b_ref, o_ref):
    @pl.when(pl.program_id(1) == 0)
    def _():
        o_ref[0] = jnp.broadcast_to(b_ref[0], o_ref.shape[1:])

    xh, xl = _split2(_silu(c_ref[...]))
    wh, wl = _split2(w_ref[0])
    o_ref[0] += _dot(xh, wh) + _dot(xh, wl) + _dot(xl, wh)


def _modulation(cc, w_ada, b_ada, tk=256):
    depth = w_ada.shape[0]
    return pl.pallas_call(
        _mod_kernel,
        grid=(depth, D // tk),
        in_specs=[pl.BlockSpec((8, tk), lambda l, k: (0, k)),
                  pl.BlockSpec((1, tk, 3 * D), lambda l, k: (l, k, 0)),
                  pl.BlockSpec((1, 1, 3 * D), lambda l, k: (l, 0, 0))],
        out_specs=pl.BlockSpec((1, 8, 3 * D), lambda l, k: (l, 0, 0)),
        out_shape=jax.ShapeDtypeStruct((depth, 8, 3 * D), F32),
        compiler_params=_cparams(("parallel", "arbitrary")),
        name="adaln_modulation",
    )(cc, w_ada, b_ada.reshape(depth, 1, 3 * D))


def _stream_specs(z, tl, nct, block, index):
    if not isinstance(z, tuple):
        return [pl.BlockSpec(block, lambda *g: index(*g))], [z]

    def ctx_index(*g):
        b, i, j = index(*g)
        return b, jnp.minimum(i, nct - 1), j

    def lat_index(*g):
        b, i, j = index(*g)
        return b, jnp.maximum(i - nct, 0), j

    return [pl.BlockSpec(block, ctx_index), pl.BlockSpec(block, lat_index)], list(z)


def _stream_tile(z_refs, is_ctx):
    if len(z_refs) == 1:
        return z_refs[0][0]
    return jnp.where(is_ctx, z_refs[0][0], z_refs[1][0])


def _norm_kernel(*refs, nct, first):
    *z_refs, sc_ref, sh_ref, o_ref = refs
    x = _stream_tile(z_refs, pl.program_id(1) + first < nct)
    y = x * lax.rsqrt(jnp.mean(x * x, axis=-1, keepdims=True) + EPS)
    o_ref[0] = (y * sc_ref[0, 0] + sh_ref[0, 0]).astype(o_ref.dtype)


def _norm_mod(z, scale, shift, n_ctx, L, out_dtype, latent_only=False, tl=256):
    bsz = scale.shape[0]
    nct = n_ctx // tl
    first = nct if latent_only else 0
    seg = lambda b, i: (b, (i + first >= nct).astype(jnp.int32), 0, 0)
    z_specs, z_args = _stream_specs(z, tl, nct, (1, tl, D), lambda b, i: (b, i + first, 0))
    return pl.pallas_call(
        functools.partial(_norm_kernel, nct=nct, first=first),
        grid=(bsz, L // tl - first),
        in_specs=z_specs + [pl.BlockSpec((1, 1, 1, D), seg), pl.BlockSpec((1, 1, 1, D), seg)],
        out_specs=pl.BlockSpec((1, tl, D), lambda b, i: (b, i, 0)),
        out_shape=jax.ShapeDtypeStruct((bsz, L - first * tl, D), out_dtype),
        compiler_params=_cparams(("parallel", "parallel")),
        name="rmsnorm_modulate",
    )(*z_args, scale, shift)


_TILE_ALIGNED, _TILE_SHIFTED, _TILE_TAIL = 0, 1, 2


def _proj_tables():
    start, kind = [], []
    for s in PROJ_SRC:
        if s % LANES == 0:
            start.append(s // LANES), kind.append(_TILE_ALIGNED)
        elif s - PROJ_SHIFT + PROJ_WIN <= D_IN:
            start.append((s - PROJ_SHIFT) // LANES), kind.append(_TILE_SHIFTED)
        else:
            assert s - (D_IN - PROJ_WIN) == LANES
            start.append(0), kind.append(_TILE_TAIL)
    return np.array(start, np.int32), np.array(kind, np.int32)


def _inproj_kernel(start_ref, kind_ref, a_ref, w_ref, tail_ref, o_ref):
    kind = kind_ref[pl.program_id(1)]

    @pl.when(kind == _TILE_ALIGNED)
    def _():
        o_ref[...] = _dot(a_ref[...], w_ref[...][:, :PROJ_TN].astype(BF16))

    @pl.when(kind == _TILE_SHIFTED)
    def _():
        o_ref[...] = _dot(a_ref[...], w_ref[...][:, PROJ_SHIFT:PROJ_SHIFT + PROJ_TN].astype(BF16))

    @pl.when(kind == _TILE_TAIL)
    def _():
        o_ref[...] = _dot(a_ref[...], tail_ref[...][:, LANES:LANES + PROJ_TN].astype(BF16))


def _in_projection(a, w, tm=2176):
    m, k = a.shape
    tm = math.gcd(m, tm)
    start, kind = _proj_tables()
    grid_spec = pltpu.PrefetchScalarGridSpec(
        num_scalar_prefetch=2,
        grid=(m // tm, len(PROJ_SRC)),
        in_specs=[pl.BlockSpec((tm, k), lambda i, j, st, kd: (i, 0)),
                  pl.BlockSpec((pl.Element(k), pl.Element(PROJ_WIN)), lambda i, j, st, kd: (0, st[j] * LANES)),
                  pl.BlockSpec((k, PROJ_WIN), lambda i, j, st, kd: (0, 0))],
        out_specs=pl.BlockSpec((tm, PROJ_TN), lambda i, j, st, kd: (i, j)),
    )
    return pl.pallas_call(
        _inproj_kernel,
        grid_spec=grid_spec,
        out_shape=jax.ShapeDtypeStruct((m, N_PROJ), F32),
        compiler_params=_cparams(("parallel", "arbitrary")),
        name="in_projection",
    )(jnp.asarray(start), jnp.asarray(kind), a, w, w[:, D_IN - PROJ_WIN:])


def _rwprep_kernel(r_ref, rp_ref, rn_ref, k_ref, kp_ref, kn_ref, v_ref, vp_ref, vn_ref,
                   w_ref, wp_ref, wn_ref, mur_ref, muk_ref, muv_ref, muw_ref,
                   wup_ref, w0_ref, aup_ref, a0_ref, kk_ref, ka_ref, rk_ref, gat_ref, sct_ref, tri_ref,
                   at_ref, rt_ref, bt_ref, kt_ref, vo_ref, bonus_ref, g_ref, *, tl, n_ctx, L):
    i = pl.program_id(1)
    start = i * tl
    has_prev = jnp.logical_and(start != 0, start != n_ctx).astype(F32)
    has_next = jnp.logical_and(start + tl != n_ctx, start + tl != L).astype(F32)

    def shift(x_ref, p_ref, n_ref, mu_ref):
        x = x_ref[0]
        row = lax.broadcasted_iota(jnp.int32, x.shape, 0)
        prev = jnp.where(row == 0, p_ref[0, 7:8, :] * has_prev, pltpu.roll(x, 1, axis=0))
        nxt = jnp.where(row == tl - 1, n_ref[0, 0:1, :] * has_next, pltpu.roll(x, tl - 1, axis=0))
        return x + mu_ref[0:1, :] * (prev - x) + mu_ref[1:2, :] * (nxt - x)

    r = shift(r_ref, rp_ref, rn_ref, mur_ref)
    k = shift(k_ref, kp_ref, kn_ref, muk_ref)
    v = shift(v_ref, vp_ref, vn_ref, muv_ref)
    wdad = shift(w_ref, wp_ref, wn_ref, muw_ref)
    gat, sct = gat_ref[...], sct_ref[...]

    kk = k * kk_ref[...]
    kk = kk / jnp.maximum(jnp.sqrt(_seg_reduce(kk * kk, gat, sct)), 1e-12)
    bonus_ref[0] = _seg_reduce(r * k * rk_ref[...], gat, sct) * v
    vo_ref[0] = v.astype(vo_ref.dtype)

    for d in range(2):
        wd = wdad[:, d * RW_LORA:(d + 1) * RW_LORA]
        ad = wdad[:, 2 * RW_LORA + d * RW_LORA:2 * RW_LORA + (d + 1) * RW_LORA]
        u = w0_ref[d:d + 1, :] + _mmx(jnp.tanh(wd), wup_ref[d], "x3")
        logw = -math.exp(-0.5) * jax.nn.sigmoid(u)
        a = jax.nn.sigmoid(a0_ref[d:d + 1, :] + _mmx(ad, aup_ref[d], "x3"))
        kd = k * (1.0 + (a - 1.0) * ka_ref[...])
        lg = _dot_const_lhs(tri_ref[d], logw)
        e_pos = jnp.exp(lg)
        e_neg = jnp.exp(-lg)
        at_ref[0, d] = (-kk * jnp.exp(lg - logw)).astype(at_ref.dtype)
        rt_ref[0, d] = (r * e_pos).astype(rt_ref.dtype)
        bt_ref[0, d] = (kk * a * e_neg).astype(bt_ref.dtype)
        kt_ref[0, d] = (kd * e_neg).astype(kt_ref.dtype)
        for c in range(tl // RW_T):
            g_ref[0, d, c] = jnp.exp(jnp.sum(logw[c * RW_T:(c + 1) * RW_T], axis=0, keepdims=True))


def _rw_prepare(proj, p, n_ctx, tl=256):
    bsz, L, _ = proj.shape
    h8 = tl // 8
    nblk8 = L // 8
    main = lambda w, cb: pl.BlockSpec((1, tl, w), lambda b, i: (b, i, cb))
    prev = lambda w, cb: pl.BlockSpec((1, 8, w), lambda b, i: (b, jnp.maximum(i * h8 - 1, 0), cb))
    nxt = lambda w, cb: pl.BlockSpec((1, 8, w), lambda b, i: (b, jnp.minimum((i + 1) * h8, nblk8 - 1), cb))
    full = lambda shape: pl.BlockSpec(shape, lambda b, i: (0,) * len(shape))
    in_specs = []
    args = []
    for name, w in (("r", 1024), ("k", 1024), ("v", 1024), ("wdad", 256)):
        cb = COL[name] // w
        in_specs += [main(w, cb), prev(w, cb), nxt(w, cb)]
        args += [proj, proj, proj]
    mu = p["shift_mu"]
    in_specs += [pl.BlockSpec((2, 1024), lambda b, i: (0, 0)), pl.BlockSpec((2, 1024), lambda b, i: (0, 1)),
                 pl.BlockSpec((2, 1024), lambda b, i: (0, 2)), pl.BlockSpec((2, 256), lambda b, i: (0, 12))]
    args += [mu, mu, mu, mu]
    in_specs += [full((2, RW_LORA, BR)), full((2, BR)), full((2, RW_LORA, BR)), full((2, BR)),
                 full((1, BR)), full((1, BR)), full((1, BR)), full((BR, 128)), full((128, BR)), full((2, tl, tl))]
    args += [p["rw_w_up"], p["rw_w0"], p["rw_a_up"], p["rw_a0"], p["rw_k_k"].reshape(1, BR),
             p["rw_k_a"].reshape(1, BR), p["rw_r_k"].reshape(1, BR), p["sum64"], p["bcast64"], p["tri"]]
    dir_spec = pl.BlockSpec((1, 2, tl, BR), lambda b, i: (b, 0, i, 0))
    tok_spec = pl.BlockSpec((1, tl, BR), lambda b, i: (b, i, 0))
    nchunk = tl // RW_T
    out_specs = [dir_spec, dir_spec, dir_spec, dir_spec, tok_spec, tok_spec,
                 pl.BlockSpec((1, 2, nchunk, 1, BR), lambda b, i: (b, 0, i, 0, 0))]
    dir_shape = jax.ShapeDtypeStruct((bsz, 2, L, BR), BF16)
    out_shape = [dir_shape, dir_shape, dir_shape, dir_shape,
                 jax.ShapeDtypeStruct((bsz, L, BR), BF16), jax.ShapeDtypeStruct((bsz, L, BR), F32),
                 jax.ShapeDtypeStruct((bsz, 2, L // RW_T, 1, BR), F32)]
    return pl.pallas_call(
        functools.partial(_rwprep_kernel, tl=tl, n_ctx=n_ctx, L=L),
        grid=(bsz, L // tl),
        in_specs=in_specs, out_specs=out_specs, out_shape=out_shape,
        compiler_params=_cparams(("parallel", "parallel")),
        name="rwkv_prepare",
    )(*args)


def _scan_chunk(d, j, nc_ctx, nc):
    bwd = jnp.where(j < nc_ctx, nc_ctx - 1 - j, nc - 1 - j + nc_ctx)
    return jnp.where(d == 0, j, bwd)


RW_PREC = {"scores": "bf16", "inv": "bf16", "xs": "bf16", "av": "bf16", "u": "bf16", "y": "bf16", "state": "bf16"}


def _mmx(a, b, mode, dims="nn"):
    f = {"nn": _dot, "nt": _dot_nt, "tn": _dot_tn}[dims]
    if mode == "hi":
        return f(a, b, precision=HI)
    ah, bh = a.astype(BF16), b.astype(BF16)
    if mode == "bf16":
        return f(ah, bh)
    al = (a - ah.astype(F32)).astype(BF16)
    bl = (b - bh.astype(F32)).astype(BF16)
    return f(ah, bh) + f(ah, bl) + f(al, bh)


def _rwscan_kernel(atf_ref, rtf_ref, btf_ref, ktf_ref, vf_ref, gf_ref,
                   atb_ref, rtb_ref, btb_ref, ktb_ref, vb_ref, gb_ref, yf_ref, yb_ref, s_ref):
    j = pl.program_id(1)
    T = RW_T

    @pl.when(j == 0)
    def _():
        s_ref[...] = jnp.zeros_like(s_ref)

    lane = lax.broadcasted_iota(jnp.int32, (2 * T, 128), 1)
    row = lax.broadcasted_iota(jnp.int32, (2 * T, 128), 0)
    own = (lane // RW_N) == (row // T)
    r2 = lax.broadcasted_iota(jnp.int32, (2 * T, 2 * T), 0)
    c2 = lax.broadcasted_iota(jnp.int32, (2 * T, 2 * T), 1)
    same = (r2 // T) == (c2 // T)
    strict = [jnp.logical_and(same, r2 > c2), jnp.logical_and(same, r2 < c2)]
    incl = [jnp.logical_and(same, r2 >= c2), jnp.logical_and(same, r2 <= c2)]
    eye = jnp.where(r2 == c2, 1.0, 0.0)
    base_blk = (r2 // RW_INV_BASE) == (c2 // RW_INV_BASE)
    merge_blks = []
    s = RW_INV_BASE
    while s < T:
        merge_blks.append(jnp.logical_and((r2 // (2 * s)) == (c2 // (2 * s)), (r2 // s) != (c2 // s)))
        s *= 2

    def stack(x):
        return jnp.where(own, jnp.concatenate([x, x], axis=0), 0.0)

    refs = ((atf_ref, rtf_ref, btf_ref, ktf_ref, vf_ref, gf_ref, yf_ref),
            (atb_ref, rtb_ref, btb_ref, ktb_ref, vb_ref, gb_ref, yb_ref))
    chains = [(d, p) for d in range(2) for p in range(BR // 128)]
    dirs = [d for d, _ in chains]
    sls = [slice(p * 128, (p + 1) * 128) for _, p in chains]
    each = lambda fn, *lists: [fn(*xs) for xs in zip(*lists)]
    prec = RW_PREC
    AR = [jnp.concatenate([stack(refs[d][0][0, 0, :, sl]), stack(refs[d][1][0, 0, :, sl])], axis=0)
          for d, sl in zip(dirs, sls)]
    BK = [jnp.concatenate([stack(refs[d][2][0, 0, :, sl]), stack(refs[d][3][0, 0, :, sl])], axis=0)
          for d, sl in zip(dirs, sls)]
    V = [stack(refs[d][4][0, :, sl]) for d, sl in zip(dirs, sls)]
    sc = each(lambda a, b: _mmx(a, b, prec["scores"], "nt"), AR, BK)
    Aab = [jnp.where(strict[d], x[:2 * T, :2 * T], 0.0) for d, x in zip(dirs, sc)]
    Aak = [jnp.where(strict[d], x[:2 * T, 2 * T:], 0.0) for d, x in zip(dirs, sc)]
    Mrbk = [jnp.concatenate([jnp.where(incl[d], x[2 * T:, :2 * T], 0.0),
                             jnp.where(incl[d], x[2 * T:, 2 * T:], 0.0)], axis=1) for d, x in zip(dirs, sc)]

    inv = lambda a, b: _mmx(a, b, prec["inv"])
    N = [jnp.where(base_blk, x, 0.0) for x in Aab]
    P = [eye + x for x in N]
    N = each(inv, N, N)
    NP = each(lambda n, q: inv(jnp.concatenate([n, q], axis=0), n), N, P)
    P = each(lambda q, x: q + x[2 * T:], P, NP)
    P = each(lambda q, x: q + inv(q, x[:2 * T]), P, NP)
    for off_blk in merge_blks:
        CP = each(lambda a, q: inv(jnp.where(off_blk, a, 0.0), q), Aab, P)
        P = each(lambda q, x: q + inv(q, x), P, CP)

    AV = each(lambda a, v: _mmx(a, v, prec["av"]), Aak, V)
    S = [s_ref[d, p] for d, p in chains]
    XS = each(lambda a, s_: _mmx(a, s_, prec["xs"], "nt"), AR, S)
    U = each(lambda q, x, w: _mmx(q, x[:2 * T] + w, prec["u"]), P, XS, AV)
    UV = each(lambda u, v: jnp.concatenate([u.astype(BF16), v], axis=0), U, V)
    Ys = each(lambda x, m, uv: x[2 * T:] + _mmx(m, uv, prec["y"]), XS, Mrbk, UV)
    dS = each(lambda uv, bk: _mmx(uv, bk, prec["state"], "tn"), UV, BK)
    for c, (d, p) in enumerate(chains):
        refs[d][6][0, :, sls[c]] = Ys[c][:T] + Ys[c][T:]
        s_ref[d, p] = (S[c] + dS[c]) * refs[d][5][0, 0, 0, :, sls[c]]


def _rw_scan(at, rt, bt, kt, v, g, n_ctx):
    bsz, _, L, _ = at.shape
    nc, nc_ctx = L // RW_T, n_ctx // RW_T

    def specs(d):
        ch = lambda j: _scan_chunk(d, j, nc_ctx, nc)
        dspec = pl.BlockSpec((1, 1, RW_T, BR), lambda b, j: (b, d, ch(j), 0))
        tspec = pl.BlockSpec((1, RW_T, BR), lambda b, j: (b, ch(j), 0))
        return tspec, [dspec, dspec, dspec, dspec, tspec,
                       pl.BlockSpec((1, 1, 1, 1, BR), lambda b, j: (b, d, ch(j), 0, 0))]

    (out_f, in_f), (out_b, in_b) = specs(0), specs(1)
    shape = jax.ShapeDtypeStruct((bsz, L, BR), F32)
    return pl.pallas_call(
        _rwscan_kernel,
        grid=(bsz, nc),
        in_specs=in_f + in_b,
        out_specs=[out_f, out_b],
        out_shape=[shape, shape],
        scratch_shapes=[pltpu.VMEM((2, BR // 128, 128, 128), F32)],
        compiler_params=_cparams(("parallel", "arbitrary")),
        name="rwkv_scan",
    )(at, rt, bt, kt, v, g, at, rt, bt, kt, v, g)


def _atprep_kernel(q_ref, k_ref, v_ref, cos_ref, sin_ref, qg_ref, kg_ref, qo_ref, ko_ref, vo_ref):
    cos = cos_ref[...]
    sin = sin_ref[...]
    lane = lax.broadcasted_iota(jnp.int32, cos.shape, 1)
    first_half = (lane % 64) < 32

    def norm_rope(x, g):
        y = x * lax.rsqrt(jnp.mean(x * x, axis=-1, keepdims=True) + EPS) * g
        partner = jnp.where(first_half, pltpu.roll(y, 96, axis=1), pltpu.roll(y, 32, axis=1))
        return y * cos + partner * sin

    for h in range(AT_H):
        sl = slice(h * AT_HD, (h + 1) * AT_HD)
        q_scale = AT_HD ** -0.5 * math.log2(math.e)
        qo_ref[0, :, sl] = (norm_rope(q_ref[0, :, sl], qg_ref[...]) * q_scale).astype(qo_ref.dtype)
    for h in range(AT_KV):
        sl = slice(h * AT_HD, (h + 1) * AT_HD)
        ko_ref[0, :, sl] = norm_rope(k_ref[0, :, sl], kg_ref[...]).astype(ko_ref.dtype)
    vo_ref[0] = v_ref[0].astype(vo_ref.dtype)


def _at_prepare(proj, cos, sin, q_g, k_g, tl=256):
    bsz, L, _ = proj.shape
    kvw = AT_KV * AT_HD
    tok = lambda w: pl.BlockSpec((1, tl, w), lambda b, i: (b, i, 0))
    return pl.pallas_call(
        _atprep_kernel,
        grid=(bsz, L // tl),
        in_specs=[pl.BlockSpec((1, tl, BR), lambda b, i: (b, i, COL["at_q"] // BR)),
                  pl.BlockSpec((1, tl, kvw), lambda b, i: (b, i, COL["at_k"] // kvw)),
                  pl.BlockSpec((1, tl, kvw), lambda b, i: (b, i, COL["at_v"] // kvw)),
                  pl.BlockSpec((tl, AT_HD), lambda b, i: (i, 0)),
                  pl.BlockSpec((tl, AT_HD), lambda b, i: (i, 0)),
                  pl.BlockSpec((1, AT_HD), lambda b, i: (0, 0)),
                  pl.BlockSpec((1, AT_HD), lambda b, i: (0, 0))],
        out_specs=[tok(BR), tok(kvw), tok(kvw)],
        out_shape=[jax.ShapeDtypeStruct((bsz, L, BR), BF16),
                   jax.ShapeDtypeStruct((bsz, L, kvw), BF16),
                   jax.ShapeDtypeStruct((bsz, L, kvw), BF16)],
        compiler_params=_cparams(("parallel", "parallel")),
        name="gqa_prepare",
    )(proj, proj, proj, cos, sin, q_g.reshape(1, AT_HD), k_g.reshape(1, AT_HD))


def _attn_kernel(q_ref, k_ref, v_ref, o_ref, s_ref, p_ref, l_ref, *, tq, n_ctx, kb, rows):
    i = pl.program_id(2)
    grp = AT_H // AT_KV
    def attend(n_keys):
        chunks = [slice(c * kb, (c + 1) * kb) for c in range(n_keys // kb)]

        def scores(r):
            s_ref[r, :, :n_keys] = _dot_nt(q_ref[0, :, r * AT_HD:(r + 1) * AT_HD], k_ref[0, :n_keys, :])

        def softmax(r):
            for rb in range(tq // rows):
                rs = slice(rb * rows, (rb + 1) * rows)
                mx = s_ref[r, rs, chunks[0]]
                for ch in chunks[1:]:
                    mx = jnp.maximum(mx, s_ref[r, rs, ch])
                m = jnp.broadcast_to(jnp.max(mx, axis=-1, keepdims=True), mx.shape)
                tot = jnp.zeros_like(mx)
                for ch in chunks:
                    p = jnp.exp2(s_ref[r, rs, ch] - m)
                    tot = tot + p
                    p_ref[r, rs, ch] = p.astype(BF16)
                l_ref[r, rs, :] = jnp.broadcast_to(jnp.sum(tot, axis=-1, keepdims=True), (rows, AT_HD))

        def values(r):
            o_ref[0, :, r * AT_HD:(r + 1) * AT_HD] = _dot(p_ref[r, :, :n_keys], v_ref[0, :n_keys, :]) / l_ref[r]

        stages = (scores, softmax, values)
        for t in range(grp + len(stages) - 1):
            for st, fn in enumerate(stages):
                if 0 <= t - st < grp:
                    fn(t - st)

    @pl.when(i * tq < n_ctx)
    def _():
        attend(n_ctx)

    @pl.when(i * tq >= n_ctx)
    def _():
        attend(k_ref.shape[1])


def _attention(q, k, v, n_ctx, tq=256, kb=256, rows=32):
    bsz, L, _ = q.shape
    gw = (AT_H // AT_KV) * AT_HD
    return pl.pallas_call(
        functools.partial(_attn_kernel, tq=tq, n_ctx=n_ctx, kb=kb, rows=rows),
        scratch_shapes=[pltpu.VMEM((gw // AT_HD, tq, L), F32), pltpu.VMEM((gw // AT_HD, tq, L), BF16),
                        pltpu.VMEM((gw // AT_HD, tq, AT_HD), F32)],
        grid=(bsz, AT_KV, L // tq),
        in_specs=[pl.BlockSpec((1, tq, gw), lambda b, g, i: (b, i, g)),
                  pl.BlockSpec((1, L, AT_HD), lambda b, g, i: (b, 0, g)),
                  pl.BlockSpec((1, L, AT_HD), lambda b, g, i: (b, 0, g))],
        out_specs=pl.BlockSpec((1, tq, gw), lambda b, g, i: (b, i, g)),
        out_shape=jax.ShapeDtypeStruct((bsz, L, BR), F32),
        compiler_params=_cparams(("parallel", "parallel", "parallel")),
        name="gqa_attention",
    )(q, k, v)


def _cap_gates(pre):
    return GATE_CAP * jnp.tanh(pre / GATE_CAP)


def _log_sigmoid(x):
    return jnp.minimum(x, 0.0) - jnp.log1p(jnp.exp(-jnp.abs(x)))


def _mlstm_kernel(qf_ref, kf_ref, vf_ref, gcf_ref, qb_ref, kb_ref, vb_ref, gcb_ref,
                  bc_ref, br_ref, hf_ref, hb_ref, c_ref, n_ref, m_ref):
    j = pl.program_id(1)
    T = ML_T

    @pl.when(j == 0)
    def _():
        c_ref[...] = jnp.zeros_like(c_ref)
        n_ref[...] = jnp.zeros_like(n_ref)
        m_ref[...] = jnp.zeros_like(m_ref)

    r2 = lax.broadcasted_iota(jnp.int32, (T, T), 0)
    c2 = lax.broadcasted_iota(jnp.int32, (T, T), 1)
    lane16 = lax.broadcasted_iota(jnp.int32, (T, 16), 1)
    sub16 = lax.broadcasted_iota(jnp.int32, (16, T), 0)

    refs = ((qf_ref, kf_ref, vf_ref, gcf_ref, None, hf_ref), (qb_ref, kb_ref, vb_ref, gcb_ref, None, hb_ref))
    seen = [c2 <= r2, c2 >= r2]
    seen_t = [r2 <= c2, r2 >= c2]
    gc = [_cap_gates(refs[d][3][0][:, :16] + bc_ref[...]) for d in range(2)]
    gr = [_cap_gates(refs[d][3][0].T[:16, :] + br_ref[...]) for d in range(2)]
    lsc = [_log_sigmoid(x) for x in gc]
    lsr = [_log_sigmoid(x) for x in gr]

    chains = [(d, h) for d in range(2) for h in range(ML_H)]
    each = lambda fn, *lists: [fn(*xs) for xs in zip(*lists)]
    pick_row = lambda x, idx: jnp.sum(jnp.where(sub16 == idx, x, 0.0), axis=0, keepdims=True)
    pick_col = lambda x, idx: jnp.sum(jnp.where(lane16 == idx, x, 0.0), axis=1, keepdims=True)
    li_row = [pick_row(gr[d], d * ML_H + h) for d, h in chains]
    lf_row = [pick_row(lsr[d], (2 + d) * ML_H + h) for d, h in chains]
    li_col = [pick_col(gc[d], d * ML_H + h) for d, h in chains]
    lf_col = [pick_col(lsc[d], (2 + d) * ML_H + h) for d, h in chains]
    b_col = [jnp.sum(jnp.where(seen[d], x, 0.0), axis=1, keepdims=True) for (d, _), x in zip(chains, lf_row)]
    b_row = [jnp.sum(jnp.where(seen_t[d], x, 0.0), axis=0, keepdims=True) for (d, _), x in zip(chains, lf_col)]
    g = [jnp.sum(x, axis=0, keepdims=True) for x in lf_col]
    m_prev = [m_ref[d, h] for d, h in chains]
    q = [refs[d][0][0, :, h * ML_DK:(h + 1) * ML_DK] * (ML_DK ** -0.5) for d, h in chains]
    k = [refs[d][1][0, :, h * ML_DK:(h + 1) * ML_DK] for d, h in chains]
    vb = [refs[d][2][0, :, h * ML_DV:(h + 1) * ML_DV].astype(BF16) for d, h in chains]
    C = [c_ref[d, h] for d, h in chains]
    n = [n_ref[d, h] for d, h in chains]
    qb = [x.astype(BF16) for x in q]

    dmat = [jnp.where(seen[d], bc - br + li, -jnp.inf) for (d, _), bc, br, li in zip(chains, b_col, b_row, li_row)]
    m_inter = each(lambda bc, m: bc + m, b_col, m_prev)
    m_t = each(lambda mi, dm: jnp.maximum(mi, jnp.max(dm, axis=-1, keepdims=True)), m_inter, dmat)
    w_inter = each(lambda mi, mt: jnp.exp(mi - mt), m_inter, m_t)
    qk = each(lambda a, b: _dot_nt(a, b.astype(BF16)), qb, k)
    qc = each(lambda a, b: _dot(a, b.astype(BF16)), qb, C)
    s = each(lambda x, dm, mt: x * jnp.exp(dm - mt), qk, dmat, m_t)
    sv = each(lambda a, b: _dot(a.astype(BF16), b), s, vb)
    qn = each(lambda a, b: jnp.sum(a * b, axis=-1, keepdims=True), q, n)
    den = each(lambda w, a, x: w * a + jnp.sum(x, axis=-1, keepdims=True), w_inter, qn, s)
    for (d, h), w, a, b, dn, mt in zip(chains, w_inter, qc, sv, den, m_t):
        refs[d][5][0, :, h * ML_DV:(h + 1) * ML_DV] = (w * a + b) / jnp.maximum(jnp.abs(dn), jnp.exp(-mt))

    loga = each(lambda g_, bc, li: g_ - bc + li, g, b_col, li_col)
    m_new = each(lambda g_, m, la: jnp.maximum(g_ + m, jnp.max(la, axis=0, keepdims=True)), g, m_prev, loga)
    carry = each(lambda g_, m, mn: jnp.exp(g_ + m - mn), g, m_prev, m_new)
    wk = each(lambda la, mn, k_: jnp.exp(la - mn) * k_, loga, m_new, k)
    kv = each(lambda a, b: _dot_tn(a.astype(BF16), b), wk, vb)
    for (d, h), cr, c_, kv_, n_, wk_, mn in zip(chains, carry, C, kv, n, wk, m_new):
        c_ref[d, h] = cr * c_ + kv_
        n_ref[d, h] = cr * n_ + jnp.sum(wk_, axis=0, keepdims=True)
        m_ref[d, h] = mn


def _mlstm(proj, bias_col, bias_row, n_ctx):
    bsz, L, _ = proj.shape
    nc, nc_ctx = L // ML_T, n_ctx // ML_T
    qw, vw = ML_H * ML_DK, ML_H * ML_DV

    def dir_specs(d):
        ch = lambda j: _scan_chunk(d, j, nc_ctx, nc)
        return [pl.BlockSpec((1, ML_T, qw), lambda b, j: (b, ch(j), COL["ml_q"] // qw)),
                pl.BlockSpec((1, ML_T, qw), lambda b, j: (b, ch(j), COL["ml_k"] // qw)),
                pl.BlockSpec((1, ML_T, vw), lambda b, j: (b, ch(j), COL["ml_v"] // vw)),
                pl.BlockSpec((1, ML_T, 128), lambda b, j: (b, ch(j), COL["ml_if"] // 128))]

    def out_spec(d):
        ch = lambda j: _scan_chunk(d, j, nc_ctx, nc)
        return pl.BlockSpec((1, ML_T, vw), lambda b, j: (b, ch(j), 0))

    shape = jax.ShapeDtypeStruct((bsz, L, vw), F32)
    return pl.pallas_call(
        _mlstm_kernel,
        grid=(bsz, nc),
        in_specs=dir_specs(0) + dir_specs(1) + [pl.BlockSpec((1, 16), lambda b, j: (0, 0)),
                                                pl.BlockSpec((16, 1), lambda b, j: (0, 0))],
        out_specs=[out_spec(0), out_spec(1)],
        out_shape=[shape, shape],
        scratch_shapes=[pltpu.VMEM((2, ML_H, ML_DK, ML_DV), F32),
                        pltpu.VMEM((2, ML_H, 1, ML_DK), F32),
                        pltpu.VMEM((2, ML_H, 1, 1), F32)],
        compiler_params=_cparams(("parallel", "arbitrary")),
        name="mlstm_scan",
    )(proj, proj, proj, proj, proj, proj, proj, proj, bias_col, bias_row)


def _epilogue_kernel(yf_ref, yb_ref, bonus_ref, rwg_ref, att_ref, atg_ref, hf_ref, hb_ref, mlo_ref, mlg_ref,
                     lnw_ref, lnb_ref, mng_ref, m64_ref, b64_ref, m256_ref, b256_ref, o_ref):
    y = yf_ref[0] + yb_ref[0]
    mu = _seg_reduce(y, m64_ref[...], b64_ref[...])
    yc = y - mu
    var = _seg_reduce(yc * yc, m64_ref[...], b64_ref[...])
    ya = yc * lax.rsqrt(var + RW_GN_EPS) * lnw_ref[...] + lnb_ref[...] + bonus_ref[0]
    o_ref[0, 0] = (ya * _silu(rwg_ref[0])).astype(o_ref.dtype)

    o_ref[1, 0] = (att_ref[0] * _silu(atg_ref[0])).astype(o_ref.dtype)

    hh = hf_ref[0] + hb_ref[0]
    ms = _seg_reduce(hh * hh, m256_ref[...], b256_ref[...])
    hn = hh * lax.rsqrt(ms + EPS) * mng_ref[...]
    o_ref[2, 0] = (jax.nn.sigmoid(mlo_ref[0]) * hn * _silu(mlg_ref[0])).astype(o_ref.dtype)


def _epilogue(proj, y_f, y_b, bonus, att, h_f, h_b, p, tl=256):
    bsz, L, _ = proj.shape
    tok = pl.BlockSpec((1, tl, BR), lambda b, i: (b, i, 0))
    pc = lambda name: pl.BlockSpec((1, tl, BR), lambda b, i: (b, i, COL[name] // BR))
    vec = pl.BlockSpec((1, BR), lambda b, i: (0, 0))
    gat = pl.BlockSpec((BR, 128), lambda b, i: (0, 0))
    sct = pl.BlockSpec((128, BR), lambda b, i: (0, 0))
    return pl.pallas_call(
        _epilogue_kernel,
        grid=(bsz, L // tl),
        in_specs=[tok, tok, tok, pc("rw_g"), tok, pc("at_g"), tok, tok, pc("ml_o"), pc("ml_g"),
                  vec, vec, vec, gat, sct, gat, sct],
        out_specs=pl.BlockSpec((3, 1, tl, BR), lambda b, i: (0, b, i, 0)),
        out_shape=jax.ShapeDtypeStruct((3, bsz, L, BR), BF16),
        compiler_params=_cparams(("parallel", "parallel")),
        name="branch_epilogue",
    )(y_f, y_b, bonus, proj, att, proj, h_f, h_b, proj, proj,
      p["rw_ln_w"].reshape(1, BR), p["rw_ln_b"].reshape(1, BR), p["ml_norm_g"].reshape(1, BR),
      p["mean64"], p["bcast64"], p["mean256"], p["bcast256"])


def _merge_kernel(y_ref, w_ref, g0_ref, g1_ref, g2_ref, o_ref):
    acc = jax.nn.sigmoid(g0_ref[...]) * _dot(y_ref[0], w_ref[0])
    acc += jax.nn.sigmoid(g1_ref[...]) * _dot(y_ref[1], w_ref[1])
    acc += jax.nn.sigmoid(g2_ref[...]) * _dot(y_ref[2], w_ref[2])
    o_ref[...] = acc.astype(o_ref.dtype)


def _merge(ys, w_branch, proj2d, tm=512, tn=1024):
    _, m, _ = ys.shape
    nb = D // tn
    gate = lambda n: pl.BlockSpec((tm, tn), lambda j, i: (i, n * nb + j))
    return pl.pallas_call(
        _merge_kernel,
        grid=(D // tn, m // tm),
        in_specs=[pl.BlockSpec((3, tm, BR), lambda j, i: (0, i, 0)),
                  pl.BlockSpec((3, BR, tn), lambda j, i: (0, 0, j)),
                  gate(0), gate(1), gate(2)],
        out_specs=pl.BlockSpec((tm, tn), lambda j, i: (i, j)),
        out_shape=jax.ShapeDtypeStruct((m, D), BF16),
        compiler_params=_cparams(("parallel", "parallel")),
        name="branch_merge",
    )(ys, w_branch, proj2d, proj2d, proj2d)


def _outproj_kernel(*refs, nct):
    a_ref, w_ref, gt_ref, *z_refs, o_ref = refs
    z = _stream_tile(z_refs, pl.program_id(2) < nct)
    o_ref[0] = z + gt_ref[0, 0] * _dot(a_ref[0], w_ref[...])


def _out_projection(mixed, w_out, z, gate, n_ctx, tl=256, tn=1024):
    bsz, L, _ = mixed.shape
    nct = n_ctx // tl
    z_specs, z_args = _stream_specs(z, tl, nct, (1, tl, tn), lambda j, b, i: (b, i, j))
    return pl.pallas_call(
        functools.partial(_outproj_kernel, nct=nct),
        grid=(D // tn, bsz, L // tl),
        in_specs=[pl.BlockSpec((1, tl, D), lambda j, b, i: (b, i, 0)),
                  pl.BlockSpec((D, tn), lambda j, b, i: (0, j)),
                  pl.BlockSpec((1, 1, 1, tn), lambda j, b, i: (b, (i >= nct).astype(jnp.int32), 0, j))] + z_specs,
        out_specs=pl.BlockSpec((1, tl, tn), lambda j, b, i: (b, i, j)),
        out_shape=jax.ShapeDtypeStruct((bsz, L, D), F32),
        compiler_params=_cparams(("parallel", "parallel", "parallel")),
        name="out_projection",
    )(mixed, w_out, gate, *z_args)


def _rope_tables(n_ctx, n_lat):
    rows = n_lat // GRID_W
    row = jnp.repeat(jnp.arange(rows), GRID_W).astype(F32)
    col = jnp.tile(jnp.arange(GRID_W), rows).astype(F32)
    inv_freq = ROPE_THETA ** (-jnp.arange(0, AT_HD // 2, 2, dtype=F32) / (AT_HD // 2))
    ang_lat = jnp.stack([row[:, None] * inv_freq, col[:, None] * inv_freq], axis=1)
    ang = jnp.concatenate([jnp.zeros((n_ctx, 2, AT_HD // 4), F32), ang_lat], axis=0)
    cos, sin = jnp.cos(ang), jnp.sin(ang)
    cos_t = jnp.concatenate([cos[:, 0], cos[:, 0], cos[:, 1], cos[:, 1]], axis=-1)
    sin_t = jnp.concatenate([-sin[:, 0], sin[:, 0], -sin[:, 1], sin[:, 1]], axis=-1)
    return cos_t, sin_t


def _group_consts(width, value):
    member = (np.arange(BR)[:, None] // width) == np.arange(128)[None, :]
    return (jnp.asarray(np.where(member, value, 0.0), dtype=BF16),
            jnp.asarray(np.where(member.T, 1.0, 0.0), dtype=BF16))


def _chunk_tri(tl):
    t = np.arange(tl)
    same = (t[:, None] // RW_T) == (t[None, :] // RW_T)
    fwd = same & (t[None, :] <= t[:, None])
    bwd = same & (t[None, :] >= t[:, None])
    return jnp.asarray(np.stack([fwd, bwd]).astype(np.float32), dtype=BF16)


def kernel(x, c, ctx, c_ctx, norm_g, w_ada, b_ada, w_in, shift_mu, rw_w_up, rw_w0, rw_a_up, rw_a0, rw_k_k, rw_k_a, rw_r_k, rw_ln_w, rw_ln_b, at_q_g, at_k_g, ml_gate_b, ml_norm_g, w_branch, w_out, final_g):
    bsz, n_lat, _ = x.shape
    n_ctx = ctx.shape[1]
    L = n_ctx + n_lat
    depth = w_in.shape[0]

    cos_t, sin_t = _rope_tables(n_ctx, n_lat)
    sum64, bcast64 = _group_consts(RW_N, 1.0)
    mean64, _ = _group_consts(RW_N, 1.0 / RW_N)
    mean256, bcast256 = _group_consts(ML_DV, 1.0 / ML_DV)
    consts = {"sum64": sum64, "bcast64": bcast64, "mean64": mean64, "mean256": mean256, "bcast256": bcast256,
              "tri": _chunk_tri(256)}

    cc = jnp.concatenate([c, c_ctx[None], jnp.zeros((8 - bsz - 1, D), F32)], axis=0)
    mod = _modulation(cc, w_ada, b_ada)

    z = (ctx, x)
    for l in range(depth):
        sh, sc, gt = mod[l, :, :D], mod[l, :, D:2 * D], mod[l, :, 2 * D:]
        pick = lambda t: jnp.stack([jnp.broadcast_to(t[bsz], (bsz, D)), t[:bsz]], axis=1)[:, :, None, :]
        scale = pick((1.0 + sc) * norm_g[l])
        shift = pick(sh)
        gate = pick(gt)

        h = _norm_mod(z, scale, shift, n_ctx, L, BF16)
        proj2d = _in_projection(h.reshape(bsz * L, D), w_in[l])
        proj = proj2d.reshape(bsz, L, N_PROJ)

        p = dict(consts, shift_mu=shift_mu[l], rw_w_up=rw_w_up[l], rw_w0=rw_w0[l], rw_a_up=rw_a_up[l],
                 rw_a0=rw_a0[l], rw_k_k=rw_k_k[l], rw_k_a=rw_k_a[l], rw_r_k=rw_r_k[l], rw_ln_w=rw_ln_w[l],
                 rw_ln_b=rw_ln_b[l], ml_norm_g=ml_norm_g[l])
        at, rt, bt, kt, v_rw, bonus, g_rw = _rw_prepare(proj, p, n_ctx)
        y_f, y_b = _rw_scan(at, rt, bt, kt, v_rw, g_rw, n_ctx)

        qn, kn, vn = _at_prepare(proj, cos_t, sin_t, at_q_g[l], at_k_g[l])
        att = _attention(qn, kn, vn, n_ctx)

        bias = ml_gate_b[l].reshape(16)
        h_f, h_b = _mlstm(proj, bias.reshape(1, 16), bias.reshape(16, 1), n_ctx)

        ys = _epilogue(proj, y_f, y_b, bonus, att, h_f, h_b, p)
        mixed = _merge(ys.reshape(3, bsz * L, BR), w_branch[l].astype(BF16), proj2d)
        z = _out_projection(mixed.reshape(bsz, L, D), w_out[l].astype(BF16), z, gate, n_ctx)

    ones = jnp.ones((bsz, 2, 1, D), F32) * final_g
    zeros = jnp.zeros((bsz, 2, 1, D), F32)
    return _norm_mod(z, ones, zeros, n_ctx, L, F32, latent_only=True)
```

```python
import functools
import math

import numpy as np
import jax
import jax.numpy as jnp
from jax import lax
from jax.experimental import pallas as pl
from jax.experimental.pallas import tpu as pltpu

F32 = jnp.float32
BF16 = jnp.bfloat16
HI = lax.Precision.HIGHEST

D = 2048
BR = 1024
EPS = 1e-6
GRID_W = 64

RW_H, RW_N, RW_LORA = 16, 64, 64
RW_GN_EPS = 64e-5
RW_T = 64
RW_INV_BASE = 8

AT_H, AT_KV, AT_HD = 8, 2, 128
ROPE_THETA = 10000.0

ML_H, ML_DK, ML_DV, ML_T = 4, 128, 256, 128
GATE_CAP = 15.0

D_IN = 17168
LANES = 128
PROJ_TN = 512
_PROJ_GROUPS = (
    ("merge", 11024, 6144), ("r", 0, 1024), ("k", 1024, 1024), ("v", 2048, 1024),
    ("rw_g", 3328, 1024), ("at_q", 4352, 1024), ("at_g", 5888, 1024), ("ml_v", 7936, 1024),
    ("ml_o", 8960, 1024), ("ml_g", 10000, 1024), ("ml_q", 6912, 512), ("ml_k", 7424, 512),
    ("at_k", 5376, 512), ("wdad", 3072, 512), ("ml_if", 9984, 512),
)
COL = {}
PROJ_SRC = []
for _name, _start, _width in _PROJ_GROUPS:
    COL[_name] = len(PROJ_SRC) * PROJ_TN
    PROJ_SRC += [_start + t * PROJ_TN for t in range(_width // PROJ_TN)]
COL["at_v"] = COL["at_k"] + 256
N_PROJ = len(PROJ_SRC) * PROJ_TN
PROJ_WIN = PROJ_TN + LANES
PROJ_SHIFT = 16
assert all(s % LANES in (0, PROJ_SHIFT) for s in PROJ_SRC)
for _name, _blk in (("merge", 2048), ("r", 1024), ("k", 1024), ("v", 1024), ("rw_g", 1024), ("at_q", 1024),
                    ("at_g", 1024), ("ml_v", 1024), ("ml_o", 1024), ("ml_g", 1024), ("ml_q", 512), ("ml_k", 512),
                    ("at_k", 256), ("at_v", 256), ("wdad", 256), ("ml_if", 128)):
    assert COL[_name] % _blk == 0, _name

VMEM_LIMIT = 48 * 1024 * 1024


def _cparams(sem):
    return pltpu.CompilerParams(dimension_semantics=sem, vmem_limit_bytes=VMEM_LIMIT)


def _dot(a, b, **kw):
    return jnp.dot(a, b, preferred_element_type=F32, **kw)


def _dot_nt(a, b, **kw):
    return lax.dot_general(a, b, (((1,), (1,)), ((), ())), preferred_element_type=F32, **kw)


def _dot_tn(a, b, **kw):
    return lax.dot_general(a, b, (((0,), (0,)), ((), ())), preferred_element_type=F32, **kw)


def _split3(x):
    h = x.astype(BF16)
    r = x - h.astype(F32)
    m = r.astype(BF16)
    l = (r - m.astype(F32)).astype(BF16)
    return h, m, l


def _split2(x):
    h = x.astype(BF16)
    return h, (x - h.astype(F32)).astype(BF16)


def _seg_reduce(x, gather, scatter):
    h, l = _split2(x)
    sh, sl = _split2(_dot(h, gather) + _dot(l, gather))
    return _dot(sh, scatter) + _dot(sl, scatter)


def _dot_const_lhs(c, x):
    h, m, l = _split3(x)
    return _dot(c, h) + _dot(c, m) + _dot(c, l)


def _silu(x):
    return x * jax.nn.sigmoid(x)


def _mod_kernel(c_ref, w_ref, b_ref, o_ref):
    @pl.when(pl.program_id(1) == 0)
    def _():
        o_ref[0] = jnp.broadcast_to(b_ref[0], o_ref.shape[1:])

    xh, xl = _split2(_silu(c_ref[...]))
    wh, wl = _split2(w_ref[0])
    o_ref[0] += _dot(xh, wh) + _dot(xh, wl) + _dot(xl, wh)


def _modulation(cc, w_ada, b_ada, tk=256):
    depth = w_ada.shape[0]
    return pl.pallas_call(
        _mod_kernel,
        grid=(depth, D // tk),
        in_specs=[pl.BlockSpec((8, tk), lambda l, k: (0, k)),
                  pl.BlockSpec((1, tk, 3 * D), lambda l, k: (l, k, 0)),
                  pl.BlockSpec((1, 1, 3 * D), lambda l, k: (l, 0, 0))],
        out_specs=pl.BlockSpec((1, 8, 3 * D), lambda l, k: (l, 0, 0)),
        out_shape=jax.ShapeDtypeStruct((depth, 8, 3 * D), F32),
        compiler_params=_cparams(("parallel", "arbitrary")),
        name="adaln_modulation",
    )(cc, w_ada, b_ada.reshape(depth, 1, 3 * D))


def _stream_specs(z, tl, nct, block, index):
    if not isinstance(z, tuple):
        return [pl.BlockSpec(block, lambda *g: index(*g))], [z]

    def ctx_index(*g):
        b, i, j = index(*g)
        return b, jnp.minimum(i, nct - 1), j

    def lat_index(*g):
        b, i, j = index(*g)
        return b, jnp.maximum(i - nct, 0), j

    return [pl.BlockSpec(block, ctx_index), pl.BlockSpec(block, lat_index)], list(z)


def _for_stream_tile(z_refs, is_ctx, body):
    if len(z_refs) == 1:
        body(z_refs[0][0])
        return
    pl.when(is_ctx)(lambda: body(z_refs[0][0]))
    pl.when(jnp.logical_not(is_ctx))(lambda: body(z_refs[1][0]))


def _norm_kernel(*refs, nct, first):
    *z_refs, sc_ref, sh_ref, o_ref = refs

    def body(x):
        y = x * lax.rsqrt(jnp.mean(x * x, axis=-1, keepdims=True) + EPS)
        o_ref[0] = (y * sc_ref[0, 0] + sh_ref[0, 0]).astype(o_ref.dtype)

    _for_stream_tile(z_refs, pl.program_id(1) + first < nct, body)


def _norm_mod(z, scale, shift, n_ctx, L, out_dtype, latent_only=False, tl=256):
    bsz = scale.shape[0]
    nct = n_ctx // tl
    first = nct if latent_only else 0
    seg = lambda b, i: (b, (i + first >= nct).astype(jnp.int32), 0, 0)
    z_specs, z_args = _stream_specs(z, tl, nct, (1, tl, D), lambda b, i: (b, i + first, 0))
    return pl.pallas_call(
        functools.partial(_norm_kernel, nct=nct, first=first),
        grid=(bsz, L // tl - first),
        in_specs=z_specs + [pl.BlockSpec((1, 1, 1, D), seg), pl.BlockSpec((1, 1, 1, D), seg)],
        out_specs=pl.BlockSpec((1, tl, D), lambda b, i: (b, i, 0)),
        out_shape=jax.ShapeDtypeStruct((bsz, L - first * tl, D), out_dtype),
        compiler_params=_cparams(("parallel", "parallel")),
        name="rmsnorm_modulate",
    )(*z_args, scale, shift)


_TILE_ALIGNED, _TILE_SHIFTED, _TILE_TAIL = 0, 1, 2


def _proj_tables():
    start, kind = [], []
    for s in PROJ_SRC:
        if s % LANES == 0:
            start.append(s // LANES), kind.append(_TILE_ALIGNED)
        elif s - PROJ_SHIFT + PROJ_WIN <= D_IN:
            start.append((s - PROJ_SHIFT) // LANES), kind.append(_TILE_SHIFTED)
        else:
            assert s - (D_IN - PROJ_WIN) == LANES
            start.append(0), kind.append(_TILE_TAIL)
    return np.array(start, np.int32), np.array(kind, np.int32)


def _inproj_kernel(start_ref, kind_ref, a_ref, w_ref, tail_ref, o_ref):
    kind = kind_ref[pl.program_id(1)]

    @pl.when(kind == _TILE_ALIGNED)
    def _():
        o_ref[...] = _dot(a_ref[...], w_ref[0][:, :PROJ_TN].astype(BF16))

    @pl.when(kind == _TILE_SHIFTED)
    def _():
        o_ref[...] = _dot(a_ref[...], w_ref[0][:, PROJ_SHIFT:PROJ_SHIFT + PROJ_TN].astype(BF16))

    @pl.when(kind == _TILE_TAIL)
    def _():
        o_ref[...] = _dot(a_ref[...], tail_ref[...][:, LANES:LANES + PROJ_TN].astype(BF16))


def _in_projection(a, w, layer, tm=2176):
    m, k = a.shape
    tm = math.gcd(m, tm)
    start, kind = _proj_tables()
    grid_spec = pltpu.PrefetchScalarGridSpec(
        num_scalar_prefetch=2,
        grid=(m // tm, len(PROJ_SRC)),
        in_specs=[pl.BlockSpec((tm, k), lambda i, j, st, kd: (i, 0)),
                  pl.BlockSpec((pl.Element(1), pl.Element(k), pl.Element(PROJ_WIN)),
                               lambda i, j, st, kd: (layer, 0, st[j] * LANES)),
                  pl.BlockSpec((k, PROJ_WIN), lambda i, j, st, kd: (0, 0))],
        out_specs=pl.BlockSpec((tm, PROJ_TN), lambda i, j, st, kd: (i, j)),
    )
    return pl.pallas_call(
        _inproj_kernel,
        grid_spec=grid_spec,
        out_shape=jax.ShapeDtypeStruct((m, N_PROJ), F32),
        compiler_params=_cparams(("parallel", "arbitrary")),
        name="in_projection",
    )(jnp.asarray(start), jnp.asarray(kind), a, w, w[layer, :, D_IN - PROJ_WIN:])


def _rwprep_kernel(r_ref, rp_ref, rn_ref, k_ref, kp_ref, kn_ref, v_ref, vp_ref, vn_ref,
                   w_ref, wp_ref, wn_ref, mur_ref, muk_ref, muv_ref, muw_ref,
                   wup_ref, w0_ref, aup_ref, a0_ref, kk_ref, ka_ref, rk_ref, gat_ref, sct_ref, tri_ref,
                   at_ref, rt_ref, bt_ref, kt_ref, vo_ref, bonus_ref, g_ref, *, tl, n_ctx, L):
    i = pl.program_id(1)
    start = i * tl
    has_prev = jnp.logical_and(start != 0, start != n_ctx).astype(F32)
    has_next = jnp.logical_and(start + tl != n_ctx, start + tl != L).astype(F32)

    def shift(x_ref, p_ref, n_ref, mu_ref):
        x = x_ref[0]
        row = lax.broadcasted_iota(jnp.int32, x.shape, 0)
        prev = jnp.where(row == 0, p_ref[0, 7:8, :] * has_prev, pltpu.roll(x, 1, axis=0))
        nxt = jnp.where(row == tl - 1, n_ref[0, 0:1, :] * has_next, pltpu.roll(x, tl - 1, axis=0))
        return x + mu_ref[0:1, :] * (prev - x) + mu_ref[1:2, :] * (nxt - x)

    r = shift(r_ref, rp_ref, rn_ref, mur_ref)
    k = shift(k_ref, kp_ref, kn_ref, muk_ref)
    v = shift(v_ref, vp_ref, vn_ref, muv_ref)
    wdad = shift(w_ref, wp_ref, wn_ref, muw_ref)
    gat, sct = gat_ref[...], sct_ref[...]

    kk = k * kk_ref[...]
    kk = kk / jnp.maximum(jnp.sqrt(_seg_reduce(kk * kk, gat, sct)), 1e-12)
    bonus_ref[0] = _seg_reduce(r * k * rk_ref[...], gat, sct) * v
    vo_ref[0] = v.astype(vo_ref.dtype)

    for d in range(2):
        wd = wdad[:, d * RW_LORA:(d + 1) * RW_LORA]
        ad = wdad[:, 2 * RW_LORA + d * RW_LORA:2 * RW_LORA + (d + 1) * RW_LORA]
        u = w0_ref[d:d + 1, :] + _mmx(jnp.tanh(wd), wup_ref[d], "x3")
        logw = -math.exp(-0.5) * jax.nn.sigmoid(u)
        a = jax.nn.sigmoid(a0_ref[d:d + 1, :] + _mmx(ad, aup_ref[d], "x3"))
        kd = k * (1.0 + (a - 1.0) * ka_ref[...])
        lg = _dot_const_lhs(tri_ref[d], logw)
        e_pos = jnp.exp(lg)
        e_neg = 1.0 / e_pos
        at_ref[0, d] = (-kk * jnp.exp(lg - logw)).astype(at_ref.dtype)
        rt_ref[0, d] = (r * e_pos).astype(rt_ref.dtype)
        bt_ref[0, d] = (kk * a * e_neg).astype(bt_ref.dtype)
        kt_ref[0, d] = (kd * e_neg).astype(kt_ref.dtype)
        for c in range(tl // RW_T):
            g_ref[0, d, c] = jnp.exp(jnp.sum(logw[c * RW_T:(c + 1) * RW_T], axis=0, keepdims=True))


def _rw_prepare(proj, p, n_ctx, tl=256):
    bsz, L, _ = proj.shape
    h8 = tl // 8
    nblk8 = L // 8
    main = lambda w, cb: pl.BlockSpec((1, tl, w), lambda b, i: (b, i, cb))
    prev = lambda w, cb: pl.BlockSpec((1, 8, w), lambda b, i: (b, jnp.maximum(i * h8 - 1, 0), cb))
    nxt = lambda w, cb: pl.BlockSpec((1, 8, w), lambda b, i: (b, jnp.minimum((i + 1) * h8, nblk8 - 1), cb))
    full = lambda shape: pl.BlockSpec(shape, lambda b, i: (0,) * len(shape))
    in_specs = []
    args = []
    for name, w in (("r", 1024), ("k", 1024), ("v", 1024), ("wdad", 256)):
        cb = COL[name] // w
        in_specs += [main(w, cb), prev(w, cb), nxt(w, cb)]
        args += [proj, proj, proj]
    mu = p["shift_mu"]
    in_specs += [pl.BlockSpec((2, 1024), lambda b, i: (0, 0)), pl.BlockSpec((2, 1024), lambda b, i: (0, 1)),
                 pl.BlockSpec((2, 1024), lambda b, i: (0, 2)), pl.BlockSpec((2, 256), lambda b, i: (0, 12))]
    args += [mu, mu, mu, mu]
    in_specs += [full((2, RW_LORA, BR)), full((2, BR)), full((2, RW_LORA, BR)), full((2, BR)),
                 full((1, BR)), full((1, BR)), full((1, BR)), full((BR, 128)), full((128, BR)), full((2, tl, tl))]
    args += [p["rw_w_up"], p["rw_w0"], p["rw_a_up"], p["rw_a0"], p["rw_k_k"].reshape(1, BR),
             p["rw_k_a"].reshape(1, BR), p["rw_r_k"].reshape(1, BR), p["sum64"], p["bcast64"], p["tri"]]
    dir_spec = pl.BlockSpec((1, 2, tl, BR), lambda b, i: (b, 0, i, 0))
    tok_spec = pl.BlockSpec((1, tl, BR), lambda b, i: (b, i, 0))
    nchunk = tl // RW_T
    out_specs = [dir_spec, dir_spec, dir_spec, dir_spec, tok_spec, tok_spec,
                 pl.BlockSpec((1, 2, nchunk, 1, BR), lambda b, i: (b, 0, i, 0, 0))]
    dir_shape = jax.ShapeDtypeStruct((bsz, 2, L, BR), BF16)
    out_shape = [dir_shape, dir_shape, dir_shape, dir_shape,
                 jax.ShapeDtypeStruct((bsz, L, BR), BF16), jax.ShapeDtypeStruct((bsz, L, BR), F32),
                 jax.ShapeDtypeStruct((bsz, 2, L // RW_T, 1, BR), F32)]
    return pl.pallas_call(
        functools.partial(_rwprep_kernel, tl=tl, n_ctx=n_ctx, L=L),
        grid=(bsz, L // tl),
        in_specs=in_specs, out_specs=out_specs, out_shape=out_shape,
        compiler_params=_cparams(("parallel", "parallel")),
        name="rwkv_prepare",
    )(*args)


def _scan_chunk(d, j, nc_ctx, nc):
    bwd = jnp.where(j < nc_ctx, nc_ctx - 1 - j, nc - 1 - j + nc_ctx)
    return jnp.where(d == 0, j, bwd)


RW_PREC = {"scores": "bf16", "inv": "bf16", "xs": "bf16", "av": "bf16", "u": "bf16", "y": "bf16", "state": "bf16"}


def _mmx(a, b, mode, dims="nn"):
    f = {"nn": _dot, "nt": _dot_nt, "tn": _dot_tn}[dims]
    if mode == "hi":
        return f(a, b, precision=HI)
    ah, bh = a.astype(BF16), b.astype(BF16)
    if mode == "bf16":
        return f(ah, bh)
    al = (a - ah.astype(F32)).astype(BF16)
    bl = (b - bh.astype(F32)).astype(BF16)
    return f(ah, bh) + f(ah, bl) + f(al, bh)


def _rwscan_kernel(atf_ref, rtf_ref, btf_ref, ktf_ref, vf_ref, gf_ref,
                   atb_ref, rtb_ref, btb_ref, ktb_ref, vb_ref, gb_ref, yf_ref, yb_ref, s_ref):
    j = pl.program_id(1)
    T = RW_T

    @pl.when(j == 0)
    def _():
        s_ref[...] = jnp.zeros_like(s_ref)

    lane = lax.broadcasted_iota(jnp.int32, (2 * T, 128), 1)
    row = lax.broadcasted_iota(jnp.int32, (2 * T, 128), 0)
    own = (lane // RW_N) == (row // T)
    r2 = lax.broadcasted_iota(jnp.int32, (2 * T, 2 * T), 0)
    c2 = lax.broadcasted_iota(jnp.int32, (2 * T, 2 * T), 1)
    same = (r2 // T) == (c2 // T)
    strict = [jnp.logical_and(same, r2 > c2), jnp.logical_and(same, r2 < c2)]
    incl = [jnp.logical_and(same, r2 >= c2), jnp.logical_and(same, r2 <= c2)]
    eye = jnp.where(r2 == c2, 1.0, 0.0)
    base_blk = (r2 // RW_INV_BASE) == (c2 // RW_INV_BASE)
    merge_blks = []
    s = RW_INV_BASE
    while s < T:
        merge_blks.append(jnp.logical_and((r2 // (2 * s)) == (c2 // (2 * s)), (r2 // s) != (c2 // s)))
        s *= 2

    def stack(x):
        return jnp.where(own, jnp.concatenate([x, x], axis=0), 0.0)

    refs = ((atf_ref, rtf_ref, btf_ref, ktf_ref, vf_ref, gf_ref, yf_ref),
            (atb_ref, rtb_ref, btb_ref, ktb_ref, vb_ref, gb_ref, yb_ref))
    chains = [(d, p) for d in range(2) for p in range(BR // 128)]
    dirs = [d for d, _ in chains]
    sls = [slice(p * 128, (p + 1) * 128) for _, p in chains]
    each = lambda fn, *lists: [fn(*xs) for xs in zip(*lists)]
    prec = RW_PREC
    AR = [jnp.concatenate([stack(refs[d][0][0, 0, :, sl]), stack(refs[d][1][0, 0, :, sl])], axis=0)
          for d, sl in zip(dirs, sls)]
    BK = [jnp.concatenate([stack(refs[d][2][0, 0, :, sl]), stack(refs[d][3][0, 0, :, sl])], axis=0)
          for d, sl in zip(dirs, sls)]
    V = [stack(refs[d][4][0, :, sl]) for d, sl in zip(dirs, sls)]
    sc = each(lambda a, b: _mmx(a, b, prec["scores"], "nt"), AR, BK)
    Aab = [jnp.where(strict[d], x[:2 * T, :2 * T], 0.0) for d, x in zip(dirs, sc)]
    Aak = [jnp.where(strict[d], x[:2 * T, 2 * T:], 0.0) for d, x in zip(dirs, sc)]
    Mrbk = [jnp.concatenate([jnp.where(incl[d], x[2 * T:, :2 * T], 0.0),
                             jnp.where(incl[d], x[2 * T:, 2 * T:], 0.0)], axis=1) for d, x in zip(dirs, sc)]

    inv = lambda a, b: _mmx(a, b, prec["inv"])
    N = [jnp.where(base_blk, x, 0.0) for x in Aab]
    P = [eye + x for x in N]
    N = each(inv, N, N)
    NP = each(lambda n, q: inv(jnp.concatenate([n, q], axis=0), n), N, P)
    P = each(lambda q, x: q + x[2 * T:], P, NP)
    P = each(lambda q, x: q + inv(q, x[:2 * T]), P, NP)
    for off_blk in merge_blks:
        CP = each(lambda a, q: inv(jnp.where(off_blk, a, 0.0), q), Aab, P)
        P = each(lambda q, x: q + inv(q, x), P, CP)

    AV = each(lambda a, v: _mmx(a, v, prec["av"]), Aak, V)
    S = [s_ref[d, p] for d, p in chains]
    XS = each(lambda a, s_: _mmx(a, s_, prec["xs"], "nt"), AR, S)
    U = each(lambda q, x, w: _mmx(q, x[:2 * T] + w, prec["u"]), P, XS, AV)
    UV = each(lambda u, v: jnp.concatenate([u.astype(BF16), v], axis=0), U, V)
    Ys = each(lambda x, m, uv: x[2 * T:] + _mmx(m, uv, prec["y"]), XS, Mrbk, UV)
    dS = each(lambda uv, bk: _mmx(uv, bk, prec["state"], "tn"), UV, BK)
    for c, (d, p) in enumerate(chains):
        refs[d][6][0, :, sls[c]] = Ys[c][:T] + Ys[c][T:]
        s_ref[d, p] = (S[c] + dS[c]) * refs[d][5][0, 0, 0, :, sls[c]]


def _rw_scan(at, rt, bt, kt, v, g, n_ctx):
    bsz, _, L, _ = at.shape
    nc, nc_ctx = L // RW_T, n_ctx // RW_T

    def specs(d):
        ch = lambda j: _scan_chunk(d, j, nc_ctx, nc)
        dspec = pl.BlockSpec((1, 1, RW_T, BR), lambda b, j: (b, d, ch(j), 0))
        tspec = pl.BlockSpec((1, RW_T, BR), lambda b, j: (b, ch(j), 0))
        return tspec, [dspec, dspec, dspec, dspec, tspec,
                       pl.BlockSpec((1, 1, 1, 1, BR), lambda b, j: (b, d, ch(j), 0, 0))]

    (out_f, in_f), (out_b, in_b) = specs(0), specs(1)
    shape = jax.ShapeDtypeStruct((bsz, L, BR), F32)
    return pl.pallas_call(
        _rwscan_kernel,
        grid=(bsz, nc),
        in_specs=in_f + in_b,
        out_specs=[out_f, out_b],
        out_shape=[shape, shape],
        scratch_shapes=[pltpu.VMEM((2, BR // 128, 128, 128), F32)],
        compiler_params=_cparams(("parallel", "arbitrary")),
        name="rwkv_scan",
    )(at, rt, bt, kt, v, g, at, rt, bt, kt, v, g)


def _atprep_kernel(q_ref, k_ref, v_ref, cos_ref, sin_ref, qg_ref, kg_ref, qo_ref, ko_ref, vo_ref):
    cos = cos_ref[...]
    sin = sin_ref[...]
    lane = lax.broadcasted_iota(jnp.int32, cos.shape, 1)
    first_half = (lane % 64) < 32

    def norm_rope(x, g):
        y = x * lax.rsqrt(jnp.mean(x * x, axis=-1, keepdims=True) + EPS) * g
        partner = jnp.where(first_half, pltpu.roll(y, 96, axis=1), pltpu.roll(y, 32, axis=1))
        return y * cos + partner * sin

    for h in range(AT_H):
        sl = slice(h * AT_HD, (h + 1) * AT_HD)
        q_scale = AT_HD ** -0.5 * math.log2(math.e)
        qo_ref[0, :, sl] = (norm_rope(q_ref[0, :, sl], qg_ref[...]) * q_scale).astype(qo_ref.dtype)
    for h in range(AT_KV):
        sl = slice(h * AT_HD, (h + 1) * AT_HD)
        ko_ref[0, :, sl] = norm_rope(k_ref[0, :, sl], kg_ref[...]).astype(ko_ref.dtype)
    vo_ref[0] = v_ref[0].astype(vo_ref.dtype)


def _at_prepare(proj, cos, sin, q_g, k_g, tl=256):
    bsz, L, _ = proj.shape
    kvw = AT_KV * AT_HD
    tok = lambda w: pl.BlockSpec((1, tl, w), lambda b, i: (b, i, 0))
    return pl.pallas_call(
        _atprep_kernel,
        grid=(bsz, L // tl),
        in_specs=[pl.BlockSpec((1, tl, BR), lambda b, i: (b, i, COL["at_q"] // BR)),
                  pl.BlockSpec((1, tl, kvw), lambda b, i: (b, i, COL["at_k"] // kvw)),
                  pl.BlockSpec((1, tl, kvw), lambda b, i: (b, i, COL["at_v"] // kvw)),
                  pl.BlockSpec((tl, AT_HD), lambda b, i: (i, 0)),
                  pl.BlockSpec((tl, AT_HD), lambda b, i: (i, 0)),
                  pl.BlockSpec((1, AT_HD), lambda b, i: (0, 0)),
                  pl.BlockSpec((1, AT_HD), lambda b, i: (0, 0))],
        out_specs=[tok(BR), tok(kvw), tok(kvw)],
        out_shape=[jax.ShapeDtypeStruct((bsz, L, BR), BF16),
                   jax.ShapeDtypeStruct((bsz, L, kvw), BF16),
                   jax.ShapeDtypeStruct((bsz, L, kvw), BF16)],
        compiler_params=_cparams(("parallel", "parallel")),
        name="gqa_prepare",
    )(proj, proj, proj, cos, sin, q_g.reshape(1, AT_HD), k_g.reshape(1, AT_HD))


def _attn_kernel(q_ref, k_ref, v_ref, o_ref, s_ref, p_ref, l_ref, *, tq, n_ctx, kb, rows):
    i = pl.program_id(2)
    grp = AT_H // AT_KV
    def attend(n_keys):
        chunks = [slice(c * kb, (c + 1) * kb) for c in range(n_keys // kb)]

        def scores(r):
            s_ref[r, :, :n_keys] = _dot_nt(q_ref[0, :, r * AT_HD:(r + 1) * AT_HD], k_ref[0, :n_keys, :])

        def softmax(r):
            for rb in range(tq // rows):
                rs = slice(rb * rows, (rb + 1) * rows)
                mx = s_ref[r, rs, chunks[0]]
                for ch in chunks[1:]:
                    mx = jnp.maximum(mx, s_ref[r, rs, ch])
                m = jnp.broadcast_to(jnp.max(mx, axis=-1, keepdims=True), mx.shape)
                tot = jnp.zeros_like(mx)
                for ch in chunks:
                    p = jnp.exp2(s_ref[r, rs, ch] - m)
                    tot = tot + p
                    p_ref[r, rs, ch] = p.astype(BF16)
                l_ref[r, rs, :] = jnp.broadcast_to(jnp.sum(tot, axis=-1, keepdims=True), (rows, AT_HD))

        def values(r):
            o_ref[0, :, r * AT_HD:(r + 1) * AT_HD] = _dot(p_ref[r, :, :n_keys], v_ref[0, :n_keys, :]) / l_ref[r]

        stages = (scores, softmax, values)
        for t in range(grp + len(stages) - 1):
            for st, fn in enumerate(stages):
                if 0 <= t - st < grp:
                    fn(t - st)

    @pl.when(i * tq < n_ctx)
    def _():
        attend(n_ctx)

    @pl.when(i * tq >= n_ctx)
    def _():
        attend(k_ref.shape[1])


def _attention(q, k, v, n_ctx, tq=256, kb=256, rows=32):
    bsz, L, _ = q.shape
    gw = (AT_H // AT_KV) * AT_HD
    return pl.pallas_call(
        functools.partial(_attn_kernel, tq=tq, n_ctx=n_ctx, kb=kb, rows=rows),
        scratch_shapes=[pltpu.VMEM((gw // AT_HD, tq, L), F32), pltpu.VMEM((gw // AT_HD, tq, L), BF16),
                        pltpu.VMEM((gw // AT_HD, tq, AT_HD), F32)],
        grid=(bsz, AT_KV, L // tq),
        in_specs=[pl.BlockSpec((1, tq, gw), lambda b, g, i: (b, i, g)),
                  pl.BlockSpec((1, L, AT_HD), lambda b, g, i: (b, 0, g)),
                  pl.BlockSpec((1, L, AT_HD), lambda b, g, i: (b, 0, g))],
        out_specs=pl.BlockSpec((1, tq, gw), lambda b, g, i: (b, i, g)),
        out_shape=jax.ShapeDtypeStruct((bsz, L, BR), F32),
        compiler_params=_cparams(("parallel", "parallel", "parallel")),
        name="gqa_attention",
    )(q, k, v)


def _cap_gates(pre):
    return GATE_CAP * jnp.tanh(pre / GATE_CAP)


def _log_sigmoid(x):
    return jnp.minimum(x, 0.0) - jnp.log1p(jnp.exp(-jnp.abs(x)))


def _mlstm_kernel(qf_ref, kf_ref, vf_ref, gcf_ref, qb_ref, kb_ref, vb_ref, gcb_ref,
                  bc_ref, br_ref, hf_ref, hb_ref, c_ref, n_ref, m_ref):
    j = pl.program_id(1)
    T = ML_T

    @pl.when(j == 0)
    def _():
        c_ref[...] = jnp.zeros_like(c_ref)
        n_ref[...] = jnp.zeros_like(n_ref)
        m_ref[...] = jnp.zeros_like(m_ref)

    r2 = lax.broadcasted_iota(jnp.int32, (T, T), 0)
    c2 = lax.broadcasted_iota(jnp.int32, (T, T), 1)
    lane16 = lax.broadcasted_iota(jnp.int32, (T, 16), 1)
    sub16 = lax.broadcasted_iota(jnp.int32, (16, T), 0)

    refs = ((qf_ref, kf_ref, vf_ref, gcf_ref, None, hf_ref), (qb_ref, kb_ref, vb_ref, gcb_ref, None, hb_ref))
    seen = [c2 <= r2, c2 >= r2]
    seen_t = [r2 <= c2, r2 >= c2]
    gc = [_cap_gates(refs[d][3][0][:, :16] + bc_ref[...]) for d in range(2)]
    gr = [_cap_gates(refs[d][3][0].T[:16, :] + br_ref[...]) for d in range(2)]
    lsc = [_log_sigmoid(x) for x in gc]
    lsr = [_log_sigmoid(x) for x in gr]

    chains = [(d, h) for d in range(2) for h in range(ML_H)]
    each = lambda fn, *lists: [fn(*xs) for xs in zip(*lists)]
    pick_row = lambda x, idx: jnp.sum(jnp.where(sub16 == idx, x, 0.0), axis=0, keepdims=True)
    pick_col = lambda x, idx: jnp.sum(jnp.where(lane16 == idx, x, 0.0), axis=1, keepdims=True)
    li_row = [pick_row(gr[d], d * ML_H + h) for d, h in chains]
    lf_row = [pick_row(lsr[d], (2 + d) * ML_H + h) for d, h in chains]
    li_col = [pick_col(gc[d], d * ML_H + h) for d, h in chains]
    lf_col = [pick_col(lsc[d], (2 + d) * ML_H + h) for d, h in chains]
    b_col = [jnp.sum(jnp.where(seen[d], x, 0.0), axis=1, keepdims=True) for (d, _), x in zip(chains, lf_row)]
    b_row = [jnp.sum(jnp.where(seen_t[d], x, 0.0), axis=0, keepdims=True) for (d, _), x in zip(chains, lf_col)]
    g = [jnp.sum(x, axis=0, keepdims=True) for x in lf_col]
    m_prev = [m_ref[d, h] for d, h in chains]
    q = [refs[d][0][0, :, h * ML_DK:(h + 1) * ML_DK] * (ML_DK ** -0.5) for d, h in chains]
    k = [refs[d][1][0, :, h * ML_DK:(h + 1) * ML_DK] for d, h in chains]
    vb = [refs[d][2][0, :, h * ML_DV:(h + 1) * ML_DV].astype(BF16) for d, h in chains]
    C = [c_ref[d, h] for d, h in chains]
    n = [n_ref[d, h] for d, h in chains]
    qb = [x.astype(BF16) for x in q]

    dmat = [jnp.where(seen[d], bc - br + li, -jnp.inf) for (d, _), bc, br, li in zip(chains, b_col, b_row, li_row)]
    m_inter = each(lambda bc, m: bc + m, b_col, m_prev)
    m_t = each(lambda mi, dm: jnp.maximum(mi, jnp.max(dm, axis=-1, keepdims=True)), m_inter, dmat)
    w_inter = each(lambda mi, mt: jnp.exp(mi - mt), m_inter, m_t)
    qk = each(lambda a, b: _dot_nt(a, b.astype(BF16)), qb, k)
    qc = each(lambda a, b: _dot(a, b.astype(BF16)), qb, C)
    s = each(lambda x, dm, mt: x * jnp.exp(dm - mt), qk, dmat, m_t)
    sv = each(lambda a, b: _dot(a.astype(BF16), b), s, vb)
    qn = each(lambda a, b: jnp.sum(a * b, axis=-1, keepdims=True), q, n)
    den = each(lambda w, a, x: w * a + jnp.sum(x, axis=-1, keepdims=True), w_inter, qn, s)
    for (d, h), w, a, b, dn, mt in zip(chains, w_inter, qc, sv, den, m_t):
        refs[d][5][0, :, h * ML_DV:(h + 1) * ML_DV] = (w * a + b) / jnp.maximum(jnp.abs(dn), jnp.exp(-mt))

    loga = each(lambda g_, bc, li: g_ - bc + li, g, b_col, li_col)
    m_new = each(lambda g_, m, la: jnp.maximum(g_ + m, jnp.max(la, axis=0, keepdims=True)), g, m_prev, loga)
    carry = each(lambda g_, m, mn: jnp.exp(g_ + m - mn), g, m_prev, m_new)
    wk = each(lambda la, mn, k_: jnp.exp(la - mn) * k_, loga, m_new, k)
    kv = each(lambda a, b: _dot_tn(a.astype(BF16), b), wk, vb)
    for (d, h), cr, c_, kv_, n_, wk_, mn in zip(chains, carry, C, kv, n, wk, m_new):
        c_ref[d, h] = cr * c_ + kv_
        n_ref[d, h] = cr * n_ + jnp.sum(wk_, axis=0, keepdims=True)
        m_ref[d, h] = mn


def _mlstm(proj, bias_col, bias_row, n_ctx):
    bsz, L, _ = proj.shape
    nc, nc_ctx = L // ML_T, n_ctx // ML_T
    qw, vw = ML_H * ML_DK, ML_H * ML_DV

    def dir_specs(d):
        ch = lambda j: _scan_chunk(d, j, nc_ctx, nc)
        return [pl.BlockSpec((1, ML_T, qw), lambda b, j: (b, ch(j), COL["ml_q"] // qw)),
                pl.BlockSpec((1, ML_T, qw), lambda b, j: (b, ch(j), COL["ml_k"] // qw)),
                pl.BlockSpec((1, ML_T, vw), lambda b, j: (b, ch(j), COL["ml_v"] // vw)),
                pl.BlockSpec((1, ML_T, 128), lambda b, j: (b, ch(j), COL["ml_if"] // 128))]

    def out_spec(d):
        ch = lambda j: _scan_chunk(d, j, nc_ctx, nc)
        return pl.BlockSpec((1, ML_T, vw), lambda b, j: (b, ch(j), 0))

    shape = jax.ShapeDtypeStruct((bsz, L, vw), F32)
    return pl.pallas_call(
        _mlstm_kernel,
        grid=(bsz, nc),
        in_specs=dir_specs(0) + dir_specs(1) + [pl.BlockSpec((1, 16), lambda b, j: (0, 0)),
                                                pl.BlockSpec((16, 1), lambda b, j: (0, 0))],
        out_specs=[out_spec(0), out_spec(1)],
        out_shape=[shape, shape],
        scratch_shapes=[pltpu.VMEM((2, ML_H, ML_DK, ML_DV), F32),
                        pltpu.VMEM((2, ML_H, 1, ML_DK), F32),
                        pltpu.VMEM((2, ML_H, 1, 1), F32)],
        compiler_params=_cparams(("parallel", "arbitrary")),
        name="mlstm_scan",
    )(proj, proj, proj, proj, proj, proj, proj, proj, bias_col, bias_row)


def _epilogue_kernel(yf_ref, yb_ref, bonus_ref, rwg_ref, att_ref, atg_ref, hf_ref, hb_ref, mlo_ref, mlg_ref,
                     lnw_ref, lnb_ref, mng_ref, m64_ref, b64_ref, m256_ref, b256_ref, o_ref):
    y = yf_ref[0] + yb_ref[0]
    mu = _seg_reduce(y, m64_ref[...], b64_ref[...])
    yc = y - mu
    var = _seg_reduce(yc * yc, m64_ref[...], b64_ref[...])
    ya = yc * lax.rsqrt(var + RW_GN_EPS) * lnw_ref[...] + lnb_ref[...] + bonus_ref[0]
    o_ref[0, 0] = (ya * _silu(rwg_ref[0])).astype(o_ref.dtype)

    o_ref[1, 0] = (att_ref[0] * _silu(atg_ref[0])).astype(o_ref.dtype)

    hh = hf_ref[0] + hb_ref[0]
    ms = _seg_reduce(hh * hh, m256_ref[...], b256_ref[...])
    hn = hh * lax.rsqrt(ms + EPS) * mng_ref[...]
    o_ref[2, 0] = (jax.nn.sigmoid(mlo_ref[0]) * hn * _silu(mlg_ref[0])).astype(o_ref.dtype)


def _epilogue(proj, y_f, y_b, bonus, att, h_f, h_b, p, tl=256):
    bsz, L, _ = proj.shape
    tok = pl.BlockSpec((1, tl, BR), lambda b, i: (b, i, 0))
    pc = lambda name: pl.BlockSpec((1, tl, BR), lambda b, i: (b, i, COL[name] // BR))
    vec = pl.BlockSpec((1, BR), lambda b, i: (0, 0))
    gat = pl.BlockSpec((BR, 128), lambda b, i: (0, 0))
    sct = pl.BlockSpec((128, BR), lambda b, i: (0, 0))
    return pl.pallas_call(
        _epilogue_kernel,
        grid=(bsz, L // tl),
        in_specs=[tok, tok, tok, pc("rw_g"), tok, pc("at_g"), tok, tok, pc("ml_o"), pc("ml_g"),
                  vec, vec, vec, gat, sct, gat, sct],
        out_specs=pl.BlockSpec((3, 1, tl, BR), lambda b, i: (0, b, i, 0)),
        out_shape=jax.ShapeDtypeStruct((3, bsz, L, BR), BF16),
        compiler_params=_cparams(("parallel", "parallel")),
        name="branch_epilogue",
    )(y_f, y_b, bonus, proj, att, proj, h_f, h_b, proj, proj,
      p["rw_ln_w"].reshape(1, BR), p["rw_ln_b"].reshape(1, BR), p["ml_norm_g"].reshape(1, BR),
      p["mean64"], p["bcast64"], p["mean256"], p["bcast256"])


def _merge_kernel(y_ref, w_ref, g0_ref, g1_ref, g2_ref, o_ref, wb_ref):
    @pl.when(pl.program_id(1) == 0)
    def _():
        wb_ref[...] = w_ref[0].astype(BF16)

    acc = jax.nn.sigmoid(g0_ref[...]) * _dot(y_ref[0], wb_ref[0])
    acc += jax.nn.sigmoid(g1_ref[...]) * _dot(y_ref[1], wb_ref[1])
    acc += jax.nn.sigmoid(g2_ref[...]) * _dot(y_ref[2], wb_ref[2])
    o_ref[...] = acc.astype(o_ref.dtype)


def _merge(ys, w_branch, layer, proj2d, tm=512, tn=1024):
    _, m, _ = ys.shape
    nb = D // tn
    gate = lambda n: pl.BlockSpec((tm, tn), lambda j, i: (i, n * nb + j))
    return pl.pallas_call(
        _merge_kernel,
        grid=(D // tn, m // tm),
        in_specs=[pl.BlockSpec((3, tm, BR), lambda j, i: (0, i, 0)),
                  pl.BlockSpec((1, 3, BR, tn), lambda j, i: (layer, 0, 0, j), pipeline_mode=pl.Buffered(1)),
                  gate(0), gate(1), gate(2)],
        out_specs=pl.BlockSpec((tm, tn), lambda j, i: (i, j)),
        out_shape=jax.ShapeDtypeStruct((m, D), BF16),
        scratch_shapes=[pltpu.VMEM((3, BR, tn), BF16)],
        compiler_params=_cparams(("arbitrary", "arbitrary")),
        name="branch_merge",
    )(ys, w_branch, proj2d, proj2d, proj2d)


def _outproj_kernel(*refs, nct):
    a_ref, w_ref, gt_ref, *z_refs, o_ref, wb_ref = refs

    @pl.when(jnp.logical_and(pl.program_id(1) == 0, pl.program_id(2) == 0))
    def _():
        wb_ref[...] = w_ref[0].astype(BF16)

    def body(z):
        o_ref[0] = z + gt_ref[0, 0] * _dot(a_ref[0], wb_ref[...])

    _for_stream_tile(z_refs, pl.program_id(2) < nct, body)


def _out_projection(mixed, w_out, layer, z, gate, n_ctx, tl=256, tn=1024):
    bsz, L, _ = mixed.shape
    nct = n_ctx // tl
    z_specs, z_args = _stream_specs(z, tl, nct, (1, tl, tn), lambda j, b, i: (b, i, j))
    return pl.pallas_call(
        functools.partial(_outproj_kernel, nct=nct),
        grid=(D // tn, bsz, L // tl),
        in_specs=[pl.BlockSpec((1, tl, D), lambda j, b, i: (b, i, 0)),
                  pl.BlockSpec((1, D, tn), lambda j, b, i: (layer, 0, j)),
                  pl.BlockSpec((1, 1, 1, tn), lambda j, b, i: (b, (i >= nct).astype(jnp.int32), 0, j))] + z_specs,
        out_specs=pl.BlockSpec((1, tl, tn), lambda j, b, i: (b, i, j)),
        out_shape=jax.ShapeDtypeStruct((bsz, L, D), F32),
        scratch_shapes=[pltpu.VMEM((D, tn), BF16)],
        compiler_params=_cparams(("arbitrary", "arbitrary", "arbitrary")),
        name="out_projection",
    )(mixed, w_out, gate, *z_args)


def _rope_tables(n_ctx, n_lat):
    rows = n_lat // GRID_W
    row = jnp.repeat(jnp.arange(rows), GRID_W).astype(F32)
    col = jnp.tile(jnp.arange(GRID_W), rows).astype(F32)
    inv_freq = ROPE_THETA ** (-jnp.arange(0, AT_HD // 2, 2, dtype=F32) / (AT_HD // 2))
    ang_lat = jnp.stack([row[:, None] * inv_freq, col[:, None] * inv_freq], axis=1)
    ang = jnp.concatenate([jnp.zeros((n_ctx, 2, AT_HD // 4), F32), ang_lat], axis=0)
    cos, sin = jnp.cos(ang), jnp.sin(ang)
    cos_t = jnp.concatenate([cos[:, 0], cos[:, 0], cos[:, 1], cos[:, 1]], axis=-1)
    sin_t = jnp.concatenate([-sin[:, 0], sin[:, 0], -sin[:, 1], sin[:, 1]], axis=-1)
    return cos_t, sin_t


def _group_consts(width, value):
    member = (np.arange(BR)[:, None] // width) == np.arange(128)[None, :]
    return (jnp.asarray(np.where(member, value, 0.0), dtype=BF16),
            jnp.asarray(np.where(member.T, 1.0, 0.0), dtype=BF16))


def _chunk_tri(tl):
    t = np.arange(tl)
    same = (t[:, None] // RW_T) == (t[None, :] // RW_T)
    fwd = same & (t[None, :] <= t[:, None])
    bwd = same & (t[None, :] >= t[:, None])
    return jnp.asarray(np.stack([fwd, bwd]).astype(np.float32), dtype=BF16)


def kernel(x, c, ctx, c_ctx, norm_g, w_ada, b_ada, w_in, shift_mu, rw_w_up, rw_w0, rw_a_up, rw_a0, rw_k_k, rw_k_a, rw_r_k, rw_ln_w, rw_ln_b, at_q_g, at_k_g, ml_gate_b, ml_norm_g, w_branch, w_out, final_g):
    bsz, n_lat, _ = x.shape
    n_ctx = ctx.shape[1]
    L = n_ctx + n_lat
    depth = w_in.shape[0]

    cos_t, sin_t = _rope_tables(n_ctx, n_lat)
    sum64, bcast64 = _group_consts(RW_N, 1.0)
    mean64, _ = _group_consts(RW_N, 1.0 / RW_N)
    mean256, bcast256 = _group_consts(ML_DV, 1.0 / ML_DV)
    consts = {"sum64": sum64, "bcast64": bcast64, "mean64": mean64, "mean256": mean256, "bcast256": bcast256,
              "tri": _chunk_tri(256)}

    cc = jnp.concatenate([c, c_ctx[None], jnp.zeros((8 - bsz - 1, D), F32)], axis=0)
    mod = _modulation(cc, w_ada, b_ada)

    z = (ctx, x)
    for l in range(depth):
        sh, sc, gt = mod[l, :, :D], mod[l, :, D:2 * D], mod[l, :, 2 * D:]
        pick = lambda t: jnp.stack([jnp.broadcast_to(t[bsz], (bsz, D)), t[:bsz]], axis=1)[:, :, None, :]
        scale = pick((1.0 + sc) * norm_g[l])
        shift = pick(sh)
        gate = pick(gt)

        h = _norm_mod(z, scale, shift, n_ctx, L, BF16)
        proj2d = _in_projection(h.reshape(bsz * L, D), w_in, l)
        proj = proj2d.reshape(bsz, L, N_PROJ)

        p = dict(consts, shift_mu=shift_mu[l], rw_w_up=rw_w_up[l], rw_w0=rw_w0[l], rw_a_up=rw_a_up[l],
                 rw_a0=rw_a0[l], rw_k_k=rw_k_k[l], rw_k_a=rw_k_a[l], rw_r_k=rw_r_k[l], rw_ln_w=rw_ln_w[l],
                 rw_ln_b=rw_ln_b[l], ml_norm_g=ml_norm_g[l])
        at, rt, bt, kt, v_rw, bonus, g_rw = _rw_prepare(proj, p, n_ctx)
        y_f, y_b = _rw_scan(at, rt, bt, kt, v_rw, g_rw, n_ctx)

        qn, kn, vn = _at_prepare(proj, cos_t, sin_t, at_q_g[l], at_k_g[l])
        att = _attention(qn, kn, vn, n_ctx)

        bias = ml_gate_b[l].reshape(16)
        h_f, h_b = _mlstm(proj, bias.reshape(1, 16), bias.reshape(16, 1), n_ctx)

        ys = _epilogue(proj, y_f, y_b, bonus, att, h_f, h_b, p)
        mixed = _merge(ys.reshape(3, bsz * L, BR), w_branch, l, proj2d)
        z = _out_projection(mixed.reshape(bsz, L, D), w_out, l, z, gate, n_ctx)

    ones = jnp.ones((bsz, 2, 1, D), F32) * final_g
    zeros = jnp.zeros((bsz, 2, 1, D), F32)
    return _norm_mod(z, ones, zeros, n_ctx, L, F32, latent_only=True)
```

```python
import functools
import math

import numpy as np
import jax
import jax.numpy as jnp
from jax import lax
from jax.experimental import pallas as pl
from jax.experimental.pallas import tpu as pltpu

F32 = jnp.float32
BF16 = jnp.bfloat16
HI = lax.Precision.HIGHEST

D = 2048
BR = 1024
EPS = 1e-6
GRID_W = 64

RW_H, RW_N, RW_LORA = 16, 64, 64
RW_GN_EPS = 64e-5
RW_T = 64
RW_INV_BASE = 8

AT_H, AT_KV, AT_HD = 8, 2, 128
ROPE_THETA = 10000.0

ML_H, ML_DK, ML_DV, ML_T = 4, 128, 256, 128
GATE_CAP = 15.0

D_IN = 17168
LANES, SUBLANES = 128, 8
PROJ_TN = 512
_PROJ_GROUPS = (
    ("merge", 11024, 6144), ("r", 0, 1024), ("k", 1024, 1024), ("v", 2048, 1024),
    ("rw_g", 3328, 1024), ("at_q", 4352, 1024), ("at_g", 5888, 1024), ("ml_v", 7936, 1024),
    ("ml_o", 8960, 1024), ("ml_g", 10000, 1024), ("ml_q", 6912, 512), ("ml_k", 7424, 512),
    ("at_k", 5376, 512), ("wdad", 3072, 512), ("ml_if", 9984, 512),
)
COL = {}
PROJ_SRC = []
for _name, _start, _width in _PROJ_GROUPS:
    COL[_name] = len(PROJ_SRC) * PROJ_TN
    PROJ_SRC += [_start + t * PROJ_TN for t in range(_width // PROJ_TN)]
COL["at_v"] = COL["at_k"] + 256
N_PROJ = len(PROJ_SRC) * PROJ_TN
for _name, _blk in (("merge", 2048), ("r", 1024), ("k", 1024), ("v", 1024), ("rw_g", 1024), ("at_q", 1024),
                    ("at_g", 1024), ("ml_v", 1024), ("ml_o", 1024), ("ml_g", 1024), ("ml_q", 512), ("ml_k", 512),
                    ("at_k", 256), ("at_v", 256), ("wdad", 256), ("ml_if", 128)):
    assert COL[_name] % _blk == 0, _name

VMEM_LIMIT = 48 * 1024 * 1024


def _cparams(sem):
    return pltpu.CompilerParams(dimension_semantics=sem, vmem_limit_bytes=VMEM_LIMIT)


def _dot(a, b, **kw):
    return jnp.dot(a, b, preferred_element_type=F32, **kw)


def _dot_nt(a, b, **kw):
    return lax.dot_general(a, b, (((1,), (1,)), ((), ())), preferred_element_type=F32, **kw)


def _dot_tn(a, b, **kw):
    return lax.dot_general(a, b, (((0,), (0,)), ((), ())), preferred_element_type=F32, **kw)


def _split3(x):
    h = x.astype(BF16)
    r = x - h.astype(F32)
    m = r.astype(BF16)
    l = (r - m.astype(F32)).astype(BF16)
    return h, m, l


def _split2(x):
    h = x.astype(BF16)
    return h, (x - h.astype(F32)).astype(BF16)


def _seg_reduce(x, gather, scatter):
    h, l = _split2(x)
    sh, sl = _split2(_dot(h, gather) + _dot(l, gather))
    return _dot(sh, scatter) + _dot(sl, scatter)


def _dot_const_lhs(c, x):
    h, m, l = _split3(x)
    return _dot(c, h) + _dot(c, m) + _dot(c, l)


def _silu(x):
    return x * jax.nn.sigmoid(x)


def _mod_kernel(c_ref, w_ref, b_ref, o_ref):
    @pl.when(pl.program_id(1) == 0)
    def _():
        o_ref[0] = jnp.broadcast_to(b_ref[0], o_ref.shape[1:])

    xh, xl = _split2(_silu(c_ref[...]))
    wh, wl = _split2(w_ref[0])
    o_ref[0] += _dot(xh, wh) + _dot(xh, wl) + _dot(xl, wh)


def _modulation(cc, w_ada, b_ada, tk=256):
    depth = w_ada.shape[0]
    return pl.pallas_call(
        _mod_kernel,
        grid=(depth, D // tk),
        in_specs=[pl.BlockSpec((8, tk), lambda l, k: (0, k)),
                  pl.BlockSpec((1, tk, 3 * D), lambda l, k: (l, k, 0)),
                  pl.BlockSpec((1, 1, 3 * D), lambda l, k: (l, 0, 0))],
        out_specs=pl.BlockSpec((1, 8, 3 * D), lambda l, k: (l, 0, 0)),
        out_shape=jax.ShapeDtypeStruct((depth, 8, 3 * D), F32),
        compiler_params=_cparams(("parallel", "arbitrary")),
        name="adaln_modulation",
    )(cc, w_ada, b_ada.reshape(depth, 1, 3 * D))


def _stream_specs(z, tl, nct, block, index):
    if not isinstance(z, tuple):
        return [pl.BlockSpec(block, lambda *g: index(*g))], [z]

    def ctx_index(*g):
        b, i, j = index(*g)
        return b, jnp.minimum(i, nct - 1), j

    def lat_index(*g):
        b, i, j = index(*g)
        return b, jnp.maximum(i - nct, 0), j

    return [pl.BlockSpec(block, ctx_index), pl.BlockSpec(block, lat_index)], list(z)


def _for_stream_tile(z_refs, is_ctx, body):
    if len(z_refs) == 1:
        body(z_refs[0][0])
        return
    pl.when(is_ctx)(lambda: body(z_refs[0][0]))
    pl.when(jnp.logical_not(is_ctx))(lambda: body(z_refs[1][0]))


def _norm_kernel(*refs, nct, first):
    *z_refs, sc_ref, sh_ref, o_ref = refs

    def body(x):
        y = x * lax.rsqrt(jnp.mean(x * x, axis=-1, keepdims=True) + EPS)
        o_ref[0] = (y * sc_ref[0, 0] + sh_ref[0, 0]).astype(o_ref.dtype)

    _for_stream_tile(z_refs, pl.program_id(1) + first < nct, body)


def _norm_mod(z, scale, shift, n_ctx, L, out_dtype, latent_only=False, tl=256):
    bsz = scale.shape[0]
    nct = n_ctx // tl
    first = nct if latent_only else 0
    seg = lambda b, i: (b, (i + first >= nct).astype(jnp.int32), 0, 0)
    z_specs, z_args = _stream_specs(z, tl, nct, (1, tl, D), lambda b, i: (b, i + first, 0))
    return pl.pallas_call(
        functools.partial(_norm_kernel, nct=nct, first=first),
        grid=(bsz, L // tl - first),
        in_specs=z_specs + [pl.BlockSpec((1, 1, 1, D), seg), pl.BlockSpec((1, 1, 1, D), seg)],
        out_specs=pl.BlockSpec((1, tl, D), lambda b, i: (b, i, 0)),
        out_shape=jax.ShapeDtypeStruct((bsz, L - first * tl, D), out_dtype),
        compiler_params=_cparams(("parallel", "parallel")),
        name="rmsnorm_modulate",
    )(*z_args, scale, shift)


def _inproj_kernel(start_ref, a_ref, wt_ref, o_ref):
    o_ref[...] = _dot_nt(a_ref[...], wt_ref[0].astype(BF16))


def _in_projection(a, w_t, layer, tm=2176):
    m, k = a.shape
    tm = math.gcd(m, tm)
    assert all(s % SUBLANES == 0 for s in PROJ_SRC)
    start = np.array(PROJ_SRC, np.int32) // SUBLANES
    grid_spec = pltpu.PrefetchScalarGridSpec(
        num_scalar_prefetch=1,
        grid=(m // tm, len(PROJ_SRC)),
        in_specs=[pl.BlockSpec((tm, k), lambda i, j, st: (i, 0)),
                  pl.BlockSpec((pl.Element(1), pl.Element(PROJ_TN), pl.Element(k)),
                               lambda i, j, st: (layer, st[j] * SUBLANES, 0))],
        out_specs=pl.BlockSpec((tm, PROJ_TN), lambda i, j, st: (i, j)),
    )
    return pl.pallas_call(
        _inproj_kernel,
        grid_spec=grid_spec,
        out_shape=jax.ShapeDtypeStruct((m, N_PROJ), F32),
        compiler_params=_cparams(("parallel", "arbitrary")),
        name="in_projection",
    )(jnp.asarray(start), a, w_t)


def _rwprep_kernel(r_ref, rp_ref, rn_ref, k_ref, kp_ref, kn_ref, v_ref, vp_ref, vn_ref,
                   w_ref, wp_ref, wn_ref, mur_ref, muk_ref, muv_ref, muw_ref,
                   wup_ref, w0_ref, aup_ref, a0_ref, kk_ref, ka_ref, rk_ref, gat_ref, sct_ref, tri_ref,
                   at_ref, rt_ref, bt_ref, kt_ref, vo_ref, bonus_ref, g_ref, *, tl, n_ctx, L):
    i = pl.program_id(1)
    start = i * tl
    has_prev = jnp.logical_and(start != 0, start != n_ctx).astype(F32)
    has_next = jnp.logical_and(start + tl != n_ctx, start + tl != L).astype(F32)

    def shift(x_ref, p_ref, n_ref, mu_ref):
        x = x_ref[0]
        row = lax.broadcasted_iota(jnp.int32, x.shape, 0)
        prev = jnp.where(row == 0, p_ref[0, 7:8, :] * has_prev, pltpu.roll(x, 1, axis=0))
        nxt = jnp.where(row == tl - 1, n_ref[0, 0:1, :] * has_next, pltpu.roll(x, tl - 1, axis=0))
        return x + mu_ref[0:1, :] * (prev - x) + mu_ref[1:2, :] * (nxt - x)

    r = shift(r_ref, rp_ref, rn_ref, mur_ref)
    k = shift(k_ref, kp_ref, kn_ref, muk_ref)
    v = shift(v_ref, vp_ref, vn_ref, muv_ref)
    wdad = shift(w_ref, wp_ref, wn_ref, muw_ref)
    gat, sct = gat_ref[...], sct_ref[...]

    kk = k * kk_ref[...]
    kk = kk / jnp.maximum(jnp.sqrt(_seg_reduce(kk * kk, gat, sct)), 1e-12)
    bonus_ref[0] = _seg_reduce(r * k * rk_ref[...], gat, sct) * v
    vo_ref[0] = v.astype(vo_ref.dtype)

    for d in range(2):
        wd = wdad[:, d * RW_LORA:(d + 1) * RW_LORA]
        ad = wdad[:, 2 * RW_LORA + d * RW_LORA:2 * RW_LORA + (d + 1) * RW_LORA]
        u = w0_ref[d:d + 1, :] + _mmx(jnp.tanh(wd), wup_ref[d], "x3")
        logw = -math.exp(-0.5) * jax.nn.sigmoid(u)
        a = jax.nn.sigmoid(a0_ref[d:d + 1, :] + _mmx(ad, aup_ref[d], "x3"))
        kd = k * (1.0 + (a - 1.0) * ka_ref[...])
        lg = _dot_const_lhs(tri_ref[d], logw)
        e_pos = jnp.exp(lg)
        e_neg = 1.0 / e_pos
        at_ref[0, d] = (-kk * jnp.exp(lg - logw)).astype(at_ref.dtype)
        rt_ref[0, d] = (r * e_pos).astype(rt_ref.dtype)
        bt_ref[0, d] = (kk * a * e_neg).astype(bt_ref.dtype)
        kt_ref[0, d] = (kd * e_neg).astype(kt_ref.dtype)
        for c in range(tl // RW_T):
            g_ref[0, d, c] = jnp.exp(jnp.sum(logw[c * RW_T:(c + 1) * RW_T], axis=0, keepdims=True))


def _rw_prepare(proj, p, n_ctx, tl=256):
    bsz, L, _ = proj.shape
    h8 = tl // 8
    nblk8 = L // 8
    main = lambda w, cb: pl.BlockSpec((1, tl, w), lambda b, i: (b, i, cb))
    prev = lambda w, cb: pl.BlockSpec((1, 8, w), lambda b, i: (b, jnp.maximum(i * h8 - 1, 0), cb))
    nxt = lambda w, cb: pl.BlockSpec((1, 8, w), lambda b, i: (b, jnp.minimum((i + 1) * h8, nblk8 - 1), cb))
    full = lambda shape: pl.BlockSpec(shape, lambda b, i: (0,) * len(shape))
    in_specs = []
    args = []
    for name, w in (("r", 1024), ("k", 1024), ("v", 1024), ("wdad", 256)):
        cb = COL[name] // w
        in_specs += [main(w, cb), prev(w, cb), nxt(w, cb)]
        args += [proj, proj, proj]
    mu = p["shift_mu"]
    in_specs += [pl.BlockSpec((2, 1024), lambda b, i: (0, 0)), pl.BlockSpec((2, 1024), lambda b, i: (0, 1)),
                 pl.BlockSpec((2, 1024), lambda b, i: (0, 2)), pl.BlockSpec((2, 256), lambda b, i: (0, 12))]
    args += [mu, mu, mu, mu]
    in_specs += [full((2, RW_LORA, BR)), full((2, BR)), full((2, RW_LORA, BR)), full((2, BR)),
                 full((1, BR)), full((1, BR)), full((1, BR)), full((BR, 128)), full((128, BR)), full((2, tl, tl))]
    args += [p["rw_w_up"], p["rw_w0"], p["rw_a_up"], p["rw_a0"], p["rw_k_k"].reshape(1, BR),
             p["rw_k_a"].reshape(1, BR), p["rw_r_k"].reshape(1, BR), p["sum64"], p["bcast64"], p["tri"]]
    dir_spec = pl.BlockSpec((1, 2, tl, BR), lambda b, i: (b, 0, i, 0))
    tok_spec = pl.BlockSpec((1, tl, BR), lambda b, i: (b, i, 0))
    nchunk = tl // RW_T
    out_specs = [dir_spec, dir_spec, dir_spec, dir_spec, tok_spec, tok_spec,
                 pl.BlockSpec((1, 2, nchunk, 1, BR), lambda b, i: (b, 0, i, 0, 0))]
    dir_shape = jax.ShapeDtypeStruct((bsz, 2, L, BR), BF16)
    out_shape = [dir_shape, dir_shape, dir_shape, dir_shape,
                 jax.ShapeDtypeStruct((bsz, L, BR), BF16), jax.ShapeDtypeStruct((bsz, L, BR), F32),
                 jax.ShapeDtypeStruct((bsz, 2, L // RW_T, 1, BR), F32)]
    return pl.pallas_call(
        functools.partial(_rwprep_kernel, tl=tl, n_ctx=n_ctx, L=L),
        grid=(bsz, L // tl),
        in_specs=in_specs, out_specs=out_specs, out_shape=out_shape,
        compiler_params=_cparams(("parallel", "parallel")),
        name="rwkv_prepare",
    )(*args)


def _scan_chunk(d, j, nc_ctx, nc):
    bwd = jnp.where(j < nc_ctx, nc_ctx - 1 - j, nc - 1 - j + nc_ctx)
    return jnp.where(d == 0, j, bwd)


RW_PREC = {"scores": "bf16", "inv": "bf16", "xs": "bf16", "av": "bf16", "u": "bf16", "y": "bf16", "state": "bf16"}


def _mmx(a, b, mode, dims="nn"):
    f = {"nn": _dot, "nt": _dot_nt, "tn": _dot_tn}[dims]
    if mode == "hi":
        return f(a, b, precision=HI)
    ah, bh = a.astype(BF16), b.astype(BF16)
    if mode == "bf16":
        return f(ah, bh)
    al = (a - ah.astype(F32)).astype(BF16)
    bl = (b - bh.astype(F32)).astype(BF16)
    return f(ah, bh) + f(ah, bl) + f(al, bh)


def _rwscan_kernel(atf_ref, rtf_ref, btf_ref, ktf_ref, vf_ref, gf_ref,
                   atb_ref, rtb_ref, btb_ref, ktb_ref, vb_ref, gb_ref, yf_ref, yb_ref, s_ref):
    j = pl.program_id(1)
    T = RW_T

    @pl.when(j == 0)
    def _():
        s_ref[...] = jnp.zeros_like(s_ref)

    lane = lax.broadcasted_iota(jnp.int32, (2 * T, 128), 1)
    row = lax.broadcasted_iota(jnp.int32, (2 * T, 128), 0)
    own = (lane // RW_N) == (row // T)
    r2 = lax.broadcasted_iota(jnp.int32, (2 * T, 2 * T), 0)
    c2 = lax.broadcasted_iota(jnp.int32, (2 * T, 2 * T), 1)
    same = (r2 // T) == (c2 // T)
    strict = [jnp.logical_and(same, r2 > c2), jnp.logical_and(same, r2 < c2)]
    incl = [jnp.logical_and(same, r2 >= c2), jnp.logical_and(same, r2 <= c2)]
    eye = jnp.where(r2 == c2, 1.0, 0.0)
    base_blk = (r2 // RW_INV_BASE) == (c2 // RW_INV_BASE)
    merge_blks = []
    s = RW_INV_BASE
    while s < T:
        merge_blks.append(jnp.logical_and((r2 // (2 * s)) == (c2 // (2 * s)), (r2 // s) != (c2 // s)))
        s *= 2

    def stack(x):
        return jnp.where(own, jnp.concatenate([x, x], axis=0), 0.0)

    refs = ((atf_ref, rtf_ref, btf_ref, ktf_ref, vf_ref, gf_ref, yf_ref),
            (atb_ref, rtb_ref, btb_ref, ktb_ref, vb_ref, gb_ref, yb_ref))
    chains = [(d, p) for d in range(2) for p in range(BR // 128)]
    dirs = [d for d, _ in chains]
    sls = [slice(p * 128, (p + 1) * 128) for _, p in chains]
    each = lambda fn, *lists: [fn(*xs) for xs in zip(*lists)]
    prec = RW_PREC
    AR = [jnp.concatenate([stack(refs[d][0][0, 0, :, sl]), stack(refs[d][1][0, 0, :, sl])], axis=0)
          for d, sl in zip(dirs, sls)]
    BK = [jnp.concatenate([stack(refs[d][2][0, 0, :, sl]), stack(refs[d][3][0, 0, :, sl])], axis=0)
          for d, sl in zip(dirs, sls)]
    V = [stack(refs[d][4][0, :, sl]) for d, sl in zip(dirs, sls)]
    sc = each(lambda a, b: _mmx(a, b, prec["scores"], "nt"), AR, BK)
    Aab = [jnp.where(strict[d], x[:2 * T, :2 * T], 0.0) for d, x in zip(dirs, sc)]
    Aak = [jnp.where(strict[d], x[:2 * T, 2 * T:], 0.0) for d, x in zip(dirs, sc)]
    Mrbk = [jnp.concatenate([jnp.where(incl[d], x[2 * T:, :2 * T], 0.0),
                             jnp.where(incl[d], x[2 * T:, 2 * T:], 0.0)], axis=1) for d, x in zip(dirs, sc)]

    inv = lambda a, b: _mmx(a, b, prec["inv"])
    N = [jnp.where(base_blk, x, 0.0) for x in Aab]
    P = [eye + x for x in N]
    N = each(inv, N, N)
    NP = each(lambda n, q: inv(jnp.concatenate([n, q], axis=0), n), N, P)
    P = each(lambda q, x: q + x[2 * T:], P, NP)
    P = each(lambda q, x: q + inv(q, x[:2 * T]), P, NP)
    for off_blk in merge_blks:
        CP = each(lambda a, q: inv(jnp.where(off_blk, a, 0.0), q), Aab, P)
        P = each(lambda q, x: q + inv(q, x), P, CP)

    AV = each(lambda a, v: _mmx(a, v, prec["av"]), Aak, V)
    S = [s_ref[d, p] for d, p in chains]
    XS = each(lambda a, s_: _mmx(a, s_, prec["xs"], "nt"), AR, S)
    U = each(lambda q, x, w: _mmx(q, x[:2 * T] + w, prec["u"]), P, XS, AV)
    UV = each(lambda u, v: jnp.concatenate([u.astype(BF16), v], axis=0), U, V)
    Ys = each(lambda x, m, uv: x[2 * T:] + _mmx(m, uv, prec["y"]), XS, Mrbk, UV)
    dS = each(lambda uv, bk: _mmx(uv, bk, prec["state"], "tn"), UV, BK)
    for c, (d, p) in enumerate(chains):
        refs[d][6][0, :, sls[c]] = Ys[c][:T] + Ys[c][T:]
        s_ref[d, p] = (S[c] + dS[c]) * refs[d][5][0, 0, 0, :, sls[c]]


def _rw_scan(at, rt, bt, kt, v, g, n_ctx):
    bsz, _, L, _ = at.shape
    nc, nc_ctx = L // RW_T, n_ctx // RW_T

    def specs(d):
        ch = lambda j: _scan_chunk(d, j, nc_ctx, nc)
        dspec = pl.BlockSpec((1, 1, RW_T, BR), lambda b, j: (b, d, ch(j), 0))
        tspec = pl.BlockSpec((1, RW_T, BR), lambda b, j: (b, ch(j), 0))
        return tspec, [dspec, dspec, dspec, dspec, tspec,
                       pl.BlockSpec((1, 1, 1, 1, BR), lambda b, j: (b, d, ch(j), 0, 0))]

    (out_f, in_f), (out_b, in_b) = specs(0), specs(1)
    shape = jax.ShapeDtypeStruct((bsz, L, BR), F32)
    return pl.pallas_call(
        _rwscan_kernel,
        grid=(bsz, nc),
        in_specs=in_f + in_b,
        out_specs=[out_f, out_b],
        out_shape=[shape, shape],
        scratch_shapes=[pltpu.VMEM((2, BR // 128, 128, 128), F32)],
        compiler_params=_cparams(("parallel", "arbitrary")),
        name="rwkv_scan",
    )(at, rt, bt, kt, v, g, at, rt, bt, kt, v, g)


def _atprep_kernel(q_ref, k_ref, v_ref, cos_ref, sin_ref, qg_ref, kg_ref, qo_ref, ko_ref, vo_ref):
    cos = cos_ref[...]
    sin = sin_ref[...]
    lane = lax.broadcasted_iota(jnp.int32, cos.shape, 1)
    first_half = (lane % 64) < 32

    def norm_rope(x, g):
        y = x * lax.rsqrt(jnp.mean(x * x, axis=-1, keepdims=True) + EPS) * g
        partner = jnp.where(first_half, pltpu.roll(y, 96, axis=1), pltpu.roll(y, 32, axis=1))
        return y * cos + partner * sin

    for h in range(AT_H):
        sl = slice(h * AT_HD, (h + 1) * AT_HD)
        q_scale = AT_HD ** -0.5 * math.log2(math.e)
        qo_ref[0, :, sl] = (norm_rope(q_ref[0, :, sl], qg_ref[...]) * q_scale).astype(qo_ref.dtype)
    for h in range(AT_KV):
        sl = slice(h * AT_HD, (h + 1) * AT_HD)
        ko_ref[0, :, sl] = norm_rope(k_ref[0, :, sl], kg_ref[...]).astype(ko_ref.dtype)
    vo_ref[0] = v_ref[0].astype(vo_ref.dtype)


def _at_prepare(proj, cos, sin, q_g, k_g, tl=256):
    bsz, L, _ = proj.shape
    kvw = AT_KV * AT_HD
    tok = lambda w: pl.BlockSpec((1, tl, w), lambda b, i: (b, i, 0))
    return pl.pallas_call(
        _atprep_kernel,
        grid=(bsz, L // tl),
        in_specs=[pl.BlockSpec((1, tl, BR), lambda b, i: (b, i, COL["at_q"] // BR)),
                  pl.BlockSpec((1, tl, kvw), lambda b, i: (b, i, COL["at_k"] // kvw)),
                  pl.BlockSpec((1, tl, kvw), lambda b, i: (b, i, COL["at_v"] // kvw)),
                  pl.BlockSpec((tl, AT_HD), lambda b, i: (i, 0)),
                  pl.BlockSpec((tl, AT_HD), lambda b, i: (i, 0)),
                  pl.BlockSpec((1, AT_HD), lambda b, i: (0, 0)),
                  pl.BlockSpec((1, AT_HD), lambda b, i: (0, 0))],
        out_specs=[tok(BR), tok(kvw), tok(kvw)],
        out_shape=[jax.ShapeDtypeStruct((bsz, L, BR), BF16),
                   jax.ShapeDtypeStruct((bsz, L, kvw), BF16),
                   jax.ShapeDtypeStruct((bsz, L, kvw), BF16)],
        compiler_params=_cparams(("parallel", "parallel")),
        name="gqa_prepare",
    )(proj, proj, proj, cos, sin, q_g.reshape(1, AT_HD), k_g.reshape(1, AT_HD))


def _attn_kernel(q_ref, k_ref, v_ref, o_ref, s_ref, p_ref, l_ref, *, tq, n_ctx, kb, rows):
    i = pl.program_id(2)
    grp = AT_H // AT_KV
    def attend(n_keys):
        chunks = [slice(c * kb, (c + 1) * kb) for c in range(n_keys // kb)]

        def scores(r):
            s_ref[r, :, :n_keys] = _dot_nt(q_ref[0, :, r * AT_HD:(r + 1) * AT_HD], k_ref[0, :n_keys, :])

        def softmax(r):
            for rb in range(tq // rows):
                rs = slice(rb * rows, (rb + 1) * rows)
                mx = s_ref[r, rs, chunks[0]]
                for ch in chunks[1:]:
                    mx = jnp.maximum(mx, s_ref[r, rs, ch])
                m = jnp.broadcast_to(jnp.max(mx, axis=-1, keepdims=True), mx.shape)
                tot = jnp.zeros_like(mx)
                for ch in chunks:
                    p = jnp.exp2(s_ref[r, rs, ch] - m)
                    tot = tot + p
                    p_ref[r, rs, ch] = p.astype(BF16)
                l_ref[r, rs, :] = jnp.broadcast_to(jnp.sum(tot, axis=-1, keepdims=True), (rows, AT_HD))

        def values(r):
            o_ref[0, :, r * AT_HD:(r + 1) * AT_HD] = _dot(p_ref[r, :, :n_keys], v_ref[0, :n_keys, :]) / l_ref[r]

        stages = (scores, softmax, values)
        for t in range(grp + len(stages) - 1):
            for st, fn in enumerate(stages):
                if 0 <= t - st < grp:
                    fn(t - st)

    @pl.when(i * tq < n_ctx)
    def _():
        attend(n_ctx)

    @pl.when(i * tq >= n_ctx)
    def _():
        attend(k_ref.shape[1])


def _attention(q, k, v, n_ctx, tq=256, kb=256, rows=32):
    bsz, L, _ = q.shape
    gw = (AT_H // AT_KV) * AT_HD
    return pl.pallas_call(
        functools.partial(_attn_kernel, tq=tq, n_ctx=n_ctx, kb=kb, rows=rows),
        scratch_shapes=[pltpu.VMEM((gw // AT_HD, tq, L), F32), pltpu.VMEM((gw // AT_HD, tq, L), BF16),
                        pltpu.VMEM((gw // AT_HD, tq, AT_HD), F32)],
        grid=(bsz, AT_KV, L // tq),
        in_specs=[pl.BlockSpec((1, tq, gw), lambda b, g, i: (b, i, g)),
                  pl.BlockSpec((1, L, AT_HD), lambda b, g, i: (b, 0, g)),
                  pl.BlockSpec((1, L, AT_HD), lambda b, g, i: (b, 0, g))],
        out_specs=pl.BlockSpec((1, tq, gw), lambda b, g, i: (b, i, g)),
        out_shape=jax.ShapeDtypeStruct((bsz, L, BR), F32),
        compiler_params=_cparams(("parallel", "parallel", "parallel")),
        name="gqa_attention",
    )(q, k, v)


def _cap_gates(pre):
    return GATE_CAP * jnp.tanh(pre / GATE_CAP)


def _log_sigmoid(x):
    return jnp.minimum(x, 0.0) - jnp.log1p(jnp.exp(-jnp.abs(x)))


def _mlstm_kernel(qf_ref, kf_ref, vf_ref, gcf_ref, qb_ref, kb_ref, vb_ref, gcb_ref,
                  bc_ref, br_ref, hf_ref, hb_ref, c_ref, n_ref, m_ref):
    j = pl.program_id(1)
    T = ML_T

    @pl.when(j == 0)
    def _():
        c_ref[...] = jnp.zeros_like(c_ref)
        n_ref[...] = jnp.zeros_like(n_ref)
        m_ref[...] = jnp.zeros_like(m_ref)

    r2 = lax.broadcasted_iota(jnp.int32, (T, T), 0)
    c2 = lax.broadcasted_iota(jnp.int32, (T, T), 1)
    lane16 = lax.broadcasted_iota(jnp.int32, (T, 16), 1)
    sub16 = lax.broadcasted_iota(jnp.int32, (16, T), 0)

    refs = ((qf_ref, kf_ref, vf_ref, gcf_ref, None, hf_ref), (qb_ref, kb_ref, vb_ref, gcb_ref, None, hb_ref))
    seen = [c2 <= r2, c2 >= r2]
    seen_t = [r2 <= c2, r2 >= c2]
    gc = [_cap_gates(refs[d][3][0][:, :16] + bc_ref[...]) for d in range(2)]
    gr = [_cap_gates(refs[d][3][0].T[:16, :] + br_ref[...]) for d in range(2)]
    lsc = [_log_sigmoid(x) for x in gc]
    lsr = [_log_sigmoid(x) for x in gr]

    chains = [(d, h) for d in range(2) for h in range(ML_H)]
    each = lambda fn, *lists: [fn(*xs) for xs in zip(*lists)]
    pick_row = lambda x, idx: jnp.sum(jnp.where(sub16 == idx, x, 0.0), axis=0, keepdims=True)
    pick_col = lambda x, idx: jnp.sum(jnp.where(lane16 == idx, x, 0.0), axis=1, keepdims=True)
    li_row = [pick_row(gr[d], d * ML_H + h) for d, h in chains]
    lf_row = [pick_row(lsr[d], (2 + d) * ML_H + h) for d, h in chains]
    li_col = [pick_col(gc[d], d * ML_H + h) for d, h in chains]
    lf_col = [pick_col(lsc[d], (2 + d) * ML_H + h) for d, h in chains]
    b_col = [jnp.sum(jnp.where(seen[d], x, 0.0), axis=1, keepdims=True) for (d, _), x in zip(chains, lf_row)]
    b_row = [jnp.sum(jnp.where(seen_t[d], x, 0.0), axis=0, keepdims=True) for (d, _), x in zip(chains, lf_col)]
    g = [jnp.sum(x, axis=0, keepdims=True) for x in lf_col]
    m_prev = [m_ref[d, h] for d, h in chains]
    q = [refs[d][0][0, :, h * ML_DK:(h + 1) * ML_DK] * (ML_DK ** -0.5) for d, h in chains]
    k = [refs[d][1][0, :, h * ML_DK:(h + 1) * ML_DK] for d, h in chains]
    vb = [refs[d][2][0, :, h * ML_DV:(h + 1) * ML_DV].astype(BF16) for d, h in chains]
    C = [c_ref[d, h] for d, h in chains]
    n = [n_ref[d, h] for d, h in chains]
    qb = [x.astype(BF16) for x in q]

    dmat = [jnp.where(seen[d], bc - br + li, -jnp.inf) for (d, _), bc, br, li in zip(chains, b_col, b_row, li_row)]
    m_inter = each(lambda bc, m: bc + m, b_col, m_prev)
    m_t = each(lambda mi, dm: jnp.maximum(mi, jnp.max(dm, axis=-1, keepdims=True)), m_inter, dmat)
    w_inter = each(lambda mi, mt: jnp.exp(mi - mt), m_inter, m_t)
    qk = each(lambda a, b: _dot_nt(a, b.astype(BF16)), qb, k)
    qc = each(lambda a, b: _dot(a, b.astype(BF16)), qb, C)
    s = each(lambda x, dm, mt: x * jnp.exp(dm - mt), qk, dmat, m_t)
    sv = each(lambda a, b: _dot(a.astype(BF16), b), s, vb)
    qn = each(lambda a, b: jnp.sum(a * b, axis=-1, keepdims=True), q, n)
    den = each(lambda w, a, x: w * a + jnp.sum(x, axis=-1, keepdims=True), w_inter, qn, s)
    for (d, h), w, a, b, dn, mt in zip(chains, w_inter, qc, sv, den, m_t):
        refs[d][5][0, :, h * ML_DV:(h + 1) * ML_DV] = (w * a + b) / jnp.maximum(jnp.abs(dn), jnp.exp(-mt))

    loga = each(lambda g_, bc, li: g_ - bc + li, g, b_col, li_col)
    m_new = each(lambda g_, m, la: jnp.maximum(g_ + m, jnp.max(la, axis=0, keepdims=True)), g, m_prev, loga)
    carry = each(lambda g_, m, mn: jnp.exp(g_ + m - mn), g, m_prev, m_new)
    wk = each(lambda la, mn, k_: jnp.exp(la - mn) * k_, loga, m_new, k)
    kv = each(lambda a, b: _dot_tn(a.astype(BF16), b), wk, vb)
    for (d, h), cr, c_, kv_, n_, wk_, mn in zip(chains, carry, C, kv, n, wk, m_new):
        c_ref[d, h] = cr * c_ + kv_
        n_ref[d, h] = cr * n_ + jnp.sum(wk_, axis=0, keepdims=True)
        m_ref[d, h] = mn


def _mlstm(proj, bias_col, bias_row, n_ctx):
    bsz, L, _ = proj.shape
    nc, nc_ctx = L // ML_T, n_ctx // ML_T
    qw, vw = ML_H * ML_DK, ML_H * ML_DV

    def dir_specs(d):
        ch = lambda j: _scan_chunk(d, j, nc_ctx, nc)
        return [pl.BlockSpec((1, ML_T, qw), lambda b, j: (b, ch(j), COL["ml_q"] // qw)),
                pl.BlockSpec((1, ML_T, qw), lambda b, j: (b, ch(j), COL["ml_k"] // qw)),
                pl.BlockSpec((1, ML_T, vw), lambda b, j: (b, ch(j), COL["ml_v"] // vw)),
                pl.BlockSpec((1, ML_T, 128), lambda b, j: (b, ch(j), COL["ml_if"] // 128))]

    def out_spec(d):
        ch = lambda j: _scan_chunk(d, j, nc_ctx, nc)
        return pl.BlockSpec((1, ML_T, vw), lambda b, j: (b, ch(j), 0))

    shape = jax.ShapeDtypeStruct((bsz, L, vw), F32)
    return pl.pallas_call(
        _mlstm_kernel,
        grid=(bsz, nc),
        in_specs=dir_specs(0) + dir_specs(1) + [pl.BlockSpec((1, 16), lambda b, j: (0, 0)),
                                                pl.BlockSpec((16, 1), lambda b, j: (0, 0))],
        out_specs=[out_spec(0), out_spec(1)],
        out_shape=[shape, shape],
        scratch_shapes=[pltpu.VMEM((2, ML_H, ML_DK, ML_DV), F32),
                        pltpu.VMEM((2, ML_H, 1, ML_DK), F32),
                        pltpu.VMEM((2, ML_H, 1, 1), F32)],
        compiler_params=_cparams(("parallel", "arbitrary")),
        name="mlstm_scan",
    )(proj, proj, proj, proj, proj, proj, proj, proj, bias_col, bias_row)


def _epilogue_kernel(yf_ref, yb_ref, bonus_ref, rwg_ref, att_ref, atg_ref, hf_ref, hb_ref, mlo_ref, mlg_ref,
                     lnw_ref, lnb_ref, mng_ref, m64_ref, b64_ref, m256_ref, b256_ref, o_ref):
    y = yf_ref[0] + yb_ref[0]
    mu = _seg_reduce(y, m64_ref[...], b64_ref[...])
    yc = y - mu
    var = _seg_reduce(yc * yc, m64_ref[...], b64_ref[...])
    ya = yc * lax.rsqrt(var + RW_GN_EPS) * lnw_ref[...] + lnb_ref[...] + bonus_ref[0]
    o_ref[0, 0] = (ya * _silu(rwg_ref[0])).astype(o_ref.dtype)

    o_ref[1, 0] = (att_ref[0] * _silu(atg_ref[0])).astype(o_ref.dtype)

    hh = hf_ref[0] + hb_ref[0]
    ms = _seg_reduce(hh * hh, m256_ref[...], b256_ref[...])
    hn = hh * lax.rsqrt(ms + EPS) * mng_ref[...]
    o_ref[2, 0] = (jax.nn.sigmoid(mlo_ref[0]) * hn * _silu(mlg_ref[0])).astype(o_ref.dtype)


def _epilogue(proj, y_f, y_b, bonus, att, h_f, h_b, p, tl=256):
    bsz, L, _ = proj.shape
    tok = pl.BlockSpec((1, tl, BR), lambda b, i: (b, i, 0))
    pc = lambda name: pl.BlockSpec((1, tl, BR), lambda b, i: (b, i, COL[name] // BR))
    vec = pl.BlockSpec((1, BR), lambda b, i: (0, 0))
    gat = pl.BlockSpec((BR, 128), lambda b, i: (0, 0))
    sct = pl.BlockSpec((128, BR), lambda b, i: (0, 0))
    return pl.pallas_call(
        _epilogue_kernel,
        grid=(bsz, L // tl),
        in_specs=[tok, tok, tok, pc("rw_g"), tok, pc("at_g"), tok, tok, pc("ml_o"), pc("ml_g"),
                  vec, vec, vec, gat, sct, gat, sct],
        out_specs=pl.BlockSpec((3, 1, tl, BR), lambda b, i: (0, b, i, 0)),
        out_shape=jax.ShapeDtypeStruct((3, bsz, L, BR), BF16),
        compiler_params=_cparams(("parallel", "parallel")),
        name="branch_epilogue",
    )(y_f, y_b, bonus, proj, att, proj, h_f, h_b, proj, proj,
      p["rw_ln_w"].reshape(1, BR), p["rw_ln_b"].reshape(1, BR), p["ml_norm_g"].reshape(1, BR),
      p["mean64"], p["bcast64"], p["mean256"], p["bcast256"])


def _merge_kernel(y_ref, w_ref, g0_ref, g1_ref, g2_ref, o_ref, wb_ref):
    @pl.when(pl.program_id(1) == 0)
    def _():
        wb_ref[...] = w_ref[0].astype(BF16)

    acc = jax.nn.sigmoid(g0_ref[...]) * _dot(y_ref[0], wb_ref[0])
    acc += jax.nn.sigmoid(g1_ref[...]) * _dot(y_ref[1], wb_ref[1])
    acc += jax.nn.sigmoid(g2_ref[...]) * _dot(y_ref[2], wb_ref[2])
    o_ref[...] = acc.astype(o_ref.dtype)


def _merge(ys, w_branch, layer, proj2d, tm=512, tn=1024):
    _, m, _ = ys.shape
    nb = D // tn
    gate = lambda n: pl.BlockSpec((tm, tn), lambda j, i: (i, n * nb + j))
    return pl.pallas_call(
        _merge_kernel,
        grid=(D // tn, m // tm),
        in_specs=[pl.BlockSpec((3, tm, BR), lambda j, i: (0, i, 0)),
                  pl.BlockSpec((1, 3, BR, tn), lambda j, i: (layer, 0, 0, j), pipeline_mode=pl.Buffered(1)),
                  gate(0), gate(1), gate(2)],
        out_specs=pl.BlockSpec((tm, tn), lambda j, i: (i, j)),
        out_shape=jax.ShapeDtypeStruct((m, D), BF16),
        scratch_shapes=[pltpu.VMEM((3, BR, tn), BF16)],
        compiler_params=_cparams(("arbitrary", "arbitrary")),
        name="branch_merge",
    )(ys, w_branch, proj2d, proj2d, proj2d)


def _outproj_kernel(*refs, nct):
    a_ref, w_ref, gt_ref, *z_refs, o_ref, wb_ref = refs

    @pl.when(jnp.logical_and(pl.program_id(1) == 0, pl.program_id(2) == 0))
    def _():
        wb_ref[...] = w_ref[0].astype(BF16)

    def body(z):
        o_ref[0] = z + gt_ref[0, 0] * _dot(a_ref[0], wb_ref[...])

    _for_stream_tile(z_refs, pl.program_id(2) < nct, body)


def _out_projection(mixed, w_out, layer, z, gate, n_ctx, tl=256, tn=1024):
    bsz, L, _ = mixed.shape
    nct = n_ctx // tl
    z_specs, z_args = _stream_specs(z, tl, nct, (1, tl, tn), lambda j, b, i: (b, i, j))
    return pl.pallas_call(
        functools.partial(_outproj_kernel, nct=nct),
        grid=(D // tn, bsz, L // tl),
        in_specs=[pl.BlockSpec((1, tl, D), lambda j, b, i: (b, i, 0)),
                  pl.BlockSpec((1, D, tn), lambda j, b, i: (layer, 0, j)),
                  pl.BlockSpec((1, 1, 1, tn), lambda j, b, i: (b, (i >= nct).astype(jnp.int32), 0, j))] + z_specs,
        out_specs=pl.BlockSpec((1, tl, tn), lambda j, b, i: (b, i, j)),
        out_shape=jax.ShapeDtypeStruct((bsz, L, D), F32),
        scratch_shapes=[pltpu.VMEM((D, tn), BF16)],
        compiler_params=_cparams(("arbitrary", "arbitrary", "arbitrary")),
        name="out_projection",
    )(mixed, w_out, gate, *z_args)


def _rope_tables(n_ctx, n_lat):
    rows = n_lat // GRID_W
    row = jnp.repeat(jnp.arange(rows), GRID_W).astype(F32)
    col = jnp.tile(jnp.arange(GRID_W), rows).astype(F32)
    inv_freq = ROPE_THETA ** (-jnp.arange(0, AT_HD // 2, 2, dtype=F32) / (AT_HD // 2))
    ang_lat = jnp.stack([row[:, None] * inv_freq, col[:, None] * inv_freq], axis=1)
    ang = jnp.concatenate([jnp.zeros((n_ctx, 2, AT_HD // 4), F32), ang_lat], axis=0)
    cos, sin = jnp.cos(ang), jnp.sin(ang)
    cos_t = jnp.concatenate([cos[:, 0], cos[:, 0], cos[:, 1], cos[:, 1]], axis=-1)
    sin_t = jnp.concatenate([-sin[:, 0], sin[:, 0], -sin[:, 1], sin[:, 1]], axis=-1)
    return cos_t, sin_t


def _group_consts(width, value):
    member = (np.arange(BR)[:, None] // width) == np.arange(128)[None, :]
    return (jnp.asarray(np.where(member, value, 0.0), dtype=BF16),
            jnp.asarray(np.where(member.T, 1.0, 0.0), dtype=BF16))


def _chunk_tri(tl):
    t = np.arange(tl)
    same = (t[:, None] // RW_T) == (t[None, :] // RW_T)
    fwd = same & (t[None, :] <= t[:, None])
    bwd = same & (t[None, :] >= t[:, None])
    return jnp.asarray(np.stack([fwd, bwd]).astype(np.float32), dtype=BF16)


def kernel(x, c, ctx, c_ctx, norm_g, w_ada, b_ada, w_in, shift_mu, rw_w_up, rw_w0, rw_a_up, rw_a0, rw_k_k, rw_k_a, rw_r_k, rw_ln_w, rw_ln_b, at_q_g, at_k_g, ml_gate_b, ml_norm_g, w_branch, w_out, final_g):
    bsz, n_lat, _ = x.shape
    n_ctx = ctx.shape[1]
    L = n_ctx + n_lat
    depth = w_in.shape[0]

    cos_t, sin_t = _rope_tables(n_ctx, n_lat)
    sum64, bcast64 = _group_consts(RW_N, 1.0)
    mean64, _ = _group_consts(RW_N, 1.0 / RW_N)
    mean256, bcast256 = _group_consts(ML_DV, 1.0 / ML_DV)
    consts = {"sum64": sum64, "bcast64": bcast64, "mean64": mean64, "mean256": mean256, "bcast256": bcast256,
              "tri": _chunk_tri(256)}

    cc = jnp.concatenate([c, c_ctx[None], jnp.zeros((8 - bsz - 1, D), F32)], axis=0)
    mod = _modulation(cc, w_ada, b_ada)

    w_in_t = jnp.swapaxes(w_in, 1, 2)
    z = (ctx, x)
    for l in range(depth):
        sh, sc, gt = mod[l, :, :D], mod[l, :, D:2 * D], mod[l, :, 2 * D:]
        pick = lambda t: jnp.stack([jnp.broadcast_to(t[bsz], (bsz, D)), t[:bsz]], axis=1)[:, :, None, :]
        scale = pick((1.0 + sc) * norm_g[l])
        shift = pick(sh)
        gate = pick(gt)

        h = _norm_mod(z, scale, shift, n_ctx, L, BF16)
        proj2d = _in_projection(h.reshape(bsz * L, D), w_in_t, l)
        proj = proj2d.reshape(bsz, L, N_PROJ)

        p = dict(consts, shift_mu=shift_mu[l], rw_w_up=rw_w_up[l], rw_w0=rw_w0[l], rw_a_up=rw_a_up[l],
                 rw_a0=rw_a0[l], rw_k_k=rw_k_k[l], rw_k_a=rw_k_a[l], rw_r_k=rw_r_k[l], rw_ln_w=rw_ln_w[l],
                 rw_ln_b=rw_ln_b[l], ml_norm_g=ml_norm_g[l])
        at, rt, bt, kt, v_rw, bonus, g_rw = _rw_prepare(proj, p, n_ctx)
        y_f, y_b = _rw_scan(at, rt, bt, kt, v_rw, g_rw, n_ctx)

        qn, kn, vn = _at_prepare(proj, cos_t, sin_t, at_q_g[l], at_k_g[l])
        att = _attention(qn, kn, vn, n_ctx)

        bias = ml_gate_b[l].reshape(16)
        h_f, h_b = _mlstm(proj, bias.reshape(1, 16), bias.reshape(16, 1), n_ctx)

        ys = _epilogue(proj, y_f, y_b, bonus, att, h_f, h_b, p)
        mixed = _merge(ys.reshape(3, bsz * L, BR), w_branch, l, proj2d)
        z = _out_projection(mixed.reshape(bsz, L, D), w_out, l, z, gate, n_ctx)

    ones = jnp.ones((bsz, 2, 1, D), F32) * final_g
    zeros = jnp.zeros((bsz, 2, 1, D), F32)
    return _norm_mod(z, ones, zeros, n_ctx, L, F32, latent_only=True)
```

```python
import functools
import math

import numpy as np
import jax
import jax.numpy as jnp
from jax import lax
from jax.experimental import pallas as pl
from jax.experimental.pallas import tpu as pltpu

F32 = jnp.float32
BF16 = jnp.bfloat16
HI = lax.Precision.HIGHEST

D = 2048
BR = 1024
EPS = 1e-6
GRID_W = 64

RW_H, RW_N, RW_LORA = 16, 64, 64
RW_GN_EPS = 64e-5
RW_T = 64
RW_INV_BASE = 8

AT_H, AT_KV, AT_HD = 8, 2, 128
ROPE_THETA = 10000.0

ML_H, ML_DK, ML_DV, ML_T = 4, 128, 256, 128
GATE_CAP = 15.0

D_IN = 17168
LANES, SUBLANES = 128, 8
PROJ_TN = 512
_PROJ_GROUPS = (
    ("merge", 11024, 6144), ("r", 0, 1024), ("k", 1024, 1024), ("v", 2048, 1024),
    ("rw_g", 3328, 1024), ("at_q", 4352, 1024), ("at_g", 5888, 1024), ("ml_v", 7936, 1024),
    ("ml_o", 8960, 1024), ("ml_g", 10000, 1024), ("ml_q", 6912, 512), ("ml_k", 7424, 512),
    ("at_k", 5376, 512), ("wdad", 3072, 512), ("ml_if", 9984, 512),
)
COL = {}
PROJ_SRC = []
for _name, _start, _width in _PROJ_GROUPS:
    COL[_name] = len(PROJ_SRC) * PROJ_TN
    PROJ_SRC += [_start + t * PROJ_TN for t in range(_width // PROJ_TN)]
COL["at_v"] = COL["at_k"] + 256
N_PROJ = len(PROJ_SRC) * PROJ_TN
for _name, _blk in (("merge", 2048), ("r", 1024), ("k", 1024), ("v", 1024), ("rw_g", 1024), ("at_q", 1024),
                    ("at_g", 1024), ("ml_v", 1024), ("ml_o", 1024), ("ml_g", 1024), ("ml_q", 512), ("ml_k", 512),
                    ("at_k", 256), ("at_v", 256), ("wdad", 256), ("ml_if", 128)):
    assert COL[_name] % _blk == 0, _name

VMEM_LIMIT = 48 * 1024 * 1024


def _cparams(sem):
    return pltpu.CompilerParams(dimension_semantics=sem, vmem_limit_bytes=VMEM_LIMIT)


def _dot(a, b, **kw):
    return jnp.dot(a, b, preferred_element_type=F32, **kw)


def _dot_nt(a, b, **kw):
    return lax.dot_general(a, b, (((1,), (1,)), ((), ())), preferred_element_type=F32, **kw)


def _dot_tn(a, b, **kw):
    return lax.dot_general(a, b, (((0,), (0,)), ((), ())), preferred_element_type=F32, **kw)


def _split3(x):
    h = x.astype(BF16)
    r = x - h.astype(F32)
    m = r.astype(BF16)
    l = (r - m.astype(F32)).astype(BF16)
    return h, m, l


def _split2(x):
    h = x.astype(BF16)
    return h, (x - h.astype(F32)).astype(BF16)


def _seg_reduce(x, gather, scatter):
    h, l = _split2(x)
    sh, sl = _split2(_dot(h, gather) + _dot(l, gather))
    return _dot(sh, scatter) + _dot(sl, scatter)


def _dot_const_lhs(c, x):
    h, m, l = _split3(x)
    return _dot(c, h) + _dot(c, m) + _dot(c, l)


def _silu(x):
    return x * jax.nn.sigmoid(x)


def _mod_kernel(c_ref, w_ref, b_ref, o_ref):
    @pl.when(pl.program_id(1) == 0)
    def _():
        o_ref[0] = jnp.broadcast_to(b_ref[0], o_ref.shape[1:])

    xh, xl = _split2(_silu(c_ref[...]))
    wh, wl = _split2(w_ref[0])
    o_ref[0] += _dot(xh, wh) + _dot(xh, wl) + _dot(xl, wh)


def _modulation(cc, w_ada, b_ada, tk=256):
    depth = w_ada.shape[0]
    return pl.pallas_call(
        _mod_kernel,
        grid=(depth, D // tk),
        in_specs=[pl.BlockSpec((8, tk), lambda l, k: (0, k)),
                  pl.BlockSpec((1, tk, 3 * D), lambda l, k: (l, k, 0)),
                  pl.BlockSpec((1, 1, 3 * D), lambda l, k: (l, 0, 0))],
        out_specs=pl.BlockSpec((1, 8, 3 * D), lambda l, k: (l, 0, 0)),
        out_shape=jax.ShapeDtypeStruct((depth, 8, 3 * D), F32),
        compiler_params=_cparams(("parallel", "arbitrary")),
        name="adaln_modulation",
    )(cc, w_ada, b_ada.reshape(depth, 1, 3 * D))


def _stream_specs(z, tl, nct, block, index):
    if not isinstance(z, tuple):
        return [pl.BlockSpec(block, lambda *g: index(*g))], [z]

    def ctx_index(*g):
        b, i, j = index(*g)
        return b, jnp.minimum(i, nct - 1), j

    def lat_index(*g):
        b, i, j = index(*g)
        return b, jnp.maximum(i - nct, 0), j

    return [pl.BlockSpec(block, ctx_index), pl.BlockSpec(block, lat_index)], list(z)


def _for_stream_tile(z_refs, is_ctx, body):
    if len(z_refs) == 1:
        body(z_refs[0][0])
        return
    pl.when(is_ctx)(lambda: body(z_refs[0][0]))
    pl.when(jnp.logical_not(is_ctx))(lambda: body(z_refs[1][0]))


def _norm_kernel(*refs, nct, first):
    *z_refs, sc_ref, sh_ref, o_ref = refs

    def body(x):
        y = x * lax.rsqrt(jnp.mean(x * x, axis=-1, keepdims=True) + EPS)
        o_ref[0] = (y * sc_ref[0, 0] + sh_ref[0, 0]).astype(o_ref.dtype)

    _for_stream_tile(z_refs, pl.program_id(1) + first < nct, body)


def _norm_mod(z, scale, shift, n_ctx, L, out_dtype, latent_only=False, tl=256):
    bsz = scale.shape[0]
    nct = n_ctx // tl
    first = nct if latent_only else 0
    seg = lambda b, i: (b, (i + first >= nct).astype(jnp.int32), 0, 0)
    z_specs, z_args = _stream_specs(z, tl, nct, (1, tl, D), lambda b, i: (b, i + first, 0))
    return pl.pallas_call(
        functools.partial(_norm_kernel, nct=nct, first=first),
        grid=(bsz, L // tl - first),
        in_specs=z_specs + [pl.BlockSpec((1, 1, 1, D), seg), pl.BlockSpec((1, 1, 1, D), seg)],
        out_specs=pl.BlockSpec((1, tl, D), lambda b, i: (b, i, 0)),
        out_shape=jax.ShapeDtypeStruct((bsz, L - first * tl, D), out_dtype),
        compiler_params=_cparams(("parallel", "parallel")),
        name="rmsnorm_modulate",
    )(*z_args, scale, shift)


def _inproj_kernel(start_ref, a_ref, wt_ref, o_ref):
    o_ref[...] = _dot_nt(a_ref[...], wt_ref[0].astype(BF16))


def _in_projection(a, w_t, layer, tm=2176):
    m, k = a.shape
    tm = math.gcd(m, tm)
    assert all(s % SUBLANES == 0 for s in PROJ_SRC)
    start = np.array(PROJ_SRC, np.int32) // SUBLANES
    grid_spec = pltpu.PrefetchScalarGridSpec(
        num_scalar_prefetch=1,
        grid=(m // tm, len(PROJ_SRC)),
        in_specs=[pl.BlockSpec((tm, k), lambda i, j, st: (i, 0)),
                  pl.BlockSpec((pl.Element(1), pl.Element(PROJ_TN), pl.Element(k)),
                               lambda i, j, st: (layer, st[j] * SUBLANES, 0))],
        out_specs=pl.BlockSpec((tm, PROJ_TN), lambda i, j, st: (i, j)),
    )
    return pl.pallas_call(
        _inproj_kernel,
        grid_spec=grid_spec,
        out_shape=jax.ShapeDtypeStruct((m, N_PROJ), F32),
        compiler_params=_cparams(("parallel", "arbitrary")),
        name="in_projection",
    )(jnp.asarray(start), a, w_t)


def _rwprep_kernel(r_ref, rp_ref, rn_ref, k_ref, kp_ref, kn_ref, v_ref, vp_ref, vn_ref,
                   w_ref, wp_ref, wn_ref, mur_ref, muk_ref, muv_ref, muw_ref,
                   wup_ref, w0_ref, aup_ref, a0_ref, kk_ref, ka_ref, rk_ref, gat_ref, sct_ref, tri_ref,
                   at_ref, rt_ref, bt_ref, kt_ref, vo_ref, bonus_ref, g_ref, *, tl, n_ctx, L):
    i = pl.program_id(1)
    start = i * tl
    has_prev = jnp.logical_and(start != 0, start != n_ctx).astype(F32)
    has_next = jnp.logical_and(start + tl != n_ctx, start + tl != L).astype(F32)

    def shift(x_ref, p_ref, n_ref, mu_ref):
        x = x_ref[0]
        row = lax.broadcasted_iota(jnp.int32, x.shape, 0)
        prev = jnp.where(row == 0, p_ref[0, 7:8, :] * has_prev, pltpu.roll(x, 1, axis=0))
        nxt = jnp.where(row == tl - 1, n_ref[0, 0:1, :] * has_next, pltpu.roll(x, tl - 1, axis=0))
        return x + mu_ref[0:1, :] * (prev - x) + mu_ref[1:2, :] * (nxt - x)

    r = shift(r_ref, rp_ref, rn_ref, mur_ref)
    k = shift(k_ref, kp_ref, kn_ref, muk_ref)
    v = shift(v_ref, vp_ref, vn_ref, muv_ref)
    wdad = shift(w_ref, wp_ref, wn_ref, muw_ref)
    gat, sct = gat_ref[...], sct_ref[...]

    kk = k * kk_ref[...]
    kk = kk / jnp.maximum(jnp.sqrt(_seg_reduce(kk * kk, gat, sct)), 1e-12)
    bonus_ref[0] = _seg_reduce(r * k * rk_ref[...], gat, sct) * v
    vo_ref[0] = v.astype(vo_ref.dtype)

    for d in range(2):
        wd = wdad[:, d * RW_LORA:(d + 1) * RW_LORA]
        ad = wdad[:, 2 * RW_LORA + d * RW_LORA:2 * RW_LORA + (d + 1) * RW_LORA]
        u = w0_ref[d:d + 1, :] + _mmx(jnp.tanh(wd), wup_ref[d], "x3")
        logw = -math.exp(-0.5) * jax.nn.sigmoid(u)
        a = jax.nn.sigmoid(a0_ref[d:d + 1, :] + _mmx(ad, aup_ref[d], "x3"))
        kd = k * (1.0 + (a - 1.0) * ka_ref[...])
        lg = _dot_const_lhs(tri_ref[d], logw)
        e_pos = jnp.exp(lg)
        e_neg = 1.0 / e_pos
        at_ref[0, d] = (-kk * jnp.exp(lg - logw)).astype(at_ref.dtype)
        rt_ref[0, d] = (r * e_pos).astype(rt_ref.dtype)
        bt_ref[0, d] = (kk * a * e_neg).astype(bt_ref.dtype)
        kt_ref[0, d] = (kd * e_neg).astype(kt_ref.dtype)
        for c in range(tl // RW_T):
            g_ref[0, d, c] = jnp.exp(jnp.sum(logw[c * RW_T:(c + 1) * RW_T], axis=0, keepdims=True))


def _rw_prepare(proj, p, n_ctx, tl=256):
    bsz, L, _ = proj.shape
    h8 = tl // 8
    nblk8 = L // 8
    main = lambda w, cb: pl.BlockSpec((1, tl, w), lambda b, i: (b, i, cb))
    prev = lambda w, cb: pl.BlockSpec((1, 8, w), lambda b, i: (b, jnp.maximum(i * h8 - 1, 0), cb))
    nxt = lambda w, cb: pl.BlockSpec((1, 8, w), lambda b, i: (b, jnp.minimum((i + 1) * h8, nblk8 - 1), cb))
    full = lambda shape: pl.BlockSpec(shape, lambda b, i: (0,) * len(shape))
    in_specs = []
    args = []
    for name, w in (("r", 1024), ("k", 1024), ("v", 1024), ("wdad", 256)):
        cb = COL[name] // w
        in_specs += [main(w, cb), prev(w, cb), nxt(w, cb)]
        args += [proj, proj, proj]
    mu = p["shift_mu"]
    in_specs += [pl.BlockSpec((2, 1024), lambda b, i: (0, 0)), pl.BlockSpec((2, 1024), lambda b, i: (0, 1)),
                 pl.BlockSpec((2, 1024), lambda b, i: (0, 2)), pl.BlockSpec((2, 256), lambda b, i: (0, 12))]
    args += [mu, mu, mu, mu]
    in_specs += [full((2, RW_LORA, BR)), full((2, BR)), full((2, RW_LORA, BR)), full((2, BR)),
                 full((1, BR)), full((1, BR)), full((1, BR)), full((BR, 128)), full((128, BR)), full((2, tl, tl))]
    args += [p["rw_w_up"], p["rw_w0"], p["rw_a_up"], p["rw_a0"], p["rw_k_k"].reshape(1, BR),
             p["rw_k_a"].reshape(1, BR), p["rw_r_k"].reshape(1, BR), p["sum64"], p["bcast64"], p["tri"]]
    dir_spec = pl.BlockSpec((1, 2, tl, BR), lambda b, i: (b, 0, i, 0))
    tok_spec = pl.BlockSpec((1, tl, BR), lambda b, i: (b, i, 0))
    nchunk = tl // RW_T
    out_specs = [dir_spec, dir_spec, dir_spec, dir_spec, tok_spec, tok_spec,
                 pl.BlockSpec((1, 2, nchunk, 1, BR), lambda b, i: (b, 0, i, 0, 0))]
    dir_shape = jax.ShapeDtypeStruct((bsz, 2, L, BR), BF16)
    out_shape = [dir_shape, dir_shape, dir_shape, dir_shape,
                 jax.ShapeDtypeStruct((bsz, L, BR), BF16), jax.ShapeDtypeStruct((bsz, L, BR), F32),
                 jax.ShapeDtypeStruct((bsz, 2, L // RW_T, 1, BR), F32)]
    return pl.pallas_call(
        functools.partial(_rwprep_kernel, tl=tl, n_ctx=n_ctx, L=L),
        grid=(bsz, L // tl),
        in_specs=in_specs, out_specs=out_specs, out_shape=out_shape,
        compiler_params=_cparams(("parallel", "parallel")),
        name="rwkv_prepare",
    )(*args)


def _scan_chunk(d, j, nc_ctx, nc):
    bwd = jnp.where(j < nc_ctx, nc_ctx - 1 - j, nc - 1 - j + nc_ctx)
    return jnp.where(d == 0, j, bwd)


def _mmx(a, b, mode, dims="nn"):
    f = {"nn": _dot, "nt": _dot_nt, "tn": _dot_tn}[dims]
    ah, bh = a.astype(BF16), b.astype(BF16)
    if mode == "bf16":
        return f(ah, bh)
    al = (a - ah.astype(F32)).astype(BF16)
    bl = (b - bh.astype(F32)).astype(BF16)
    return f(ah, bh) + f(ah, bl) + f(al, bh)


def _rwscan_kernel(atf_ref, rtf_ref, btf_ref, ktf_ref, vf_ref, gf_ref,
                   atb_ref, rtb_ref, btb_ref, ktb_ref, vb_ref, gb_ref, yf_ref, yb_ref, s_ref):
    j = pl.program_id(1)
    T = RW_T

    @pl.when(j == 0)
    def _():
        s_ref[...] = jnp.zeros_like(s_ref)

    tok = lax.broadcasted_iota(jnp.int32, (T, 128), 0)
    col = lax.broadcasted_iota(jnp.int32, (T, 128), 1) % RW_N
    strict = [col < tok, col > tok]
    incl = [col <= tok, col >= tok]
    eye = jnp.where(col == tok, 1.0, 0.0)
    base_blk = (tok // RW_INV_BASE) == (col // RW_INV_BASE)
    merge_blks = []
    s = RW_INV_BASE
    while s < T:
        merge_blks.append(jnp.logical_and((tok // (2 * s)) == (col // (2 * s)), (tok // s) != (col // s)))
        s *= 2
    row2 = lax.broadcasted_iota(jnp.int32, (2 * T, 128), 0)
    lane2 = lax.broadcasted_iota(jnp.int32, (2 * T, 128), 1)
    same_head = (lane2 // RW_N) == (row2 // T)

    def stack(x):
        x = x.astype(BF16)
        return jnp.where(same_head, jnp.concatenate([x, x], axis=0), 0.0)

    cat = lambda xs, axis=0: jnp.concatenate(xs, axis=axis)
    mm = lambda a, b: _dot(a.astype(BF16), b)

    refs = ((atf_ref, rtf_ref, btf_ref, ktf_ref, vf_ref, gf_ref, yf_ref),
            (atb_ref, rtb_ref, btb_ref, ktb_ref, vb_ref, gb_ref, yb_ref))
    chains = [(d, p) for d in range(2) for p in range(BR // 128)]
    dirs = [d for d, _ in chains]
    sls = [slice(p * 128, (p + 1) * 128) for _, p in chains]
    each = lambda fn, *lists: [fn(*xs) for xs in zip(*lists)]
    At = [refs[d][0][0, 0, :, sl] for d, sl in zip(dirs, sls)]
    Rt = [refs[d][1][0, 0, :, sl] for d, sl in zip(dirs, sls)]
    Bt = [refs[d][2][0, 0, :, sl] for d, sl in zip(dirs, sls)]
    Kt = [refs[d][3][0, 0, :, sl] for d, sl in zip(dirs, sls)]
    V = [refs[d][4][0, :, sl] for d, sl in zip(dirs, sls)]
    AR = each(lambda a, r: cat([a, r]), At, Rt)
    BKs = each(lambda b, k: cat([stack(b), stack(k)]), Bt, Kt)
    Vs = [stack(v) for v in V]
    sc = each(_dot_nt, AR, BKs)
    Aab = [jnp.where(strict[d], x[:T, :128], 0.0) for d, x in zip(dirs, sc)]
    Aak = [jnp.where(strict[d], x[:T, 128:], 0.0) for d, x in zip(dirs, sc)]
    Mrbk = [cat([jnp.where(incl[d], x[T:, :128], 0.0), jnp.where(incl[d], x[T:, 128:], 0.0)], axis=1)
            for d, x in zip(dirs, sc)]

    N = [jnp.where(base_blk, x, 0.0) for x in Aab]
    P = [eye + x for x in N]
    N = each(lambda n: mm(n, stack(n)), N)
    NP = each(lambda n, q: mm(cat([n, q]), stack(n)), N, P)
    P = each(lambda q, x: q + x[T:], P, NP)
    P = each(lambda q, x: q + mm(q, stack(x[:T])), P, NP)
    for off_blk in merge_blks:
        CP = each(lambda a, q: mm(jnp.where(off_blk, a, 0.0), stack(q)), Aab, P)
        P = each(lambda q, x: q + mm(q, stack(x)), P, CP)

    AV = each(mm, Aak, Vs)
    S = [s_ref[d, p] for d, p in chains]
    XS = each(lambda a, s_: _dot_nt(a, s_.astype(BF16)), AR, S)
    U = each(lambda q, x, w: mm(q, stack(x[:T] + w)).astype(BF16), P, XS, AV)
    Y = each(lambda x, m, u, v: x[T:] + mm(m, cat([stack(u), v])), XS, Mrbk, U, Vs)
    dS = each(lambda u, v, b, k: _dot_tn(cat([u, v]), cat([b, k])), U, V, Bt, Kt)
    same_head_sq = (lane2 // RW_N) == (row2 // RW_N)
    for c, (d, p) in enumerate(chains):
        refs[d][6][0, :, sls[c]] = Y[c]
        s_ref[d, p] = (S[c] + jnp.where(same_head_sq, dS[c], 0.0)) * refs[d][5][0, 0, 0, :, sls[c]]


def _rw_scan(at, rt, bt, kt, v, g, n_ctx):
    bsz, _, L, _ = at.shape
    nc, nc_ctx = L // RW_T, n_ctx // RW_T

    def specs(d):
        ch = lambda j: _scan_chunk(d, j, nc_ctx, nc)
        dspec = pl.BlockSpec((1, 1, RW_T, BR), lambda b, j: (b, d, ch(j), 0))
        tspec = pl.BlockSpec((1, RW_T, BR), lambda b, j: (b, ch(j), 0))
        return tspec, [dspec, dspec, dspec, dspec, tspec,
                       pl.BlockSpec((1, 1, 1, 1, BR), lambda b, j: (b, d, ch(j), 0, 0))]

    (out_f, in_f), (out_b, in_b) = specs(0), specs(1)
    shape = jax.ShapeDtypeStruct((bsz, L, BR), F32)
    return pl.pallas_call(
        _rwscan_kernel,
        grid=(bsz, nc),
        in_specs=in_f + in_b,
        out_specs=[out_f, out_b],
        out_shape=[shape, shape],
        scratch_shapes=[pltpu.VMEM((2, BR // 128, 128, 128), F32)],
        compiler_params=_cparams(("parallel", "arbitrary")),
        name="rwkv_scan",
    )(at, rt, bt, kt, v, g, at, rt, bt, kt, v, g)


def _atprep_kernel(q_ref, k_ref, v_ref, cos_ref, sin_ref, qg_ref, kg_ref, qo_ref, ko_ref, vo_ref):
    cos = cos_ref[...]
    sin = sin_ref[...]
    lane = lax.broadcasted_iota(jnp.int32, cos.shape, 1)
    first_half = (lane % 64) < 32

    def norm_rope(x, g):
        y = x * lax.rsqrt(jnp.mean(x * x, axis=-1, keepdims=True) + EPS) * g
        partner = jnp.where(first_half, pltpu.roll(y, 96, axis=1), pltpu.roll(y, 32, axis=1))
        return y * cos + partner * sin

    for h in range(AT_H):
        sl = slice(h * AT_HD, (h + 1) * AT_HD)
        q_scale = AT_HD ** -0.5 * math.log2(math.e)
        qo_ref[0, :, sl] = (norm_rope(q_ref[0, :, sl], qg_ref[...]) * q_scale).astype(qo_ref.dtype)
    for h in range(AT_KV):
        sl = slice(h * AT_HD, (h + 1) * AT_HD)
        ko_ref[0, :, sl] = norm_rope(k_ref[0, :, sl], kg_ref[...]).astype(ko_ref.dtype)
    vo_ref[0] = v_ref[0].astype(vo_ref.dtype)


def _at_prepare(proj, cos, sin, q_g, k_g, tl=256):
    bsz, L, _ = proj.shape
    kvw = AT_KV * AT_HD
    tok = lambda w: pl.BlockSpec((1, tl, w), lambda b, i: (b, i, 0))
    return pl.pallas_call(
        _atprep_kernel,
        grid=(bsz, L // tl),
        in_specs=[pl.BlockSpec((1, tl, BR), lambda b, i: (b, i, COL["at_q"] // BR)),
                  pl.BlockSpec((1, tl, kvw), lambda b, i: (b, i, COL["at_k"] // kvw)),
                  pl.BlockSpec((1, tl, kvw), lambda b, i: (b, i, COL["at_v"] // kvw)),
                  pl.BlockSpec((tl, AT_HD), lambda b, i: (i, 0)),
                  pl.BlockSpec((tl, AT_HD), lambda b, i: (i, 0)),
                  pl.BlockSpec((1, AT_HD), lambda b, i: (0, 0)),
                  pl.BlockSpec((1, AT_HD), lambda b, i: (0, 0))],
        out_specs=[tok(BR), tok(kvw), tok(kvw)],
        out_shape=[jax.ShapeDtypeStruct((bsz, L, BR), BF16),
                   jax.ShapeDtypeStruct((bsz, L, kvw), BF16),
                   jax.ShapeDtypeStruct((bsz, L, kvw), BF16)],
        compiler_params=_cparams(("parallel", "parallel")),
        name="gqa_prepare",
    )(proj, proj, proj, cos, sin, q_g.reshape(1, AT_HD), k_g.reshape(1, AT_HD))


def _attn_kernel(q_ref, k_ref, v_ref, o_ref, s_ref, p_ref, l_ref, *, tq, n_ctx, kb, rows):
    i = pl.program_id(2)
    grp = AT_H // AT_KV
    def attend(n_keys):
        chunks = [slice(c * kb, (c + 1) * kb) for c in range(n_keys // kb)]

        def scores(r):
            s_ref[r, :, :n_keys] = _dot_nt(q_ref[0, :, r * AT_HD:(r + 1) * AT_HD], k_ref[0, :n_keys, :])

        def softmax(r):
            for rb in range(tq // rows):
                rs = slice(rb * rows, (rb + 1) * rows)
                mx = s_ref[r, rs, chunks[0]]
                for ch in chunks[1:]:
                    mx = jnp.maximum(mx, s_ref[r, rs, ch])
                m = jnp.broadcast_to(jnp.max(mx, axis=-1, keepdims=True), mx.shape)
                tot = jnp.zeros_like(mx)
                for ch in chunks:
                    p = jnp.exp2(s_ref[r, rs, ch] - m)
                    tot = tot + p
                    p_ref[r, rs, ch] = p.astype(BF16)
                l_ref[r, rs, :] = jnp.broadcast_to(jnp.sum(tot, axis=-1, keepdims=True), (rows, AT_HD))

        def values(r):
            o_ref[0, :, r * AT_HD:(r + 1) * AT_HD] = _dot(p_ref[r, :, :n_keys], v_ref[0, :n_keys, :]) / l_ref[r]

        stages = (scores, softmax, values)
        for t in range(grp + len(stages) - 1):
            for st, fn in enumerate(stages):
                if 0 <= t - st < grp:
                    fn(t - st)

    @pl.when(i * tq < n_ctx)
    def _():
        attend(n_ctx)

    @pl.when(i * tq >= n_ctx)
    def _():
        attend(k_ref.shape[1])


def _attention(q, k, v, n_ctx, tq=256, kb=256, rows=32):
    bsz, L, _ = q.shape
    gw = (AT_H // AT_KV) * AT_HD
    return pl.pallas_call(
        functools.partial(_attn_kernel, tq=tq, n_ctx=n_ctx, kb=kb, rows=rows),
        scratch_shapes=[pltpu.VMEM((gw // AT_HD, tq, L), F32), pltpu.VMEM((gw // AT_HD, tq, L), BF16),
                        pltpu.VMEM((gw // AT_HD, tq, AT_HD), F32)],
        grid=(bsz, AT_KV, L // tq),
        in_specs=[pl.BlockSpec((1, tq, gw), lambda b, g, i: (b, i, g)),
                  pl.BlockSpec((1, L, AT_HD), lambda b, g, i: (b, 0, g)),
                  pl.BlockSpec((1, L, AT_HD), lambda b, g, i: (b, 0, g))],
        out_specs=pl.BlockSpec((1, tq, gw), lambda b, g, i: (b, i, g)),
        out_shape=jax.ShapeDtypeStruct((bsz, L, BR), F32),
        compiler_params=_cparams(("parallel", "parallel", "parallel")),
        name="gqa_attention",
    )(q, k, v)


def _cap_gates(pre):
    return GATE_CAP * jnp.tanh(pre / GATE_CAP)


def _log_sigmoid(x):
    return jnp.minimum(x, 0.0) - jnp.log1p(jnp.exp(-jnp.abs(x)))


def _mlstm_kernel(qf_ref, kf_ref, vf_ref, gcf_ref, qb_ref, kb_ref, vb_ref, gcb_ref,
                  bc_ref, br_ref, hf_ref, hb_ref, c_ref, n_ref, m_ref):
    j = pl.program_id(1)
    T = ML_T

    @pl.when(j == 0)
    def _():
        c_ref[...] = jnp.zeros_like(c_ref)
        n_ref[...] = jnp.zeros_like(n_ref)
        m_ref[...] = jnp.zeros_like(m_ref)

    r2 = lax.broadcasted_iota(jnp.int32, (T, T), 0)
    c2 = lax.broadcasted_iota(jnp.int32, (T, T), 1)
    lane16 = lax.broadcasted_iota(jnp.int32, (T, 16), 1)
    sub16 = lax.broadcasted_iota(jnp.int32, (16, T), 0)

    refs = ((qf_ref, kf_ref, vf_ref, gcf_ref, None, hf_ref), (qb_ref, kb_ref, vb_ref, gcb_ref, None, hb_ref))
    seen = [c2 <= r2, c2 >= r2]
    seen_t = [r2 <= c2, r2 >= c2]
    gc = [_cap_gates(refs[d][3][0][:, :16] + bc_ref[...]) for d in range(2)]
    gr = [_cap_gates(refs[d][3][0].T[:16, :] + br_ref[...]) for d in range(2)]
    lsc = [_log_sigmoid(x) for x in gc]
    lsr = [_log_sigmoid(x) for x in gr]

    chains = [(d, h) for d in range(2) for h in range(ML_H)]
    each = lambda fn, *lists: [fn(*xs) for xs in zip(*lists)]
    pick_row = lambda x, idx: jnp.sum(jnp.where(sub16 == idx, x, 0.0), axis=0, keepdims=True)
    pick_col = lambda x, idx: jnp.sum(jnp.where(lane16 == idx, x, 0.0), axis=1, keepdims=True)
    li_row = [pick_row(gr[d], d * ML_H + h) for d, h in chains]
    lf_row = [pick_row(lsr[d], (2 + d) * ML_H + h) for d, h in chains]
    li_col = [pick_col(gc[d], d * ML_H + h) for d, h in chains]
    lf_col = [pick_col(lsc[d], (2 + d) * ML_H + h) for d, h in chains]
    b_col = [jnp.sum(jnp.where(seen[d], x, 0.0), axis=1, keepdims=True) for (d, _), x in zip(chains, lf_row)]
    b_row = [jnp.sum(jnp.where(seen_t[d], x, 0.0), axis=0, keepdims=True) for (d, _), x in zip(chains, lf_col)]
    g = [jnp.sum(x, axis=0, keepdims=True) for x in lf_col]
    m_prev = [m_ref[d, h] for d, h in chains]
    q = [refs[d][0][0, :, h * ML_DK:(h + 1) * ML_DK] * (ML_DK ** -0.5) for d, h in chains]
    k = [refs[d][1][0, :, h * ML_DK:(h + 1) * ML_DK] for d, h in chains]
    vb = [refs[d][2][0, :, h * ML_DV:(h + 1) * ML_DV].astype(BF16) for d, h in chains]
    C = [c_ref[d, h] for d, h in chains]
    n = [n_ref[d, h] for d, h in chains]
    qb = [x.astype(BF16) for x in q]

    dmat = [jnp.where(seen[d], bc - br + li, -jnp.inf) for (d, _), bc, br, li in zip(chains, b_col, b_row, li_row)]
    m_inter = each(lambda bc, m: bc + m, b_col, m_prev)
    m_t = each(lambda mi, dm: jnp.maximum(mi, jnp.max(dm, axis=-1, keepdims=True)), m_inter, dmat)
    w_inter = each(lambda mi, mt: jnp.exp(mi - mt), m_inter, m_t)
    qk = each(lambda a, b: _dot_nt(a, b.astype(BF16)), qb, k)
    qc = each(lambda a, b: _dot(a, b.astype(BF16)), qb, C)
    s = each(lambda x, dm, mt: x * jnp.exp(dm - mt), qk, dmat, m_t)
    sv = each(lambda a, b: _dot(a.astype(BF16), b), s, vb)
    qn = each(lambda a, b: jnp.sum(a * b, axis=-1, keepdims=True), q, n)
    den = each(lambda w, a, x: w * a + jnp.sum(x, axis=-1, keepdims=True), w_inter, qn, s)
    for (d, h), w, a, b, dn, mt in zip(chains, w_inter, qc, sv, den, m_t):
        refs[d][5][0, :, h * ML_DV:(h + 1) * ML_DV] = (w * a + b) / jnp.maximum(jnp.abs(dn), jnp.exp(-mt))

    loga = each(lambda g_, bc, li: g_ - bc + li, g, b_col, li_col)
    m_new = each(lambda g_, m, la: jnp.maximum(g_ + m, jnp.max(la, axis=0, keepdims=True)), g, m_prev, loga)
    carry = each(lambda g_, m, mn: jnp.exp(g_ + m - mn), g, m_prev, m_new)
    wk = each(lambda la, mn, k_: jnp.exp(la - mn) * k_, loga, m_new, k)
    kv = each(lambda a, b: _dot_tn(a.astype(BF16), b), wk, vb)
    for (d, h), cr, c_, kv_, n_, wk_, mn in zip(chains, carry, C, kv, n, wk, m_new):
        c_ref[d, h] = cr * c_ + kv_
        n_ref[d, h] = cr * n_ + jnp.sum(wk_, axis=0, keepdims=True)
        m_ref[d, h] = mn


def _mlstm(proj, bias_col, bias_row, n_ctx):
    bsz, L, _ = proj.shape
    nc, nc_ctx = L // ML_T, n_ctx // ML_T
    qw, vw = ML_H * ML_DK, ML_H * ML_DV

    def dir_specs(d):
        ch = lambda j: _scan_chunk(d, j, nc_ctx, nc)
        return [pl.BlockSpec((1, ML_T, qw), lambda b, j: (b, ch(j), COL["ml_q"] // qw)),
                pl.BlockSpec((1, ML_T, qw), lambda b, j: (b, ch(j), COL["ml_k"] // qw)),
                pl.BlockSpec((1, ML_T, vw), lambda b, j: (b, ch(j), COL["ml_v"] // vw)),
                pl.BlockSpec((1, ML_T, 128), lambda b, j: (b, ch(j), COL["ml_if"] // 128))]

    def out_spec(d):
        ch = lambda j: _scan_chunk(d, j, nc_ctx, nc)
        return pl.BlockSpec((1, ML_T, vw), lambda b, j: (b, ch(j), 0))

    shape = jax.ShapeDtypeStruct((bsz, L, vw), F32)
    return pl.pallas_call(
        _mlstm_kernel,
        grid=(bsz, nc),
        in_specs=dir_specs(0) + dir_specs(1) + [pl.BlockSpec((1, 16), lambda b, j: (0, 0)),
                                                pl.BlockSpec((16, 1), lambda b, j: (0, 0))],
        out_specs=[out_spec(0), out_spec(1)],
        out_shape=[shape, shape],
        scratch_shapes=[pltpu.VMEM((2, ML_H, ML_DK, ML_DV), F32),
                        pltpu.VMEM((2, ML_H, 1, ML_DK), F32),
                        pltpu.VMEM((2, ML_H, 1, 1), F32)],
        compiler_params=_cparams(("parallel", "arbitrary")),
        name="mlstm_scan",
    )(proj, proj, proj, proj, proj, proj, proj, proj, bias_col, bias_row)


def _epilogue_kernel(yf_ref, yb_ref, bonus_ref, rwg_ref, att_ref, atg_ref, hf_ref, hb_ref, mlo_ref, mlg_ref,
                     lnw_ref, lnb_ref, mng_ref, m64_ref, b64_ref, m256_ref, b256_ref, o_ref):
    y = yf_ref[0] + yb_ref[0]
    mu = _seg_reduce(y, m64_ref[...], b64_ref[...])
    yc = y - mu
    var = _seg_reduce(yc * yc, m64_ref[...], b64_ref[...])
    ya = yc * lax.rsqrt(var + RW_GN_EPS) * lnw_ref[...] + lnb_ref[...] + bonus_ref[0]
    o_ref[0, 0] = (ya * _silu(rwg_ref[0])).astype(o_ref.dtype)

    o_ref[1, 0] = (att_ref[0] * _silu(atg_ref[0])).astype(o_ref.dtype)

    hh = hf_ref[0] + hb_ref[0]
    ms = _seg_reduce(hh * hh, m256_ref[...], b256_ref[...])
    hn = hh * lax.rsqrt(ms + EPS) * mng_ref[...]
    o_ref[2, 0] = (jax.nn.sigmoid(mlo_ref[0]) * hn * _silu(mlg_ref[0])).astype(o_ref.dtype)


def _epilogue(proj, y_f, y_b, bonus, att, h_f, h_b, p, tl=256):
    bsz, L, _ = proj.shape
    tok = pl.BlockSpec((1, tl, BR), lambda b, i: (b, i, 0))
    pc = lambda name: pl.BlockSpec((1, tl, BR), lambda b, i: (b, i, COL[name] // BR))
    vec = pl.BlockSpec((1, BR), lambda b, i: (0, 0))
    gat = pl.BlockSpec((BR, 128), lambda b, i: (0, 0))
    sct = pl.BlockSpec((128, BR), lambda b, i: (0, 0))
    return pl.pallas_call(
        _epilogue_kernel,
        grid=(bsz, L // tl),
        in_specs=[tok, tok, tok, pc("rw_g"), tok, pc("at_g"), tok, tok, pc("ml_o"), pc("ml_g"),
                  vec, vec, vec, gat, sct, gat, sct],
        out_specs=pl.BlockSpec((3, 1, tl, BR), lambda b, i: (0, b, i, 0)),
        out_shape=jax.ShapeDtypeStruct((3, bsz, L, BR), BF16),
        compiler_params=_cparams(("parallel", "parallel")),
        name="branch_epilogue",
    )(y_f, y_b, bonus, proj, att, proj, h_f, h_b, proj, proj,
      p["rw_ln_w"].reshape(1, BR), p["rw_ln_b"].reshape(1, BR), p["ml_norm_g"].reshape(1, BR),
      p["mean64"], p["bcast64"], p["mean256"], p["bcast256"])


def _merge_kernel(y_ref, w_ref, g0_ref, g1_ref, g2_ref, o_ref, wb_ref):
    @pl.when(pl.program_id(1) == 0)
    def _():
        wb_ref[...] = w_ref[0].astype(BF16)

    acc = jax.nn.sigmoid(g0_ref[...]) * _dot(y_ref[0], wb_ref[0])
    acc += jax.nn.sigmoid(g1_ref[...]) * _dot(y_ref[1], wb_ref[1])
    acc += jax.nn.sigmoid(g2_ref[...]) * _dot(y_ref[2], wb_ref[2])
    o_ref[...] = acc.astype(o_ref.dtype)


def _merge(ys, w_branch, layer, proj2d, tm=512, tn=1024):
    _, m, _ = ys.shape
    nb = D // tn
    gate = lambda n: pl.BlockSpec((tm, tn), lambda j, i: (i, n * nb + j))
    return pl.pallas_call(
        _merge_kernel,
        grid=(D // tn, m // tm),
        in_specs=[pl.BlockSpec((3, tm, BR), lambda j, i: (0, i, 0)),
                  pl.BlockSpec((1, 3, BR, tn), lambda j, i: (layer, 0, 0, j), pipeline_mode=pl.Buffered(1)),
                  gate(0), gate(1), gate(2)],
        out_specs=pl.BlockSpec((tm, tn), lambda j, i: (i, j)),
        out_shape=jax.ShapeDtypeStruct((m, D), BF16),
        scratch_shapes=[pltpu.VMEM((3, BR, tn), BF16)],
        compiler_params=_cparams(("arbitrary", "arbitrary")),
        name="branch_merge",
    )(ys, w_branch, proj2d, proj2d, proj2d)


def _outproj_kernel(*refs, nct):
    a_ref, w_ref, gt_ref, *z_refs, o_ref, wb_ref = refs

    @pl.when(jnp.logical_and(pl.program_id(1) == 0, pl.program_id(2) == 0))
    def _():
        wb_ref[...] = w_ref[0].astype(BF16)

    def body(z):
        o_ref[0] = z + gt_ref[0, 0] * _dot(a_ref[0], wb_ref[...])

    _for_stream_tile(z_refs, pl.program_id(2) < nct, body)


def _out_projection(mixed, w_out, layer, z, gate, n_ctx, tl=256, tn=1024):
    bsz, L, _ = mixed.shape
    nct = n_ctx // tl
    z_specs, z_args = _stream_specs(z, tl, nct, (1, tl, tn), lambda j, b, i: (b, i, j))
    return pl.pallas_call(
        functools.partial(_outproj_kernel, nct=nct),
        grid=(D // tn, bsz, L // tl),
        in_specs=[pl.BlockSpec((1, tl, D), lambda j, b, i: (b, i, 0)),
                  pl.BlockSpec((1, D, tn), lambda j, b, i: (layer, 0, j)),
                  pl.BlockSpec((1, 1, 1, tn), lambda j, b, i: (b, (i >= nct).astype(jnp.int32), 0, j))] + z_specs,
        out_specs=pl.BlockSpec((1, tl, tn), lambda j, b, i: (b, i, j)),
        out_shape=jax.ShapeDtypeStruct((bsz, L, D), F32),
        scratch_shapes=[pltpu.VMEM((D, tn), BF16)],
        compiler_params=_cparams(("arbitrary", "arbitrary", "arbitrary")),
        name="out_projection",
    )(mixed, w_out, gate, *z_args)


def _rope_tables(n_ctx, n_lat):
    rows = n_lat // GRID_W
    row = jnp.repeat(jnp.arange(rows), GRID_W).astype(F32)
    col = jnp.tile(jnp.arange(GRID_W), rows).astype(F32)
    inv_freq = ROPE_THETA ** (-jnp.arange(0, AT_HD // 2, 2, dtype=F32) / (AT_HD // 2))
    ang_lat = jnp.stack([row[:, None] * inv_freq, col[:, None] * inv_freq], axis=1)
    ang = jnp.concatenate([jnp.zeros((n_ctx, 2, AT_HD // 4), F32), ang_lat], axis=0)
    cos, sin = jnp.cos(ang), jnp.sin(ang)
    cos_t = jnp.concatenate([cos[:, 0], cos[:, 0], cos[:, 1], cos[:, 1]], axis=-1)
    sin_t = jnp.concatenate([-sin[:, 0], sin[:, 0], -sin[:, 1], sin[:, 1]], axis=-1)
    return cos_t, sin_t


def _group_consts(width, value):
    member = (np.arange(BR)[:, None] // width) == np.arange(128)[None, :]
    return (jnp.asarray(np.where(member, value, 0.0), dtype=BF16),
            jnp.asarray(np.where(member.T, 1.0, 0.0), dtype=BF16))


def _chunk_tri(tl):
    t = np.arange(tl)
    same = (t[:, None] // RW_T) == (t[None, :] // RW_T)
    fwd = same & (t[None, :] <= t[:, None])
    bwd = same & (t[None, :] >= t[:, None])
    return jnp.asarray(np.stack([fwd, bwd]).astype(np.float32), dtype=BF16)


def kernel(x, c, ctx, c_ctx, norm_g, w_ada, b_ada, w_in, shift_mu, rw_w_up, rw_w0, rw_a_up, rw_a0, rw_k_k, rw_k_a, rw_r_k, rw_ln_w, rw_ln_b, at_q_g, at_k_g, ml_gate_b, ml_norm_g, w_branch, w_out, final_g):
    bsz, n_lat, _ = x.shape
    n_ctx = ctx.shape[1]
    L = n_ctx + n_lat
    depth = w_in.shape[0]

    cos_t, sin_t = _rope_tables(n_ctx, n_lat)
    sum64, bcast64 = _group_consts(RW_N, 1.0)
    mean64, _ = _group_consts(RW_N, 1.0 / RW_N)
    mean256, bcast256 = _group_consts(ML_DV, 1.0 / ML_DV)
    consts = {"sum64": sum64, "bcast64": bcast64, "mean64": mean64, "mean256": mean256, "bcast256": bcast256,
              "tri": _chunk_tri(256)}

    cc = jnp.concatenate([c, c_ctx[None], jnp.zeros((8 - bsz - 1, D), F32)], axis=0)
    mod = _modulation(cc, w_ada, b_ada)

    w_in_t = jnp.swapaxes(w_in, 1, 2)
    z = (ctx, x)
    for l in range(depth):
        sh, sc, gt = mod[l, :, :D], mod[l, :, D:2 * D], mod[l, :, 2 * D:]
        pick = lambda t: jnp.stack([jnp.broadcast_to(t[bsz], (bsz, D)), t[:bsz]], axis=1)[:, :, None, :]
        scale = pick((1.0 + sc) * norm_g[l])
        shift = pick(sh)
        gate = pick(gt)

        h = _norm_mod(z, scale, shift, n_ctx, L, BF16)
        proj2d = _in_projection(h.reshape(bsz * L, D), w_in_t, l)
        proj = proj2d.reshape(bsz, L, N_PROJ)

        p = dict(consts, shift_mu=shift_mu[l], rw_w_up=rw_w_up[l], rw_w0=rw_w0[l], rw_a_up=rw_a_up[l],
                 rw_a0=rw_a0[l], rw_k_k=rw_k_k[l], rw_k_a=rw_k_a[l], rw_r_k=rw_r_k[l], rw_ln_w=rw_ln_w[l],
                 rw_ln_b=rw_ln_b[l], ml_norm_g=ml_norm_g[l])
        at, rt, bt, kt, v_rw, bonus, g_rw = _rw_prepare(proj, p, n_ctx)
        y_f, y_b = _rw_scan(at, rt, bt, kt, v_rw, g_rw, n_ctx)

        qn, kn, vn = _at_prepare(proj, cos_t, sin_t, at_q_g[l], at_k_g[l])
        att = _attention(qn, kn, vn, n_ctx)

        bias = ml_gate_b[l].reshape(16)
        h_f, h_b = _mlstm(proj, bias.reshape(1, 16), bias.reshape(16, 1), n_ctx)

        ys = _epilogue(proj, y_f, y_b, bonus, att, h_f, h_b, p)
        mixed = _merge(ys.reshape(3, bsz * L, BR), w_branch, l, proj2d)
        z = _out_projection(mixed.reshape(bsz, L, D), w_out, l, z, gate, n_ctx)

    ones = jnp.ones((bsz, 2, 1, D), F32) * final_g
    zeros = jnp.zeros((bsz, 2, 1, D), F32)
    return _norm_mod(z, ones, zeros, n_ctx, L, F32, latent_only=True)
```

```python
import functools
import math

import numpy as np
import jax
import jax.numpy as jnp
from jax import lax
from jax.experimental import pallas as pl
from jax.experimental.pallas import tpu as pltpu

F32 = jnp.float32
BF16 = jnp.bfloat16
HI = lax.Precision.HIGHEST

D = 2048
BR = 1024
EPS = 1e-6
GRID_W = 64

RW_H, RW_N, RW_LORA = 16, 64, 64
RW_GN_EPS = 64e-5
RW_T = 64
RW_INV_BASE = 8

AT_H, AT_KV, AT_HD = 8, 2, 128
ROPE_THETA = 10000.0

ML_H, ML_DK, ML_DV, ML_T = 4, 128, 256, 128
GATE_CAP = 15.0

D_IN = 17168
LANES, SUBLANES = 128, 8
PROJ_TN = 512
_PROJ_GROUPS = (
    ("merge", 11024, 6144), ("r", 0, 1024), ("k", 1024, 1024), ("v", 2048, 1024),
    ("rw_g", 3328, 1024), ("at_q", 4352, 1024), ("at_g", 5888, 1024), ("ml_v", 7936, 1024),
    ("ml_o", 8960, 1024), ("ml_g", 10000, 1024), ("ml_q", 6912, 512), ("ml_k", 7424, 512),
    ("at_k", 5376, 512), ("wdad", 3072, 512), ("ml_if", 9984, 512),
)
COL = {}
PROJ_SRC = []
for _name, _start, _width in _PROJ_GROUPS:
    COL[_name] = len(PROJ_SRC) * PROJ_TN
    PROJ_SRC += [_start + t * PROJ_TN for t in range(_width // PROJ_TN)]
COL["at_v"] = COL["at_k"] + 256
N_PROJ = len(PROJ_SRC) * PROJ_TN
for _name, _blk in (("merge", 2048), ("r", 1024), ("k", 1024), ("v", 1024), ("rw_g", 1024), ("at_q", 1024),
                    ("at_g", 1024), ("ml_v", 1024), ("ml_o", 1024), ("ml_g", 1024), ("ml_q", 512), ("ml_k", 512),
                    ("at_k", 256), ("at_v", 256), ("wdad", 256), ("ml_if", 128)):
    assert COL[_name] % _blk == 0, _name

VMEM_LIMIT = 48 * 1024 * 1024


def _cparams(sem):
    return pltpu.CompilerParams(dimension_semantics=sem, vmem_limit_bytes=VMEM_LIMIT)


def _dot(a, b, **kw):
    return jnp.dot(a, b, preferred_element_type=F32, **kw)


def _dot_nt(a, b, **kw):
    return lax.dot_general(a, b, (((1,), (1,)), ((), ())), preferred_element_type=F32, **kw)


def _dot_tn(a, b, **kw):
    return lax.dot_general(a, b, (((0,), (0,)), ((), ())), preferred_element_type=F32, **kw)


def _split3(x):
    h = x.astype(BF16)
    r = x - h.astype(F32)
    m = r.astype(BF16)
    l = (r - m.astype(F32)).astype(BF16)
    return h, m, l


def _split2(x):
    h = x.astype(BF16)
    return h, (x - h.astype(F32)).astype(BF16)


def _seg_reduce(x, gather, scatter):
    h, l = _split2(x)
    sh, sl = _split2(_dot(h, gather) + _dot(l, gather))
    return _dot(sh, scatter) + _dot(sl, scatter)


def _dot_const_lhs(c, x):
    h, m, l = _split3(x)
    return _dot(c, h) + _dot(c, m) + _dot(c, l)


def _sigmoid(x):
    return 0.5 * jnp.tanh(0.5 * x) + 0.5


def _silu(x):
    return x * _sigmoid(x)


def _mod_kernel(c_ref, w_ref, b_ref, o_ref):
    @pl.when(pl.program_id(1) == 0)
    def _():
        o_ref[0] = jnp.broadcast_to(b_ref[0], o_ref.shape[1:])

    xh, xl = _split2(_silu(c_ref[...]))
    wh, wl = _split2(w_ref[0])
    o_ref[0] += _dot(xh, wh) + _dot(xh, wl) + _dot(xl, wh)


def _modulation(cc, w_ada, b_ada, tk=256):
    depth = w_ada.shape[0]
    return pl.pallas_call(
        _mod_kernel,
        grid=(depth, D // tk),
        in_specs=[pl.BlockSpec((8, tk), lambda l, k: (0, k)),
                  pl.BlockSpec((1, tk, 3 * D), lambda l, k: (l, k, 0)),
                  pl.BlockSpec((1, 1, 3 * D), lambda l, k: (l, 0, 0))],
        out_specs=pl.BlockSpec((1, 8, 3 * D), lambda l, k: (l, 0, 0)),
        out_shape=jax.ShapeDtypeStruct((depth, 8, 3 * D), F32),
        compiler_params=_cparams(("parallel", "arbitrary")),
        name="adaln_modulation",
    )(cc, w_ada, b_ada.reshape(depth, 1, 3 * D))


def _stream_specs(z, tl, nct, block, index):
    if not isinstance(z, tuple):
        return [pl.BlockSpec(block, lambda *g: index(*g))], [z]

    def ctx_index(*g):
        b, i, j = index(*g)
        return b, jnp.minimum(i, nct - 1), j

    def lat_index(*g):
        b, i, j = index(*g)
        return b, jnp.maximum(i - nct, 0), j

    return [pl.BlockSpec(block, ctx_index), pl.BlockSpec(block, lat_index)], list(z)


def _for_stream_tile(z_refs, is_ctx, body):
    if len(z_refs) == 1:
        body(z_refs[0][0])
        return
    pl.when(is_ctx)(lambda: body(z_refs[0][0]))
    pl.when(jnp.logical_not(is_ctx))(lambda: body(z_refs[1][0]))


def _norm_kernel(*refs, nct, first):
    *z_refs, sc_ref, sh_ref, o_ref = refs

    def body(x):
        y = x * lax.rsqrt(jnp.mean(x * x, axis=-1, keepdims=True) + EPS)
        o_ref[0] = (y * sc_ref[0, 0] + sh_ref[0, 0]).astype(o_ref.dtype)

    _for_stream_tile(z_refs, pl.program_id(1) + first < nct, body)


def _norm_mod(z, scale, shift, n_ctx, L, out_dtype, latent_only=False, tl=256):
    bsz = scale.shape[0]
    nct = n_ctx // tl
    first = nct if latent_only else 0
    seg = lambda b, i: (b, (i + first >= nct).astype(jnp.int32), 0, 0)
    z_specs, z_args = _stream_specs(z, tl, nct, (1, tl, D), lambda b, i: (b, i + first, 0))
    return pl.pallas_call(
        functools.partial(_norm_kernel, nct=nct, first=first),
        grid=(bsz, L // tl - first),
        in_specs=z_specs + [pl.BlockSpec((1, 1, 1, D), seg), pl.BlockSpec((1, 1, 1, D), seg)],
        out_specs=pl.BlockSpec((1, tl, D), lambda b, i: (b, i, 0)),
        out_shape=jax.ShapeDtypeStruct((bsz, L - first * tl, D), out_dtype),
        compiler_params=_cparams(("parallel", "parallel")),
        name="rmsnorm_modulate",
    )(*z_args, scale, shift)


def _inproj_kernel(start_ref, a_ref, wt_ref, o_ref):
    o_ref[...] = _dot_nt(a_ref[...], wt_ref[0].astype(BF16))


def _in_projection(a, w_t, layer, tm=2176):
    m, k = a.shape
    tm = math.gcd(m, tm)
    assert all(s % SUBLANES == 0 for s in PROJ_SRC)
    start = np.array(PROJ_SRC, np.int32) // SUBLANES
    grid_spec = pltpu.PrefetchScalarGridSpec(
        num_scalar_prefetch=1,
        grid=(m // tm, len(PROJ_SRC)),
        in_specs=[pl.BlockSpec((tm, k), lambda i, j, st: (i, 0)),
                  pl.BlockSpec((pl.Element(1), pl.Element(PROJ_TN), pl.Element(k)),
                               lambda i, j, st: (layer, st[j] * SUBLANES, 0))],
        out_specs=pl.BlockSpec((tm, PROJ_TN), lambda i, j, st: (i, j)),
    )
    return pl.pallas_call(
        _inproj_kernel,
        grid_spec=grid_spec,
        out_shape=jax.ShapeDtypeStruct((m, N_PROJ), F32),
        compiler_params=_cparams(("parallel", "arbitrary")),
        name="in_projection",
    )(jnp.asarray(start), a, w_t)


def _rwprep_kernel(r_ref, rp_ref, rn_ref, k_ref, kp_ref, kn_ref, v_ref, vp_ref, vn_ref,
                   w_ref, wp_ref, wn_ref, mur_ref, muk_ref, muv_ref, muw_ref,
                   wup_ref, w0_ref, aup_ref, a0_ref, kk_ref, ka_ref, rk_ref, gat_ref, sct_ref, tri_ref,
                   at_ref, rt_ref, bt_ref, kt_ref, vo_ref, bonus_ref, g_ref, *, tl, n_ctx, L):
    i = pl.program_id(1)
    start = i * tl
    has_prev = jnp.logical_and(start != 0, start != n_ctx).astype(F32)
    has_next = jnp.logical_and(start + tl != n_ctx, start + tl != L).astype(F32)

    def shift(x_ref, p_ref, n_ref, mu_ref):
        x = x_ref[0]
        row = lax.broadcasted_iota(jnp.int32, x.shape, 0)
        prev = jnp.where(row == 0, p_ref[0, 7:8, :] * has_prev, pltpu.roll(x, 1, axis=0))
        nxt = jnp.where(row == tl - 1, n_ref[0, 0:1, :] * has_next, pltpu.roll(x, tl - 1, axis=0))
        mu0, mu1 = mu_ref[0:1, :], mu_ref[1:2, :]
        return x * (1.0 - mu0 - mu1) + mu0 * prev + mu1 * nxt

    r = shift(r_ref, rp_ref, rn_ref, mur_ref)
    k = shift(k_ref, kp_ref, kn_ref, muk_ref)
    v = shift(v_ref, vp_ref, vn_ref, muv_ref)
    wdad = shift(w_ref, wp_ref, wn_ref, muw_ref)
    gat, sct = gat_ref[...], sct_ref[...]

    kk = k * kk_ref[...]
    kk = kk * lax.rsqrt(jnp.maximum(_seg_reduce(kk * kk, gat, sct), 1e-24))
    bonus_ref[0] = _seg_reduce(r * k * rk_ref[...], gat, sct) * v
    vo_ref[0] = v.astype(vo_ref.dtype)

    for d in range(2):
        wd = wdad[:, d * RW_LORA:(d + 1) * RW_LORA]
        ad = wdad[:, 2 * RW_LORA + d * RW_LORA:2 * RW_LORA + (d + 1) * RW_LORA]
        u = w0_ref[d:d + 1, :] + _mmx(jnp.tanh(wd), wup_ref[d], "x3")
        logw = -math.exp(-0.5) * _sigmoid(u)
        a = _sigmoid(a0_ref[d:d + 1, :] + _mmx(ad, aup_ref[d], "x3"))
        kd = k * (1.0 + (a - 1.0) * ka_ref[...])
        lg = _dot_const_lhs(tri_ref[d], logw)
        e_pos = jnp.exp(lg)
        e_neg = 1.0 / e_pos
        at_ref[0, d] = (-kk * jnp.exp(lg - logw)).astype(at_ref.dtype)
        rt_ref[0, d] = (r * e_pos).astype(rt_ref.dtype)
        bt_ref[0, d] = (kk * a * e_neg).astype(bt_ref.dtype)
        kt_ref[0, d] = (kd * e_neg).astype(kt_ref.dtype)
        for c in range(tl // RW_T):
            g_ref[0, d, c] = jnp.exp(jnp.sum(logw[c * RW_T:(c + 1) * RW_T], axis=0, keepdims=True))


def _rw_prepare(proj, p, n_ctx, tl=256):
    bsz, L, _ = proj.shape
    h8 = tl // 8
    nblk8 = L // 8
    main = lambda w, cb: pl.BlockSpec((1, tl, w), lambda b, i: (b, i, cb))
    prev = lambda w, cb: pl.BlockSpec((1, 8, w), lambda b, i: (b, jnp.maximum(i * h8 - 1, 0), cb))
    nxt = lambda w, cb: pl.BlockSpec((1, 8, w), lambda b, i: (b, jnp.minimum((i + 1) * h8, nblk8 - 1), cb))
    full = lambda shape: pl.BlockSpec(shape, lambda b, i: (0,) * len(shape))
    in_specs = []
    args = []
    for name, w in (("r", 1024), ("k", 1024), ("v", 1024), ("wdad", 256)):
        cb = COL[name] // w
        in_specs += [main(w, cb), prev(w, cb), nxt(w, cb)]
        args += [proj, proj, proj]
    mu = p["shift_mu"]
    in_specs += [pl.BlockSpec((2, 1024), lambda b, i: (0, 0)), pl.BlockSpec((2, 1024), lambda b, i: (0, 1)),
                 pl.BlockSpec((2, 1024), lambda b, i: (0, 2)), pl.BlockSpec((2, 256), lambda b, i: (0, 12))]
    args += [mu, mu, mu, mu]
    in_specs += [full((2, RW_LORA, BR)), full((2, BR)), full((2, RW_LORA, BR)), full((2, BR)),
                 full((1, BR)), full((1, BR)), full((1, BR)), full((BR, 128)), full((128, BR)), full((2, tl, tl))]
    args += [p["rw_w_up"], p["rw_w0"], p["rw_a_up"], p["rw_a0"], p["rw_k_k"].reshape(1, BR),
             p["rw_k_a"].reshape(1, BR), p["rw_r_k"].reshape(1, BR), p["sum64"], p["bcast64"], p["tri"]]
    dir_spec = pl.BlockSpec((1, 2, tl, BR), lambda b, i: (b, 0, i, 0))
    tok_spec = pl.BlockSpec((1, tl, BR), lambda b, i: (b, i, 0))
    nchunk = tl // RW_T
    out_specs = [dir_spec, dir_spec, dir_spec, dir_spec, tok_spec, tok_spec,
                 pl.BlockSpec((1, 2, nchunk, 1, BR), lambda b, i: (b, 0, i, 0, 0))]
    dir_shape = jax.ShapeDtypeStruct((bsz, 2, L, BR), BF16)
    out_shape = [dir_shape, dir_shape, dir_shape, dir_shape,
                 jax.ShapeDtypeStruct((bsz, L, BR), BF16), jax.ShapeDtypeStruct((bsz, L, BR), F32),
                 jax.ShapeDtypeStruct((bsz, 2, L // RW_T, 1, BR), F32)]
    return pl.pallas_call(
        functools.partial(_rwprep_kernel, tl=tl, n_ctx=n_ctx, L=L),
        grid=(bsz, L // tl),
        in_specs=in_specs, out_specs=out_specs, out_shape=out_shape,
        compiler_params=_cparams(("parallel", "parallel")),
        name="rwkv_prepare",
    )(*args)


def _scan_chunk(d, j, nc_ctx, nc):
    bwd = jnp.where(j < nc_ctx, nc_ctx - 1 - j, nc - 1 - j + nc_ctx)
    return jnp.where(d == 0, j, bwd)


def _mmx(a, b, mode, dims="nn"):
    f = {"nn": _dot, "nt": _dot_nt, "tn": _dot_tn}[dims]
    ah, bh = a.astype(BF16), b.astype(BF16)
    if mode == "bf16":
        return f(ah, bh)
    al = (a - ah.astype(F32)).astype(BF16)
    bl = (b - bh.astype(F32)).astype(BF16)
    return f(ah, bh) + f(ah, bl) + f(al, bh)


def _rwscan_kernel(atf_ref, rtf_ref, btf_ref, ktf_ref, vf_ref, gf_ref,
                   atb_ref, rtb_ref, btb_ref, ktb_ref, vb_ref, gb_ref, yf_ref, yb_ref, s_ref):
    j = pl.program_id(1)
    T = RW_T

    @pl.when(j == 0)
    def _():
        s_ref[...] = jnp.zeros_like(s_ref)

    tok = lax.broadcasted_iota(jnp.int32, (T, 128), 0)
    col = lax.broadcasted_iota(jnp.int32, (T, 128), 1) % RW_N
    strict = [col < tok, col > tok]
    incl = [col <= tok, col >= tok]
    eye = jnp.where(col == tok, 1.0, 0.0)
    base_blk = (tok // RW_INV_BASE) == (col // RW_INV_BASE)
    merge_blks = []
    s = RW_INV_BASE
    while s < T:
        merge_blks.append(jnp.logical_and((tok // (2 * s)) == (col // (2 * s)), (tok // s) != (col // s)))
        s *= 2
    row2 = lax.broadcasted_iota(jnp.int32, (2 * T, 128), 0)
    lane2 = lax.broadcasted_iota(jnp.int32, (2 * T, 128), 1)
    same_head = (lane2 // RW_N) == (row2 // T)

    def stack(x):
        x = x.astype(BF16)
        return jnp.where(same_head, jnp.concatenate([x, x], axis=0), 0.0)

    cat = lambda xs, axis=0: jnp.concatenate(xs, axis=axis)
    mm = lambda a, b: _dot(a.astype(BF16), b)

    refs = ((atf_ref, rtf_ref, btf_ref, ktf_ref, vf_ref, gf_ref, yf_ref),
            (atb_ref, rtb_ref, btb_ref, ktb_ref, vb_ref, gb_ref, yb_ref))
    chains = [(d, p) for d in range(2) for p in range(BR // 128)]
    dirs = [d for d, _ in chains]
    sls = [slice(p * 128, (p + 1) * 128) for _, p in chains]
    each = lambda fn, *lists: [fn(*xs) for xs in zip(*lists)]
    At = [refs[d][0][0, 0, :, sl] for d, sl in zip(dirs, sls)]
    Rt = [refs[d][1][0, 0, :, sl] for d, sl in zip(dirs, sls)]
    Bt = [refs[d][2][0, 0, :, sl] for d, sl in zip(dirs, sls)]
    Kt = [refs[d][3][0, 0, :, sl] for d, sl in zip(dirs, sls)]
    V = [refs[d][4][0, :, sl] for d, sl in zip(dirs, sls)]
    AR = each(lambda a, r: cat([a, r]), At, Rt)
    BKs = each(lambda b, k: cat([stack(b), stack(k)]), Bt, Kt)
    Vs = [stack(v) for v in V]
    sc = each(_dot_nt, AR, BKs)
    Aab = [jnp.where(strict[d], x[:T, :128], 0.0) for d, x in zip(dirs, sc)]
    Aak = [jnp.where(strict[d], x[:T, 128:], 0.0) for d, x in zip(dirs, sc)]
    Mrbk = [cat([jnp.where(incl[d], x[T:, :128], 0.0), jnp.where(incl[d], x[T:, 128:], 0.0)], axis=1)
            for d, x in zip(dirs, sc)]

    N = [jnp.where(base_blk, x, 0.0) for x in Aab]
    P = [eye + x for x in N]
    N = each(lambda n: mm(n, stack(n)), N)
    NP = each(lambda n, q: mm(cat([n, q]), stack(n)), N, P)
    P = each(lambda q, x: q + x[T:], P, NP)
    P = each(lambda q, x: q + mm(q, stack(x[:T])), P, NP)
    for off_blk in merge_blks:
        CP = each(lambda a, q: mm(jnp.where(off_blk, a, 0.0), stack(q)), Aab, P)
        P = each(lambda q, x: q + mm(q, stack(x)), P, CP)

    AV = each(mm, Aak, Vs)
    S = [s_ref[d, p] for d, p in chains]
    XS = each(lambda a, s_: _dot_nt(a, s_.astype(BF16)), AR, S)
    U = each(lambda q, x, w: mm(q, stack(x[:T] + w)).astype(BF16), P, XS, AV)
    Y = each(lambda x, m, u, v: x[T:] + mm(m, cat([stack(u), v])), XS, Mrbk, U, Vs)
    dS = each(lambda u, v, b, k: _dot_tn(cat([u, v]), cat([b, k])), U, V, Bt, Kt)
    same_head_sq = (lane2 // RW_N) == (row2 // RW_N)
    for c, (d, p) in enumerate(chains):
        refs[d][6][0, :, sls[c]] = Y[c]
        s_ref[d, p] = (S[c] + jnp.where(same_head_sq, dS[c], 0.0)) * refs[d][5][0, 0, 0, :, sls[c]]


def _rw_scan(at, rt, bt, kt, v, g, n_ctx):
    bsz, _, L, _ = at.shape
    nc, nc_ctx = L // RW_T, n_ctx // RW_T

    def specs(d):
        ch = lambda j: _scan_chunk(d, j, nc_ctx, nc)
        dspec = pl.BlockSpec((1, 1, RW_T, BR), lambda b, j: (b, d, ch(j), 0))
        tspec = pl.BlockSpec((1, RW_T, BR), lambda b, j: (b, ch(j), 0))
        return tspec, [dspec, dspec, dspec, dspec, tspec,
                       pl.BlockSpec((1, 1, 1, 1, BR), lambda b, j: (b, d, ch(j), 0, 0))]

    (out_f, in_f), (out_b, in_b) = specs(0), specs(1)
    shape = jax.ShapeDtypeStruct((bsz, L, BR), F32)
    return pl.pallas_call(
        _rwscan_kernel,
        grid=(bsz, nc),
        in_specs=in_f + in_b,
        out_specs=[out_f, out_b],
        out_shape=[shape, shape],
        scratch_shapes=[pltpu.VMEM((2, BR // 128, 128, 128), F32)],
        compiler_params=_cparams(("parallel", "arbitrary")),
        name="rwkv_scan",
    )(at, rt, bt, kt, v, g, at, rt, bt, kt, v, g)


def _atprep_kernel(q_ref, k_ref, v_ref, cos_ref, sin_ref, qg_ref, kg_ref, qo_ref, ko_ref, vo_ref):
    cos = cos_ref[...]
    sin = sin_ref[...]
    lane = lax.broadcasted_iota(jnp.int32, cos.shape, 1)
    first_half = (lane % 64) < 32

    q_scale = AT_HD ** -0.5 * math.log2(math.e)
    heads = [(q_ref, qo_ref, h, qg_ref[...] * q_scale) for h in range(AT_H)]
    heads += [(k_ref, ko_ref, h, kg_ref[...]) for h in range(AT_KV)]
    x = [src[0, :, h * AT_HD:(h + 1) * AT_HD] for src, _, h, _ in heads]
    y = [xi * lax.rsqrt(jnp.mean(xi * xi, axis=-1, keepdims=True) + EPS) * g for xi, (_, _, _, g) in zip(x, heads)]
    up = [pltpu.roll(yi, 96, axis=1) for yi in y]
    down = [pltpu.roll(yi, 32, axis=1) for yi in y]
    for (_, dst, h, _), yi, u, dn in zip(heads, y, up, down):
        dst[0, :, h * AT_HD:(h + 1) * AT_HD] = (yi * cos + jnp.where(first_half, u, dn) * sin).astype(dst.dtype)
    vo_ref[0] = v_ref[0].astype(vo_ref.dtype)


def _at_prepare(proj, cos, sin, q_g, k_g, tl=256):
    bsz, L, _ = proj.shape
    kvw = AT_KV * AT_HD
    tok = lambda w: pl.BlockSpec((1, tl, w), lambda b, i: (b, i, 0))
    return pl.pallas_call(
        _atprep_kernel,
        grid=(bsz, L // tl),
        in_specs=[pl.BlockSpec((1, tl, BR), lambda b, i: (b, i, COL["at_q"] // BR)),
                  pl.BlockSpec((1, tl, kvw), lambda b, i: (b, i, COL["at_k"] // kvw)),
                  pl.BlockSpec((1, tl, kvw), lambda b, i: (b, i, COL["at_v"] // kvw)),
                  pl.BlockSpec((tl, AT_HD), lambda b, i: (i, 0)),
                  pl.BlockSpec((tl, AT_HD), lambda b, i: (i, 0)),
                  pl.BlockSpec((1, AT_HD), lambda b, i: (0, 0)),
                  pl.BlockSpec((1, AT_HD), lambda b, i: (0, 0))],
        out_specs=[tok(BR), tok(kvw), tok(kvw)],
        out_shape=[jax.ShapeDtypeStruct((bsz, L, BR), BF16),
                   jax.ShapeDtypeStruct((bsz, L, kvw), BF16),
                   jax.ShapeDtypeStruct((bsz, L, kvw), BF16)],
        compiler_params=_cparams(("parallel", "parallel")),
        name="gqa_prepare",
    )(proj, proj, proj, cos, sin, q_g.reshape(1, AT_HD), k_g.reshape(1, AT_HD))


def _attn_kernel(q_ref, k_ref, v_ref, o_ref, s_ref, p_ref, l_ref, *, tq, n_ctx, kb, rows):
    i = pl.program_id(2)
    grp = AT_H // AT_KV
    def attend(n_keys):
        chunks = [slice(c * kb, (c + 1) * kb) for c in range(n_keys // kb)]

        def scores(r):
            s_ref[r, :, :n_keys] = _dot_nt(q_ref[0, :, r * AT_HD:(r + 1) * AT_HD], k_ref[0, :n_keys, :])

        def softmax(r):
            for rb in range(tq // rows):
                rs = slice(rb * rows, (rb + 1) * rows)
                mx = s_ref[r, rs, chunks[0]]
                for ch in chunks[1:]:
                    mx = jnp.maximum(mx, s_ref[r, rs, ch])
                m = jnp.broadcast_to(jnp.max(mx, axis=-1, keepdims=True), mx.shape)
                tot = jnp.zeros_like(mx)
                for ch in chunks:
                    p = jnp.exp2(s_ref[r, rs, ch] - m)
                    tot = tot + p
                    p_ref[r, rs, ch] = p.astype(BF16)
                l_ref[r, rs, :] = jnp.broadcast_to(jnp.sum(tot, axis=-1, keepdims=True), (rows, AT_HD))

        def values(r):
            o_ref[0, :, r * AT_HD:(r + 1) * AT_HD] = _dot(p_ref[r, :, :n_keys], v_ref[0, :n_keys, :]) / l_ref[r]

        stages = (scores, softmax, values)
        for t in range(grp + len(stages) - 1):
            for st, fn in enumerate(stages):
                if 0 <= t - st < grp:
                    fn(t - st)

    @pl.when(i * tq < n_ctx)
    def _():
        attend(n_ctx)

    @pl.when(i * tq >= n_ctx)
    def _():
        attend(k_ref.shape[1])


def _attention(q, k, v, n_ctx, tq=256, kb=256, rows=32):
    bsz, L, _ = q.shape
    gw = (AT_H // AT_KV) * AT_HD
    return pl.pallas_call(
        functools.partial(_attn_kernel, tq=tq, n_ctx=n_ctx, kb=kb, rows=rows),
        scratch_shapes=[pltpu.VMEM((gw // AT_HD, tq, L), F32), pltpu.VMEM((gw // AT_HD, tq, L), BF16),
                        pltpu.VMEM((gw // AT_HD, tq, AT_HD), F32)],
        grid=(bsz, AT_KV, L // tq),
        in_specs=[pl.BlockSpec((1, tq, gw), lambda b, g, i: (b, i, g)),
                  pl.BlockSpec((1, L, AT_HD), lambda b, g, i: (b, 0, g)),
                  pl.BlockSpec((1, L, AT_HD), lambda b, g, i: (b, 0, g))],
        out_specs=pl.BlockSpec((1, tq, gw), lambda b, g, i: (b, i, g)),
        out_shape=jax.ShapeDtypeStruct((bsz, L, BR), F32),
        compiler_params=_cparams(("parallel", "parallel", "parallel")),
        name="gqa_attention",
    )(q, k, v)


def _cap_gates(pre):
    return GATE_CAP * jnp.tanh(pre / GATE_CAP)


def _log_sigmoid(x):
    return jnp.minimum(x, 0.0) - jnp.log1p(jnp.exp(-jnp.abs(x)))


def _mlstm_kernel(qf_ref, kf_ref, vf_ref, gcf_ref, qb_ref, kb_ref, vb_ref, gcb_ref,
                  bc_ref, br_ref, hf_ref, hb_ref, c_ref, n_ref, m_ref):
    j = pl.program_id(1)
    T = ML_T

    @pl.when(j == 0)
    def _():
        c_ref[...] = jnp.zeros_like(c_ref)
        n_ref[...] = jnp.zeros_like(n_ref)
        m_ref[...] = jnp.zeros_like(m_ref)

    r2 = lax.broadcasted_iota(jnp.int32, (T, T), 0)
    c2 = lax.broadcasted_iota(jnp.int32, (T, T), 1)

    refs = ((qf_ref, kf_ref, vf_ref, gcf_ref, None, hf_ref), (qb_ref, kb_ref, vb_ref, gcb_ref, None, hb_ref))
    seen = [c2 <= r2, c2 >= r2]
    seen_t = [r2 <= c2, r2 >= c2]
    gc = [_cap_gates(refs[d][3][0][:, :16] + bc_ref[...]) for d in range(2)]
    gr = [_cap_gates(refs[d][3][0].T[:16, :] + br_ref[...]) for d in range(2)]
    lsc = [_log_sigmoid(x) for x in gc]
    lsr = [_log_sigmoid(x) for x in gr]

    chains = [(d, h) for d in range(2) for h in range(ML_H)]
    each = lambda fn, *lists: [fn(*xs) for xs in zip(*lists)]
    lane16 = lax.broadcasted_iota(jnp.int32, (T, 16), 1)
    sub16 = lax.broadcasted_iota(jnp.int32, (16, T), 0)
    pick_row = lambda x, idx: jnp.sum(jnp.where(sub16 == idx, x, 0.0), axis=0, keepdims=True)
    pick_col = lambda x, idx: jnp.sum(jnp.where(lane16 == idx, x, 0.0), axis=1, keepdims=True)
    li_row = [pick_row(gr[d], d * ML_H + h) for d, h in chains]
    lf_row = [pick_row(lsr[d], (2 + d) * ML_H + h) for d, h in chains]
    li_col = [pick_col(gc[d], d * ML_H + h) for d, h in chains]
    lf_col = [pick_col(lsc[d], (2 + d) * ML_H + h) for d, h in chains]
    b_col = [jnp.sum(jnp.where(seen[d], x, 0.0), axis=1, keepdims=True) for (d, _), x in zip(chains, lf_row)]
    b_row = [jnp.sum(jnp.where(seen_t[d], x, 0.0), axis=0, keepdims=True) for (d, _), x in zip(chains, lf_col)]
    g = [jnp.sum(x, axis=0, keepdims=True) for x in lf_col]
    m_prev = [m_ref[d, h] for d, h in chains]
    q = [refs[d][0][0, :, h * ML_DK:(h + 1) * ML_DK] * (ML_DK ** -0.5) for d, h in chains]
    k = [refs[d][1][0, :, h * ML_DK:(h + 1) * ML_DK] for d, h in chains]
    vb = [refs[d][2][0, :, h * ML_DV:(h + 1) * ML_DV].astype(BF16) for d, h in chains]
    C = [c_ref[d, h] for d, h in chains]
    n = [n_ref[d, h] for d, h in chains]
    qb = [x.astype(BF16) for x in q]

    dmat = [jnp.where(seen[d], bc - br + li, -jnp.inf) for (d, _), bc, br, li in zip(chains, b_col, b_row, li_row)]
    m_inter = each(lambda bc, m: bc + m, b_col, m_prev)
    m_t = each(lambda mi, dm: jnp.maximum(mi, jnp.max(dm, axis=-1, keepdims=True)), m_inter, dmat)
    w_inter = each(lambda mi, mt: jnp.exp(mi - mt), m_inter, m_t)
    qk = each(lambda a, b: _dot_nt(a, b.astype(BF16)), qb, k)
    qc = each(lambda a, b: _dot(a, b.astype(BF16)), qb, C)
    s = each(lambda x, dm, mt: x * jnp.exp(dm - mt), qk, dmat, m_t)
    sv = each(lambda a, b: _dot(a.astype(BF16), b), s, vb)
    qn = each(lambda a, b: jnp.sum(a * b, axis=-1, keepdims=True), q, n)
    den = each(lambda w, a, x: w * a + jnp.sum(x, axis=-1, keepdims=True), w_inter, qn, s)
    for (d, h), w, a, b, dn, mt in zip(chains, w_inter, qc, sv, den, m_t):
        refs[d][5][0, :, h * ML_DV:(h + 1) * ML_DV] = (w * a + b) / jnp.maximum(jnp.abs(dn), jnp.exp(-mt))

    loga = each(lambda g_, bc, li: g_ - bc + li, g, b_col, li_col)
    m_new = each(lambda g_, m, la: jnp.maximum(g_ + m, jnp.max(la, axis=0, keepdims=True)), g, m_prev, loga)
    carry = each(lambda g_, m, mn: jnp.exp(g_ + m - mn), g, m_prev, m_new)
    wk = each(lambda la, mn, k_: jnp.exp(la - mn) * k_, loga, m_new, k)
    kv = each(lambda a, b: _dot_tn(a.astype(BF16), b), wk, vb)
    for (d, h), cr, c_, kv_, n_, wk_, mn in zip(chains, carry, C, kv, n, wk, m_new):
        c_ref[d, h] = cr * c_ + kv_
        n_ref[d, h] = cr * n_ + jnp.sum(wk_, axis=0, keepdims=True)
        m_ref[d, h] = mn


def _mlstm(proj, bias_col, bias_row, n_ctx):
    bsz, L, _ = proj.shape
    nc, nc_ctx = L // ML_T, n_ctx // ML_T
    qw, vw = ML_H * ML_DK, ML_H * ML_DV

    def dir_specs(d):
        ch = lambda j: _scan_chunk(d, j, nc_ctx, nc)
        return [pl.BlockSpec((1, ML_T, qw), lambda b, j: (b, ch(j), COL["ml_q"] // qw)),
                pl.BlockSpec((1, ML_T, qw), lambda b, j: (b, ch(j), COL["ml_k"] // qw)),
                pl.BlockSpec((1, ML_T, vw), lambda b, j: (b, ch(j), COL["ml_v"] // vw)),
                pl.BlockSpec((1, ML_T, 128), lambda b, j: (b, ch(j), COL["ml_if"] // 128))]

    def out_spec(d):
        ch = lambda j: _scan_chunk(d, j, nc_ctx, nc)
        return pl.BlockSpec((1, ML_T, vw), lambda b, j: (b, ch(j), 0))

    shape = jax.ShapeDtypeStruct((bsz, L, vw), F32)
    return pl.pallas_call(
        _mlstm_kernel,
        grid=(bsz, nc),
        in_specs=dir_specs(0) + dir_specs(1) + [pl.BlockSpec((1, 16), lambda b, j: (0, 0)),
                                                pl.BlockSpec((16, 1), lambda b, j: (0, 0))],
        out_specs=[out_spec(0), out_spec(1)],
        out_shape=[shape, shape],
        scratch_shapes=[pltpu.VMEM((2, ML_H, ML_DK, ML_DV), F32),
                        pltpu.VMEM((2, ML_H, 1, ML_DK), F32),
                        pltpu.VMEM((2, ML_H, 1, 1), F32)],
        compiler_params=_cparams(("parallel", "arbitrary")),
        name="mlstm_scan",
    )(proj, proj, proj, proj, proj, proj, proj, proj, bias_col, bias_row)


def _epilogue_kernel(yf_ref, yb_ref, bonus_ref, rwg_ref, att_ref, atg_ref, hf_ref, hb_ref, mlo_ref, mlg_ref,
                     lnw_ref, lnb_ref, mng_ref, m64_ref, b64_ref, m256_ref, b256_ref, o_ref):
    y = yf_ref[0] + yb_ref[0]
    mu = _seg_reduce(y, m64_ref[...], b64_ref[...])
    yc = y - mu
    var = _seg_reduce(yc * yc, m64_ref[...], b64_ref[...])
    ya = yc * lax.rsqrt(var + RW_GN_EPS) * lnw_ref[...] + lnb_ref[...] + bonus_ref[0]
    o_ref[0, 0] = (ya * _silu(rwg_ref[0])).astype(o_ref.dtype)

    o_ref[1, 0] = (att_ref[0] * _silu(atg_ref[0])).astype(o_ref.dtype)

    hh = hf_ref[0] + hb_ref[0]
    ms = _seg_reduce(hh * hh, m256_ref[...], b256_ref[...])
    hn = hh * lax.rsqrt(ms + EPS) * mng_ref[...]
    o_ref[2, 0] = (_sigmoid(mlo_ref[0]) * hn * _silu(mlg_ref[0])).astype(o_ref.dtype)


def _epilogue(proj, y_f, y_b, bonus, att, h_f, h_b, p, tl=256):
    bsz, L, _ = proj.shape
    tok = pl.BlockSpec((1, tl, BR), lambda b, i: (b, i, 0))
    pc = lambda name: pl.BlockSpec((1, tl, BR), lambda b, i: (b, i, COL[name] // BR))
    vec = pl.BlockSpec((1, BR), lambda b, i: (0, 0))
    gat = pl.BlockSpec((BR, 128), lambda b, i: (0, 0))
    sct = pl.BlockSpec((128, BR), lambda b, i: (0, 0))
    return pl.pallas_call(
        _epilogue_kernel,
        grid=(bsz, L // tl),
        in_specs=[tok, tok, tok, pc("rw_g"), tok, pc("at_g"), tok, tok, pc("ml_o"), pc("ml_g"),
                  vec, vec, vec, gat, sct, gat, sct],
        out_specs=pl.BlockSpec((3, 1, tl, BR), lambda b, i: (0, b, i, 0)),
        out_shape=jax.ShapeDtypeStruct((3, bsz, L, BR), BF16),
        compiler_params=_cparams(("parallel", "parallel")),
        name="branch_epilogue",
    )(y_f, y_b, bonus, proj, att, proj, h_f, h_b, proj, proj,
      p["rw_ln_w"].reshape(1, BR), p["rw_ln_b"].reshape(1, BR), p["ml_norm_g"].reshape(1, BR),
      p["mean64"], p["bcast64"], p["mean256"], p["bcast256"])


def _merge_kernel(y_ref, w_ref, g0_ref, g1_ref, g2_ref, o_ref, wb_ref):
    @pl.when(pl.program_id(1) == 0)
    def _():
        wb_ref[...] = w_ref[0].astype(BF16)

    acc = _sigmoid(g0_ref[...]) * _dot(y_ref[0], wb_ref[0])
    acc += _sigmoid(g1_ref[...]) * _dot(y_ref[1], wb_ref[1])
    acc += _sigmoid(g2_ref[...]) * _dot(y_ref[2], wb_ref[2])
    o_ref[...] = acc.astype(o_ref.dtype)


def _merge(ys, w_branch, layer, proj2d, tm=512, tn=1024):
    _, m, _ = ys.shape
    nb = D // tn
    gate = lambda n: pl.BlockSpec((tm, tn), lambda j, i: (i, n * nb + j))
    return pl.pallas_call(
        _merge_kernel,
        grid=(D // tn, m // tm),
        in_specs=[pl.BlockSpec((3, tm, BR), lambda j, i: (0, i, 0)),
                  pl.BlockSpec((1, 3, BR, tn), lambda j, i: (layer, 0, 0, j), pipeline_mode=pl.Buffered(1)),
                  gate(0), gate(1), gate(2)],
        out_specs=pl.BlockSpec((tm, tn), lambda j, i: (i, j)),
        out_shape=jax.ShapeDtypeStruct((m, D), BF16),
        scratch_shapes=[pltpu.VMEM((3, BR, tn), BF16)],
        compiler_params=_cparams(("arbitrary", "arbitrary")),
        name="branch_merge",
    )(ys, w_branch, proj2d, proj2d, proj2d)


def _outproj_kernel(*refs, nct):
    a_ref, w_ref, gt_ref, *z_refs, o_ref, wb_ref = refs

    @pl.when(jnp.logical_and(pl.program_id(1) == 0, pl.program_id(2) == 0))
    def _():
        wb_ref[...] = w_ref[0].astype(BF16)

    def body(z):
        o_ref[0] = z + gt_ref[0, 0] * _dot(a_ref[0], wb_ref[...])

    _for_stream_tile(z_refs, pl.program_id(2) < nct, body)


def _out_projection(mixed, w_out, layer, z, gate, n_ctx, tl=256, tn=1024):
    bsz, L, _ = mixed.shape
    nct = n_ctx // tl
    z_specs, z_args = _stream_specs(z, tl, nct, (1, tl, tn), lambda j, b, i: (b, i, j))
    return pl.pallas_call(
        functools.partial(_outproj_kernel, nct=nct),
        grid=(D // tn, bsz, L // tl),
        in_specs=[pl.BlockSpec((1, tl, D), lambda j, b, i: (b, i, 0)),
                  pl.BlockSpec((1, D, tn), lambda j, b, i: (layer, 0, j)),
                  pl.BlockSpec((1, 1, 1, tn), lambda j, b, i: (b, (i >= nct).astype(jnp.int32), 0, j))] + z_specs,
        out_specs=pl.BlockSpec((1, tl, tn), lambda j, b, i: (b, i, j)),
        out_shape=jax.ShapeDtypeStruct((bsz, L, D), F32),
        scratch_shapes=[pltpu.VMEM((D, tn), BF16)],
        compiler_params=_cparams(("arbitrary", "arbitrary", "arbitrary")),
        name="out_projection",
    )(mixed, w_out, gate, *z_args)


def _rope_tables(n_ctx, n_lat):
    rows = n_lat // GRID_W
    row = jnp.repeat(jnp.arange(rows), GRID_W).astype(F32)
    col = jnp.tile(jnp.arange(GRID_W), rows).astype(F32)
    inv_freq = ROPE_THETA ** (-jnp.arange(0, AT_HD // 2, 2, dtype=F32) / (AT_HD // 2))
    ang_lat = jnp.stack([row[:, None] * inv_freq, col[:, None] * inv_freq], axis=1)
    ang = jnp.concatenate([jnp.zeros((n_ctx, 2, AT_HD // 4), F32), ang_lat], axis=0)
    cos, sin = jnp.cos(ang), jnp.sin(ang)
    cos_t = jnp.concatenate([cos[:, 0], cos[:, 0], cos[:, 1], cos[:, 1]], axis=-1)
    sin_t = jnp.concatenate([-sin[:, 0], sin[:, 0], -sin[:, 1], sin[:, 1]], axis=-1)
    return cos_t, sin_t


def _group_consts(width, value):
    member = (np.arange(BR)[:, None] // width) == np.arange(128)[None, :]
    return (jnp.asarray(np.where(member, value, 0.0), dtype=BF16),
            jnp.asarray(np.where(member.T, 1.0, 0.0), dtype=BF16))


def _chunk_tri(tl):
    t = np.arange(tl)
    same = (t[:, None] // RW_T) == (t[None, :] // RW_T)
    fwd = same & (t[None, :] <= t[:, None])
    bwd = same & (t[None, :] >= t[:, None])
    return jnp.asarray(np.stack([fwd, bwd]).astype(np.float32), dtype=BF16)


def kernel(x, c, ctx, c_ctx, norm_g, w_ada, b_ada, w_in, shift_mu, rw_w_up, rw_w0, rw_a_up, rw_a0, rw_k_k, rw_k_a, rw_r_k, rw_ln_w, rw_ln_b, at_q_g, at_k_g, ml_gate_b, ml_norm_g, w_branch, w_out, final_g):
    bsz, n_lat, _ = x.shape
    n_ctx = ctx.shape[1]
    L = n_ctx + n_lat
    depth = w_in.shape[0]

    cos_t, sin_t = _rope_tables(n_ctx, n_lat)
    sum64, bcast64 = _group_consts(RW_N, 1.0)
    mean64, _ = _group_consts(RW_N, 1.0 / RW_N)
    mean256, bcast256 = _group_consts(ML_DV, 1.0 / ML_DV)
    consts = {"sum64": sum64, "bcast64": bcast64, "mean64": mean64, "mean256": mean256, "bcast256": bcast256,
              "tri": _chunk_tri(256)}

    cc = jnp.concatenate([c, c_ctx[None], jnp.zeros((8 - bsz - 1, D), F32)], axis=0)
    mod = _modulation(cc, w_ada, b_ada)

    w_in_t = jnp.swapaxes(w_in, 1, 2)
    z = (ctx, x)
    for l in range(depth):
        sh, sc, gt = mod[l, :, :D], mod[l, :, D:2 * D], mod[l, :, 2 * D:]
        pick = lambda t: jnp.stack([jnp.broadcast_to(t[bsz], (bsz, D)), t[:bsz]], axis=1)[:, :, None, :]
        scale = pick((1.0 + sc) * norm_g[l])
        shift = pick(sh)
        gate = pick(gt)

        h = _norm_mod(z, scale, shift, n_ctx, L, BF16)
        proj2d = _in_projection(h.reshape(bsz * L, D), w_in_t, l)
        proj = proj2d.reshape(bsz, L, N_PROJ)

        p = dict(consts, shift_mu=shift_mu[l], rw_w_up=rw_w_up[l], rw_w0=rw_w0[l], rw_a_up=rw_a_up[l],
                 rw_a0=rw_a0[l], rw_k_k=rw_k_k[l], rw_k_a=rw_k_a[l], rw_r_k=rw_r_k[l], rw_ln_w=rw_ln_w[l],
                 rw_ln_b=rw_ln_b[l], ml_norm_g=ml_norm_g[l])
        at, rt, bt, kt, v_rw, bonus, g_rw = _rw_prepare(proj, p, n_ctx)
        y_f, y_b = _rw_scan(at, rt, bt, kt, v_rw, g_rw, n_ctx)

        qn, kn, vn = _at_prepare(proj, cos_t, sin_t, at_q_g[l], at_k_g[l])
        att = _attention(qn, kn, vn, n_ctx)

        bias = ml_gate_b[l].reshape(16)
        h_f, h_b = _mlstm(proj, bias.reshape(1, 16), bias.reshape(16, 1), n_ctx)

        ys = _epilogue(proj, y_f, y_b, bonus, att, h_f, h_b, p)
        mixed = _merge(ys.reshape(3, bsz * L, BR), w_branch, l, proj2d)
        z = _out_projection(mixed.reshape(bsz, L, D), w_out, l, z, gate, n_ctx)

    ones = jnp.ones((bsz, 2, 1, D), F32) * final_g
    zeros = jnp.zeros((bsz, 2, 1, D), F32)
    return _norm_mod(z, ones, zeros, n_ctx, L, F32, latent_only=True)
```

```python
import functools
import math

import numpy as np
import jax
import jax.numpy as jnp
from jax import lax
from jax.experimental import pallas as pl
from jax.experimental.pallas import tpu as pltpu

F32 = jnp.float32
BF16 = jnp.bfloat16
HI = lax.Precision.HIGHEST

D = 2048
BR = 1024
EPS = 1e-6
GRID_W = 64

RW_H, RW_N, RW_LORA = 16, 64, 64
RW_GN_EPS = 64e-5
RW_T = 64
RW_INV_BASE = 8

AT_H, AT_KV, AT_HD = 8, 2, 128
ROPE_THETA = 10000.0

ML_H, ML_DK, ML_DV, ML_T = 4, 128, 256, 128
GATE_CAP = 15.0

D_IN = 17168
LANES, SUBLANES = 128, 8
PROJ_TN = 512
_PROJ_GROUPS = (
    ("merge", 11024, 6144), ("r", 0, 1024), ("k", 1024, 1024), ("v", 2048, 1024),
    ("rw_g", 3328, 1024), ("at_q", 4352, 1024), ("at_g", 5888, 1024), ("ml_v", 7936, 1024),
    ("ml_o", 8960, 1024), ("ml_g", 10000, 1024), ("ml_q", 6912, 512), ("ml_k", 7424, 512),
    ("at_k", 5376, 512), ("wdad", 3072, 512), ("ml_if", 9984, 512),
)
COL = {}
PROJ_SRC = []
for _name, _start, _width in _PROJ_GROUPS:
    COL[_name] = len(PROJ_SRC) * PROJ_TN
    PROJ_SRC += [_start + t * PROJ_TN for t in range(_width // PROJ_TN)]
COL["at_v"] = COL["at_k"] + 256
N_PROJ = len(PROJ_SRC) * PROJ_TN
for _name, _blk in (("merge", 2048), ("r", 1024), ("k", 1024), ("v", 1024), ("rw_g", 1024), ("at_q", 1024),
                    ("at_g", 1024), ("ml_v", 1024), ("ml_o", 1024), ("ml_g", 1024), ("ml_q", 512), ("ml_k", 512),
                    ("at_k", 256), ("at_v", 256), ("wdad", 256), ("ml_if", 128)):
    assert COL[_name] % _blk == 0, _name

VMEM_LIMIT = 48 * 1024 * 1024


def _cparams(sem):
    return pltpu.CompilerParams(dimension_semantics=sem, vmem_limit_bytes=VMEM_LIMIT)


def _dot(a, b, **kw):
    return jnp.dot(a, b, preferred_element_type=F32, **kw)


def _dot_nt(a, b, **kw):
    return lax.dot_general(a, b, (((1,), (1,)), ((), ())), preferred_element_type=F32, **kw)


def _dot_tn(a, b, **kw):
    return lax.dot_general(a, b, (((0,), (0,)), ((), ())), preferred_element_type=F32, **kw)


def _split3(x):
    h = x.astype(BF16)
    r = x - h.astype(F32)
    m = r.astype(BF16)
    l = (r - m.astype(F32)).astype(BF16)
    return h, m, l


def _split2(x):
    h = x.astype(BF16)
    return h, (x - h.astype(F32)).astype(BF16)


def _seg_reduce(x, gather, scatter):
    sh, sl = _split2(_dot(x.astype(BF16), gather))
    return _dot(sh, scatter) + _dot(sl, scatter)


def _dot_const_lhs(c, x):
    h, m, l = _split3(x)
    return _dot(c, h) + _dot(c, m) + _dot(c, l)


def _sigmoid(x):
    return 0.5 * jnp.tanh(0.5 * x) + 0.5


def _silu(x):
    return x * _sigmoid(x)


def _mod_kernel(c_ref, w_ref, b_ref, o_ref):
    @pl.when(pl.program_id(1) == 0)
    def _():
        o_ref[0] = jnp.broadcast_to(b_ref[0], o_ref.shape[1:])

    xh, xl = _split2(_silu(c_ref[...]))
    wh, wl = _split2(w_ref[0])
    o_ref[0] += _dot(xh, wh) + _dot(xh, wl) + _dot(xl, wh)


def _modulation(cc, w_ada, b_ada, tk=256):
    depth = w_ada.shape[0]
    return pl.pallas_call(
        _mod_kernel,
        grid=(depth, D // tk),
        in_specs=[pl.BlockSpec((8, tk), lambda l, k: (0, k)),
                  pl.BlockSpec((1, tk, 3 * D), lambda l, k: (l, k, 0)),
                  pl.BlockSpec((1, 1, 3 * D), lambda l, k: (l, 0, 0))],
        out_specs=pl.BlockSpec((1, 8, 3 * D), lambda l, k: (l, 0, 0)),
        out_shape=jax.ShapeDtypeStruct((depth, 8, 3 * D), F32),
        compiler_params=_cparams(("parallel", "arbitrary")),
        name="adaln_modulation",
    )(cc, w_ada, b_ada.reshape(depth, 1, 3 * D))


def _stream_specs(z, tl, nct, block, index):
    if not isinstance(z, tuple):
        return [pl.BlockSpec(block, lambda *g: index(*g))], [z]

    def ctx_index(*g):
        b, i, j = index(*g)
        return b, jnp.minimum(i, nct - 1), j

    def lat_index(*g):
        b, i, j = index(*g)
        return b, jnp.maximum(i - nct, 0), j

    return [pl.BlockSpec(block, ctx_index), pl.BlockSpec(block, lat_index)], list(z)


def _for_stream_tile(z_refs, is_ctx, body):
    if len(z_refs) == 1:
        body(z_refs[0][0])
        return
    pl.when(is_ctx)(lambda: body(z_refs[0][0]))
    pl.when(jnp.logical_not(is_ctx))(lambda: body(z_refs[1][0]))


def _norm_kernel(*refs, nct, first):
    *z_refs, sc_ref, sh_ref, o_ref = refs

    def body(x):
        y = x * lax.rsqrt(jnp.mean(x * x, axis=-1, keepdims=True) + EPS)
        o_ref[0] = (y * sc_ref[0, 0] + sh_ref[0, 0]).astype(o_ref.dtype)

    _for_stream_tile(z_refs, pl.program_id(1) + first < nct, body)


def _norm_mod(z, scale, shift, n_ctx, L, out_dtype, latent_only=False, tl=256):
    bsz = scale.shape[0]
    nct = n_ctx // tl
    first = nct if latent_only else 0
    seg = lambda b, i: (b, (i + first >= nct).astype(jnp.int32), 0, 0)
    z_specs, z_args = _stream_specs(z, tl, nct, (1, tl, D), lambda b, i: (b, i + first, 0))
    return pl.pallas_call(
        functools.partial(_norm_kernel, nct=nct, first=first),
        grid=(bsz, L // tl - first),
        in_specs=z_specs + [pl.BlockSpec((1, 1, 1, D), seg), pl.BlockSpec((1, 1, 1, D), seg)],
        out_specs=pl.BlockSpec((1, tl, D), lambda b, i: (b, i, 0)),
        out_shape=jax.ShapeDtypeStruct((bsz, L - first * tl, D), out_dtype),
        compiler_params=_cparams(("parallel", "parallel")),
        name="rmsnorm_modulate",
    )(*z_args, scale, shift)


def _inproj_kernel(start_ref, a_ref, wt_ref, o_ref):
    o_ref[...] = _dot_nt(a_ref[...], wt_ref[0].astype(BF16))


def _in_projection(a, w_t, layer, tm=2176):
    m, k = a.shape
    tm = math.gcd(m, tm)
    assert all(s % SUBLANES == 0 for s in PROJ_SRC)
    start = np.array(PROJ_SRC, np.int32) // SUBLANES
    grid_spec = pltpu.PrefetchScalarGridSpec(
        num_scalar_prefetch=1,
        grid=(m // tm, len(PROJ_SRC)),
        in_specs=[pl.BlockSpec((tm, k), lambda i, j, st: (i, 0)),
                  pl.BlockSpec((pl.Element(1), pl.Element(PROJ_TN), pl.Element(k)),
                               lambda i, j, st: (layer, st[j] * SUBLANES, 0))],
        out_specs=pl.BlockSpec((tm, PROJ_TN), lambda i, j, st: (i, j)),
    )
    return pl.pallas_call(
        _inproj_kernel,
        grid_spec=grid_spec,
        out_shape=jax.ShapeDtypeStruct((m, N_PROJ), F32),
        compiler_params=_cparams(("parallel", "arbitrary")),
        name="in_projection",
    )(jnp.asarray(start), a, w_t)


def _rwprep_kernel(r_ref, rp_ref, rn_ref, k_ref, kp_ref, kn_ref, v_ref, vp_ref, vn_ref,
                   w_ref, wp_ref, wn_ref, mur_ref, muk_ref, muv_ref, muw_ref,
                   wup_ref, w0_ref, aup_ref, a0_ref, kk_ref, ka_ref, rk_ref, gat_ref, sct_ref, tri_ref,
                   at_ref, rt_ref, bt_ref, kt_ref, vo_ref, bonus_ref, g_ref, *, tl, n_ctx, L):
    i = pl.program_id(1)
    start = i * tl
    has_prev = jnp.logical_and(start != 0, start != n_ctx).astype(F32)
    has_next = jnp.logical_and(start + tl != n_ctx, start + tl != L).astype(F32)

    def shift(x_ref, p_ref, n_ref, mu_ref):
        x = x_ref[0]
        row = lax.broadcasted_iota(jnp.int32, x.shape, 0)
        prev = jnp.where(row == 0, p_ref[0, 7:8, :] * has_prev, pltpu.roll(x, 1, axis=0))
        nxt = jnp.where(row == tl - 1, n_ref[0, 0:1, :] * has_next, pltpu.roll(x, tl - 1, axis=0))
        mu0, mu1 = mu_ref[0:1, :], mu_ref[1:2, :]
        return x * (1.0 - mu0 - mu1) + mu0 * prev + mu1 * nxt

    r = shift(r_ref, rp_ref, rn_ref, mur_ref)
    k = shift(k_ref, kp_ref, kn_ref, muk_ref)
    v = shift(v_ref, vp_ref, vn_ref, muv_ref)
    wdad = shift(w_ref, wp_ref, wn_ref, muw_ref)
    gat, sct = gat_ref[...], sct_ref[...]

    kk = k * kk_ref[...]
    kk = kk * lax.rsqrt(jnp.maximum(_seg_reduce(kk * kk, gat, sct), 1e-24))
    bonus_ref[0] = _seg_reduce(r * k * rk_ref[...], gat, sct) * v
    vo_ref[0] = v.astype(vo_ref.dtype)

    for d in range(2):
        wd = wdad[:, d * RW_LORA:(d + 1) * RW_LORA]
        ad = wdad[:, 2 * RW_LORA + d * RW_LORA:2 * RW_LORA + (d + 1) * RW_LORA]
        u = w0_ref[d:d + 1, :] + _mmx(jnp.tanh(wd), wup_ref[d], "x3")
        logw = -math.exp(-0.5) * _sigmoid(u)
        a = _sigmoid(a0_ref[d:d + 1, :] + _mmx(ad, aup_ref[d], "x3"))
        kd = k * (1.0 + (a - 1.0) * ka_ref[...])
        lg = _dot_const_lhs(tri_ref[d], logw)
        e_pos = jnp.exp(lg)
        e_neg = 1.0 / e_pos
        at_ref[0, d] = (-kk * jnp.exp(lg - logw)).astype(at_ref.dtype)
        rt_ref[0, d] = (r * e_pos).astype(rt_ref.dtype)
        bt_ref[0, d] = (kk * a * e_neg).astype(bt_ref.dtype)
        kt_ref[0, d] = (kd * e_neg).astype(kt_ref.dtype)
        for c in range(tl // RW_T):
            g_ref[0, d, c] = jnp.exp(jnp.sum(logw[c * RW_T:(c + 1) * RW_T], axis=0, keepdims=True))


def _rw_prepare(proj, p, n_ctx, tl=256):
    bsz, L, _ = proj.shape
    h8 = tl // 8
    nblk8 = L // 8
    main = lambda w, cb: pl.BlockSpec((1, tl, w), lambda b, i: (b, i, cb))
    prev = lambda w, cb: pl.BlockSpec((1, 8, w), lambda b, i: (b, jnp.maximum(i * h8 - 1, 0), cb))
    nxt = lambda w, cb: pl.BlockSpec((1, 8, w), lambda b, i: (b, jnp.minimum((i + 1) * h8, nblk8 - 1), cb))
    full = lambda shape: pl.BlockSpec(shape, lambda b, i: (0,) * len(shape))
    in_specs = []
    args = []
    for name, w in (("r", 1024), ("k", 1024), ("v", 1024), ("wdad", 256)):
        cb = COL[name] // w
        in_specs += [main(w, cb), prev(w, cb), nxt(w, cb)]
        args += [proj, proj, proj]
    mu = p["shift_mu"]
    in_specs += [pl.BlockSpec((2, 1024), lambda b, i: (0, 0)), pl.BlockSpec((2, 1024), lambda b, i: (0, 1)),
                 pl.BlockSpec((2, 1024), lambda b, i: (0, 2)), pl.BlockSpec((2, 256), lambda b, i: (0, 12))]
    args += [mu, mu, mu, mu]
    in_specs += [full((2, RW_LORA, BR)), full((2, BR)), full((2, RW_LORA, BR)), full((2, BR)),
                 full((1, BR)), full((1, BR)), full((1, BR)), full((BR, 128)), full((128, BR)), full((2, tl, tl))]
    args += [p["rw_w_up"], p["rw_w0"], p["rw_a_up"], p["rw_a0"], p["rw_k_k"].reshape(1, BR),
             p["rw_k_a"].reshape(1, BR), p["rw_r_k"].reshape(1, BR), p["sum64"], p["bcast64"], p["tri"]]
    dir_spec = pl.BlockSpec((1, 2, tl, BR), lambda b, i: (b, 0, i, 0))
    tok_spec = pl.BlockSpec((1, tl, BR), lambda b, i: (b, i, 0))
    nchunk = tl // RW_T
    out_specs = [dir_spec, dir_spec, dir_spec, dir_spec, tok_spec, tok_spec,
                 pl.BlockSpec((1, 2, nchunk, 1, BR), lambda b, i: (b, 0, i, 0, 0))]
    dir_shape = jax.ShapeDtypeStruct((bsz, 2, L, BR), BF16)
    out_shape = [dir_shape, dir_shape, dir_shape, dir_shape,
                 jax.ShapeDtypeStruct((bsz, L, BR), BF16), jax.ShapeDtypeStruct((bsz, L, BR), F32),
                 jax.ShapeDtypeStruct((bsz, 2, L // RW_T, 1, BR), F32)]
    return pl.pallas_call(
        functools.partial(_rwprep_kernel, tl=tl, n_ctx=n_ctx, L=L),
        grid=(bsz, L // tl),
        in_specs=in_specs, out_specs=out_specs, out_shape=out_shape,
        compiler_params=_cparams(("parallel", "parallel")),
        name="rwkv_prepare",
    )(*args)


def _scan_chunk(d, j, nc_ctx, nc):
    bwd = jnp.where(j < nc_ctx, nc_ctx - 1 - j, nc - 1 - j + nc_ctx)
    return jnp.where(d == 0, j, bwd)


def _mmx(a, b, mode, dims="nn"):
    f = {"nn": _dot, "nt": _dot_nt, "tn": _dot_tn}[dims]
    ah, bh = a.astype(BF16), b.astype(BF16)
    if mode == "bf16":
        return f(ah, bh)
    al = (a - ah.astype(F32)).astype(BF16)
    bl = (b - bh.astype(F32)).astype(BF16)
    return f(ah, bh) + f(ah, bl) + f(al, bh)


def _rwscan_kernel(atf_ref, rtf_ref, btf_ref, ktf_ref, vf_ref, gf_ref,
                   atb_ref, rtb_ref, btb_ref, ktb_ref, vb_ref, gb_ref, yf_ref, yb_ref, s_ref):
    j = pl.program_id(1)
    T = RW_T

    @pl.when(j == 0)
    def _():
        s_ref[...] = jnp.zeros_like(s_ref)

    tok = lax.broadcasted_iota(jnp.int32, (T, 128), 0)
    col = lax.broadcasted_iota(jnp.int32, (T, 128), 1) % RW_N
    strict = [col < tok, col > tok]
    incl = [col <= tok, col >= tok]
    eye = jnp.where(col == tok, 1.0, 0.0)
    base_blk = (tok // RW_INV_BASE) == (col // RW_INV_BASE)
    merge_blks = []
    s = RW_INV_BASE
    while s < T:
        merge_blks.append(jnp.logical_and((tok // (2 * s)) == (col // (2 * s)), (tok // s) != (col // s)))
        s *= 2
    row2 = lax.broadcasted_iota(jnp.int32, (2 * T, 128), 0)
    lane2 = lax.broadcasted_iota(jnp.int32, (2 * T, 128), 1)
    same_head = (lane2 // RW_N) == (row2 // T)

    def stack(x):
        x = x.astype(BF16)
        return jnp.where(same_head, jnp.concatenate([x, x], axis=0), 0.0)

    cat = lambda xs, axis=0: jnp.concatenate(xs, axis=axis)
    mm = lambda a, b: _dot(a.astype(BF16), b)

    refs = ((atf_ref, rtf_ref, btf_ref, ktf_ref, vf_ref, gf_ref, yf_ref),
            (atb_ref, rtb_ref, btb_ref, ktb_ref, vb_ref, gb_ref, yb_ref))
    chains = [(d, p) for d in range(2) for p in range(BR // 128)]
    dirs = [d for d, _ in chains]
    sls = [slice(p * 128, (p + 1) * 128) for _, p in chains]
    each = lambda fn, *lists: [fn(*xs) for xs in zip(*lists)]
    At = [refs[d][0][0, 0, :, sl] for d, sl in zip(dirs, sls)]
    Rt = [refs[d][1][0, 0, :, sl] for d, sl in zip(dirs, sls)]
    Bt = [refs[d][2][0, 0, :, sl] for d, sl in zip(dirs, sls)]
    Kt = [refs[d][3][0, 0, :, sl] for d, sl in zip(dirs, sls)]
    V = [refs[d][4][0, :, sl] for d, sl in zip(dirs, sls)]
    AR = each(lambda a, r: cat([a, r]), At, Rt)
    BKs = each(lambda b, k: cat([stack(b), stack(k)]), Bt, Kt)
    Vs = [stack(v) for v in V]
    sc = each(_dot_nt, AR, BKs)
    Aab = [jnp.where(strict[d], x[:T, :128], 0.0) for d, x in zip(dirs, sc)]
    Aak = [jnp.where(strict[d], x[:T, 128:], 0.0) for d, x in zip(dirs, sc)]
    Mrbk = [cat([jnp.where(incl[d], x[T:, :128], 0.0), jnp.where(incl[d], x[T:, 128:], 0.0)], axis=1)
            for d, x in zip(dirs, sc)]

    N = [jnp.where(base_blk, x, 0.0) for x in Aab]
    P = [eye + x for x in N]
    N = each(lambda n: mm(n, stack(n)), N)
    NP = each(lambda n, q: mm(cat([n, q]), stack(n)), N, P)
    P = each(lambda q, x: q + x[T:], P, NP)
    P = each(lambda q, x: q + mm(q, stack(x[:T])), P, NP)
    for off_blk in merge_blks:
        CP = each(lambda a, q: mm(jnp.where(off_blk, a, 0.0), stack(q)), Aab, P)
        P = each(lambda q, x: q + mm(q, stack(x)), P, CP)

    AV = each(mm, Aak, Vs)
    S = [s_ref[d, p] for d, p in chains]
    XS = each(lambda a, s_: _dot_nt(a, s_.astype(BF16)), AR, S)
    U = each(lambda q, x, w: mm(q, stack(x[:T] + w)).astype(BF16), P, XS, AV)
    Y = each(lambda x, m, u, v: x[T:] + mm(m, cat([stack(u), v])), XS, Mrbk, U, Vs)
    dS = each(lambda u, v, b, k: _dot_tn(cat([u, v]), cat([b, k])), U, V, Bt, Kt)
    same_head_sq = (lane2 // RW_N) == (row2 // RW_N)
    for c, (d, p) in enumerate(chains):
        refs[d][6][0, :, sls[c]] = Y[c]
        s_ref[d, p] = (S[c] + jnp.where(same_head_sq, dS[c], 0.0)) * refs[d][5][0, 0, 0, :, sls[c]]


def _rw_scan(at, rt, bt, kt, v, g, n_ctx):
    bsz, _, L, _ = at.shape
    nc, nc_ctx = L // RW_T, n_ctx // RW_T

    def specs(d):
        ch = lambda j: _scan_chunk(d, j, nc_ctx, nc)
        dspec = pl.BlockSpec((1, 1, RW_T, BR), lambda b, j: (b, d, ch(j), 0))
        tspec = pl.BlockSpec((1, RW_T, BR), lambda b, j: (b, ch(j), 0))
        return tspec, [dspec, dspec, dspec, dspec, tspec,
                       pl.BlockSpec((1, 1, 1, 1, BR), lambda b, j: (b, d, ch(j), 0, 0))]

    (out_f, in_f), (out_b, in_b) = specs(0), specs(1)
    shape = jax.ShapeDtypeStruct((bsz, L, BR), F32)
    return pl.pallas_call(
        _rwscan_kernel,
        grid=(bsz, nc),
        in_specs=in_f + in_b,
        out_specs=[out_f, out_b],
        out_shape=[shape, shape],
        scratch_shapes=[pltpu.VMEM((2, BR // 128, 128, 128), F32)],
        compiler_params=_cparams(("parallel", "arbitrary")),
        name="rwkv_scan",
    )(at, rt, bt, kt, v, g, at, rt, bt, kt, v, g)


def _atprep_kernel(q_ref, k_ref, v_ref, cos_ref, sin_ref, qg_ref, kg_ref, qo_ref, ko_ref, vo_ref):
    cos = cos_ref[...]
    sin = sin_ref[...]
    lane = lax.broadcasted_iota(jnp.int32, cos.shape, 1)
    first_half = (lane % 64) < 32

    q_scale = AT_HD ** -0.5 * math.log2(math.e)
    heads = [(q_ref, qo_ref, h, qg_ref[...] * q_scale) for h in range(AT_H)]
    heads += [(k_ref, ko_ref, h, kg_ref[...]) for h in range(AT_KV)]
    x = [src[0, :, h * AT_HD:(h + 1) * AT_HD] for src, _, h, _ in heads]
    y = [xi * lax.rsqrt(jnp.mean(xi * xi, axis=-1, keepdims=True) + EPS) * g for xi, (_, _, _, g) in zip(x, heads)]
    up = [pltpu.roll(yi, 96, axis=1) for yi in y]
    down = [pltpu.roll(yi, 32, axis=1) for yi in y]
    for (_, dst, h, _), yi, u, dn in zip(heads, y, up, down):
        dst[0, :, h * AT_HD:(h + 1) * AT_HD] = (yi * cos + jnp.where(first_half, u, dn) * sin).astype(dst.dtype)
    vo_ref[0] = v_ref[0].astype(vo_ref.dtype)


def _at_prepare(proj, cos, sin, q_g, k_g, tl=256):
    bsz, L, _ = proj.shape
    kvw = AT_KV * AT_HD
    tok = lambda w: pl.BlockSpec((1, tl, w), lambda b, i: (b, i, 0))
    return pl.pallas_call(
        _atprep_kernel,
        grid=(bsz, L // tl),
        in_specs=[pl.BlockSpec((1, tl, BR), lambda b, i: (b, i, COL["at_q"] // BR)),
                  pl.BlockSpec((1, tl, kvw), lambda b, i: (b, i, COL["at_k"] // kvw)),
                  pl.BlockSpec((1, tl, kvw), lambda b, i: (b, i, COL["at_v"] // kvw)),
                  pl.BlockSpec((tl, AT_HD), lambda b, i: (i, 0)),
                  pl.BlockSpec((tl, AT_HD), lambda b, i: (i, 0)),
                  pl.BlockSpec((1, AT_HD), lambda b, i: (0, 0)),
                  pl.BlockSpec((1, AT_HD), lambda b, i: (0, 0))],
        out_specs=[tok(BR), tok(kvw), tok(kvw)],
        out_shape=[jax.ShapeDtypeStruct((bsz, L, BR), BF16),
                   jax.ShapeDtypeStruct((bsz, L, kvw), BF16),
                   jax.ShapeDtypeStruct((bsz, L, kvw), BF16)],
        compiler_params=_cparams(("parallel", "parallel")),
        name="gqa_prepare",
    )(proj, proj, proj, cos, sin, q_g.reshape(1, AT_HD), k_g.reshape(1, AT_HD))


def _attn_kernel(q_ref, k_ref, v_ref, gate_ref, o_ref, s_ref, p_ref, l_ref, *, tq, n_ctx, kb, rows):
    i = pl.program_id(2)
    grp = AT_H // AT_KV
    def attend(n_keys):
        chunks = [slice(c * kb, (c + 1) * kb) for c in range(n_keys // kb)]

        def scores(r):
            s_ref[r, :, :n_keys] = _dot_nt(q_ref[0, :, r * AT_HD:(r + 1) * AT_HD], k_ref[0, :n_keys, :])

        def softmax(r):
            for rb in range(tq // rows):
                rs = slice(rb * rows, (rb + 1) * rows)
                mx = s_ref[r, rs, chunks[0]]
                for ch in chunks[1:]:
                    mx = jnp.maximum(mx, s_ref[r, rs, ch])
                m = jnp.broadcast_to(jnp.max(mx, axis=-1, keepdims=True), mx.shape)
                tot = jnp.zeros_like(mx)
                for ch in chunks:
                    p = jnp.exp2(s_ref[r, rs, ch] - m)
                    tot = tot + p
                    p_ref[r, rs, ch] = p.astype(BF16)
                l_ref[r, rs, :] = jnp.broadcast_to(jnp.sum(tot, axis=-1, keepdims=True), (rows, AT_HD))

        def values(r):
            hs = slice(r * AT_HD, (r + 1) * AT_HD)
            att = _dot(p_ref[r, :, :n_keys], v_ref[0, :n_keys, :]) / l_ref[r]
            o_ref[0, :, hs] = (att * _silu(gate_ref[0, :, hs])).astype(o_ref.dtype)

        stages = (scores, softmax, values)
        for t in range(grp + len(stages) - 1):
            for st, fn in enumerate(stages):
                if 0 <= t - st < grp:
                    fn(t - st)

    @pl.when(i * tq < n_ctx)
    def _():
        attend(n_ctx)

    @pl.when(i * tq >= n_ctx)
    def _():
        attend(k_ref.shape[1])


def _attention(q, k, v, proj, n_ctx, tq=256, kb=256, rows=32):
    bsz, L, _ = q.shape
    gw = (AT_H // AT_KV) * AT_HD
    return pl.pallas_call(
        functools.partial(_attn_kernel, tq=tq, n_ctx=n_ctx, kb=kb, rows=rows),
        scratch_shapes=[pltpu.VMEM((gw // AT_HD, tq, L), F32), pltpu.VMEM((gw // AT_HD, tq, L), BF16),
                        pltpu.VMEM((gw // AT_HD, tq, AT_HD), F32)],
        grid=(bsz, AT_KV, L // tq),
        in_specs=[pl.BlockSpec((1, tq, gw), lambda b, g, i: (b, i, g)),
                  pl.BlockSpec((1, L, AT_HD), lambda b, g, i: (b, 0, g)),
                  pl.BlockSpec((1, L, AT_HD), lambda b, g, i: (b, 0, g)),
                  pl.BlockSpec((1, tq, gw), lambda b, g, i: (b, i, COL["at_g"] // gw + g))],
        out_specs=pl.BlockSpec((1, tq, gw), lambda b, g, i: (b, i, g)),
        out_shape=jax.ShapeDtypeStruct((bsz, L, BR), BF16),
        compiler_params=_cparams(("parallel", "parallel", "parallel")),
        name="gqa_attention",
    )(q, k, v, proj)


def _cap_gates(pre):
    return GATE_CAP * jnp.tanh(pre / GATE_CAP)


def _log_sigmoid(x):
    return jnp.minimum(x, 0.0) - jnp.log1p(jnp.exp(-jnp.abs(x)))


def _mlstm_kernel(qf_ref, kf_ref, vf_ref, gcf_ref, qb_ref, kb_ref, vb_ref, gcb_ref,
                  bc_ref, br_ref, hf_ref, hb_ref, c_ref, n_ref, m_ref):
    j = pl.program_id(1)
    T = ML_T

    @pl.when(j == 0)
    def _():
        c_ref[...] = jnp.zeros_like(c_ref)
        n_ref[...] = jnp.zeros_like(n_ref)
        m_ref[...] = jnp.zeros_like(m_ref)

    r2 = lax.broadcasted_iota(jnp.int32, (T, T), 0)
    c2 = lax.broadcasted_iota(jnp.int32, (T, T), 1)

    refs = ((qf_ref, kf_ref, vf_ref, gcf_ref, None, hf_ref), (qb_ref, kb_ref, vb_ref, gcb_ref, None, hb_ref))
    seen = [c2 <= r2, c2 >= r2]
    seen_t = [r2 <= c2, r2 >= c2]
    gc = [_cap_gates(refs[d][3][0][:, :16] + bc_ref[...]) for d in range(2)]
    gr = [_cap_gates(refs[d][3][0].T[:16, :] + br_ref[...]) for d in range(2)]
    lsc = [_log_sigmoid(x) for x in gc]
    lsr = [_log_sigmoid(x) for x in gr]

    chains = [(d, h) for d in range(2) for h in range(ML_H)]
    each = lambda fn, *lists: [fn(*xs) for xs in zip(*lists)]
    lane16 = lax.broadcasted_iota(jnp.int32, (T, 16), 1)
    sub16 = lax.broadcasted_iota(jnp.int32, (16, T), 0)
    pick_row = lambda x, idx: jnp.sum(jnp.where(sub16 == idx, x, 0.0), axis=0, keepdims=True)
    pick_col = lambda x, idx: jnp.sum(jnp.where(lane16 == idx, x, 0.0), axis=1, keepdims=True)
    li_row = [pick_row(gr[d], d * ML_H + h) for d, h in chains]
    lf_row = [pick_row(lsr[d], (2 + d) * ML_H + h) for d, h in chains]
    li_col = [pick_col(gc[d], d * ML_H + h) for d, h in chains]
    lf_col = [pick_col(lsc[d], (2 + d) * ML_H + h) for d, h in chains]
    b_col = [jnp.sum(jnp.where(seen[d], x, 0.0), axis=1, keepdims=True) for (d, _), x in zip(chains, lf_row)]
    b_row = [jnp.sum(jnp.where(seen_t[d], x, 0.0), axis=0, keepdims=True) for (d, _), x in zip(chains, lf_col)]
    g = [jnp.sum(x, axis=0, keepdims=True) for x in lf_col]
    m_prev = [m_ref[d, h] for d, h in chains]
    q = [refs[d][0][0, :, h * ML_DK:(h + 1) * ML_DK] * (ML_DK ** -0.5) for d, h in chains]
    k = [refs[d][1][0, :, h * ML_DK:(h + 1) * ML_DK] for d, h in chains]
    vb = [refs[d][2][0, :, h * ML_DV:(h + 1) * ML_DV].astype(BF16) for d, h in chains]
    C = [c_ref[d, h] for d, h in chains]
    n = [n_ref[d, h] for d, h in chains]
    qb = [x.astype(BF16) for x in q]

    dmat = [jnp.where(seen[d], bc - br + li, -jnp.inf) for (d, _), bc, br, li in zip(chains, b_col, b_row, li_row)]
    m_inter = each(lambda bc, m: bc + m, b_col, m_prev)
    m_t = each(lambda mi, dm: jnp.maximum(mi, jnp.max(dm, axis=-1, keepdims=True)), m_inter, dmat)
    w_inter = each(lambda mi, mt: jnp.exp(mi - mt), m_inter, m_t)
    qk = each(lambda a, b: _dot_nt(a, b.astype(BF16)), qb, k)
    qc = each(lambda a, b: _dot(a, b.astype(BF16)), qb, C)
    s = each(lambda x, dm, mt: x * jnp.exp(dm - mt), qk, dmat, m_t)
    sv = each(lambda a, b: _dot(a.astype(BF16), b), s, vb)
    qn = each(lambda a, b: jnp.sum(a * b, axis=-1, keepdims=True), q, n)
    den = each(lambda w, a, x: w * a + jnp.sum(x, axis=-1, keepdims=True), w_inter, qn, s)
    for (d, h), w, a, b, dn, mt in zip(chains, w_inter, qc, sv, den, m_t):
        refs[d][5][0, :, h * ML_DV:(h + 1) * ML_DV] = (w * a + b) / jnp.maximum(jnp.abs(dn), jnp.exp(-mt))

    loga = each(lambda g_, bc, li: g_ - bc + li, g, b_col, li_col)
    m_new = each(lambda g_, m, la: jnp.maximum(g_ + m, jnp.max(la, axis=0, keepdims=True)), g, m_prev, loga)
    carry = each(lambda g_, m, mn: jnp.exp(g_ + m - mn), g, m_prev, m_new)
    wk = each(lambda la, mn, k_: jnp.exp(la - mn) * k_, loga, m_new, k)
    kv = each(lambda a, b: _dot_tn(a.astype(BF16), b), wk, vb)
    for (d, h), cr, c_, kv_, n_, wk_, mn in zip(chains, carry, C, kv, n, wk, m_new):
        c_ref[d, h] = cr * c_ + kv_
        n_ref[d, h] = cr * n_ + jnp.sum(wk_, axis=0, keepdims=True)
        m_ref[d, h] = mn


def _mlstm(proj, bias_col, bias_row, n_ctx):
    bsz, L, _ = proj.shape
    nc, nc_ctx = L // ML_T, n_ctx // ML_T
    qw, vw = ML_H * ML_DK, ML_H * ML_DV

    def dir_specs(d):
        ch = lambda j: _scan_chunk(d, j, nc_ctx, nc)
        return [pl.BlockSpec((1, ML_T, qw), lambda b, j: (b, ch(j), COL["ml_q"] // qw)),
                pl.BlockSpec((1, ML_T, qw), lambda b, j: (b, ch(j), COL["ml_k"] // qw)),
                pl.BlockSpec((1, ML_T, vw), lambda b, j: (b, ch(j), COL["ml_v"] // vw)),
                pl.BlockSpec((1, ML_T, 128), lambda b, j: (b, ch(j), COL["ml_if"] // 128))]

    def out_spec(d):
        ch = lambda j: _scan_chunk(d, j, nc_ctx, nc)
        return pl.BlockSpec((1, ML_T, vw), lambda b, j: (b, ch(j), 0))

    shape = jax.ShapeDtypeStruct((bsz, L, vw), F32)
    return pl.pallas_call(
        _mlstm_kernel,
        grid=(bsz, nc),
        in_specs=dir_specs(0) + dir_specs(1) + [pl.BlockSpec((1, 16), lambda b, j: (0, 0)),
                                                pl.BlockSpec((16, 1), lambda b, j: (0, 0))],
        out_specs=[out_spec(0), out_spec(1)],
        out_shape=[shape, shape],
        scratch_shapes=[pltpu.VMEM((2, ML_H, ML_DK, ML_DV), F32),
                        pltpu.VMEM((2, ML_H, 1, ML_DK), F32),
                        pltpu.VMEM((2, ML_H, 1, 1), F32)],
        compiler_params=_cparams(("parallel", "arbitrary")),
        name="mlstm_scan",
    )(proj, proj, proj, proj, proj, proj, proj, proj, bias_col, bias_row)


def _epilogue_kernel(yf_ref, yb_ref, bonus_ref, rwg_ref, hf_ref, hb_ref, mlo_ref, mlg_ref,
                     lnw_ref, lnb_ref, mng_ref, m64_ref, b64_ref, m256_ref, b256_ref, o_ref):
    y = yf_ref[0] + yb_ref[0]
    mu = _seg_reduce(y, m64_ref[...], b64_ref[...])
    yc = y - mu
    var = _seg_reduce(yc * yc, m64_ref[...], b64_ref[...])
    ya = yc * lax.rsqrt(var + RW_GN_EPS) * lnw_ref[...] + lnb_ref[...] + bonus_ref[0]
    o_ref[0, 0] = (ya * _silu(rwg_ref[0])).astype(o_ref.dtype)

    hh = hf_ref[0] + hb_ref[0]
    ms = _seg_reduce(hh * hh, m256_ref[...], b256_ref[...])
    hn = hh * lax.rsqrt(ms + EPS) * mng_ref[...]
    o_ref[1, 0] = (_sigmoid(mlo_ref[0]) * hn * _silu(mlg_ref[0])).astype(o_ref.dtype)


def _epilogue(proj, y_f, y_b, bonus, h_f, h_b, p, tl=256):
    bsz, L, _ = proj.shape
    tok = pl.BlockSpec((1, tl, BR), lambda b, i: (b, i, 0))
    pc = lambda name: pl.BlockSpec((1, tl, BR), lambda b, i: (b, i, COL[name] // BR))
    vec = pl.BlockSpec((1, BR), lambda b, i: (0, 0))
    gat = pl.BlockSpec((BR, 128), lambda b, i: (0, 0))
    sct = pl.BlockSpec((128, BR), lambda b, i: (0, 0))
    return pl.pallas_call(
        _epilogue_kernel,
        grid=(bsz, L // tl),
        in_specs=[tok, tok, tok, pc("rw_g"), tok, tok, pc("ml_o"), pc("ml_g"),
                  vec, vec, vec, gat, sct, gat, sct],
        out_specs=pl.BlockSpec((2, 1, tl, BR), lambda b, i: (0, b, i, 0)),
        out_shape=jax.ShapeDtypeStruct((2, bsz, L, BR), BF16),
        compiler_params=_cparams(("parallel", "parallel")),
        name="branch_epilogue",
    )(y_f, y_b, bonus, proj, h_f, h_b, proj, proj,
      p["rw_ln_w"].reshape(1, BR), p["rw_ln_b"].reshape(1, BR), p["ml_norm_g"].reshape(1, BR),
      p["mean64"], p["bcast64"], p["mean256"], p["bcast256"])


def _merge_kernel(y_ref, yatt_ref, w_ref, g0_ref, g1_ref, g2_ref, o_ref, wb_ref):
    @pl.when(pl.program_id(1) == 0)
    def _():
        wb_ref[...] = w_ref[0].astype(BF16)

    acc = _sigmoid(g0_ref[...]) * _dot(y_ref[0], wb_ref[0])
    acc += _sigmoid(g1_ref[...]) * _dot(yatt_ref[...], wb_ref[1])
    acc += _sigmoid(g2_ref[...]) * _dot(y_ref[1], wb_ref[2])
    o_ref[...] = acc.astype(o_ref.dtype)


def _merge(ys, y_att, w_branch, layer, proj2d, tm=512, tn=1024):
    _, m, _ = ys.shape
    nb = D // tn
    gate = lambda n: pl.BlockSpec((tm, tn), lambda j, i: (i, n * nb + j))
    return pl.pallas_call(
        _merge_kernel,
        grid=(D // tn, m // tm),
        in_specs=[pl.BlockSpec((2, tm, BR), lambda j, i: (0, i, 0)),
                  pl.BlockSpec((tm, BR), lambda j, i: (i, 0)),
                  pl.BlockSpec((1, 3, BR, tn), lambda j, i: (layer, 0, 0, j), pipeline_mode=pl.Buffered(1)),
                  gate(0), gate(1), gate(2)],
        out_specs=pl.BlockSpec((tm, tn), lambda j, i: (i, j)),
        out_shape=jax.ShapeDtypeStruct((m, D), BF16),
        scratch_shapes=[pltpu.VMEM((3, BR, tn), BF16)],
        compiler_params=_cparams(("arbitrary", "arbitrary")),
        name="branch_merge",
    )(ys, y_att, w_branch, proj2d, proj2d, proj2d)


def _outproj_kernel(*refs, nct):
    a_ref, w_ref, gt_ref, *z_refs, o_ref, wb_ref = refs

    @pl.when(jnp.logical_and(pl.program_id(1) == 0, pl.program_id(2) == 0))
    def _():
        wb_ref[...] = w_ref[0].astype(BF16)

    def body(z):
        o_ref[0] = z + gt_ref[0, 0] * _dot(a_ref[0], wb_ref[...])

    _for_stream_tile(z_refs, pl.program_id(2) < nct, body)


def _out_projection(mixed, w_out, layer, z, gate, n_ctx, tl=256, tn=1024):
    bsz, L, _ = mixed.shape
    nct = n_ctx // tl
    z_specs, z_args = _stream_specs(z, tl, nct, (1, tl, tn), lambda j, b, i: (b, i, j))
    return pl.pallas_call(
        functools.partial(_outproj_kernel, nct=nct),
        grid=(D // tn, bsz, L // tl),
        in_specs=[pl.BlockSpec((1, tl, D), lambda j, b, i: (b, i, 0)),
                  pl.BlockSpec((1, D, tn), lambda j, b, i: (layer, 0, j)),
                  pl.BlockSpec((1, 1, 1, tn), lambda j, b, i: (b, (i >= nct).astype(jnp.int32), 0, j))] + z_specs,
        out_specs=pl.BlockSpec((1, tl, tn), lambda j, b, i: (b, i, j)),
        out_shape=jax.ShapeDtypeStruct((bsz, L, D), F32),
        scratch_shapes=[pltpu.VMEM((D, tn), BF16)],
        compiler_params=_cparams(("arbitrary", "arbitrary", "arbitrary")),
        name="out_projection",
    )(mixed, w_out, gate, *z_args)


def _rope_tables(n_ctx, n_lat):
    rows = n_lat // GRID_W
    row = jnp.repeat(jnp.arange(rows), GRID_W).astype(F32)
    col = jnp.tile(jnp.arange(GRID_W), rows).astype(F32)
    inv_freq = ROPE_THETA ** (-jnp.arange(0, AT_HD // 2, 2, dtype=F32) / (AT_HD // 2))
    ang_lat = jnp.stack([row[:, None] * inv_freq, col[:, None] * inv_freq], axis=1)
    ang = jnp.concatenate([jnp.zeros((n_ctx, 2, AT_HD // 4), F32), ang_lat], axis=0)
    cos, sin = jnp.cos(ang), jnp.sin(ang)
    cos_t = jnp.concatenate([cos[:, 0], cos[:, 0], cos[:, 1], cos[:, 1]], axis=-1)
    sin_t = jnp.concatenate([-sin[:, 0], sin[:, 0], -sin[:, 1], sin[:, 1]], axis=-1)
    return cos_t, sin_t


def _group_consts(width, value):
    member = (np.arange(BR)[:, None] // width) == np.arange(128)[None, :]
    return (jnp.asarray(np.where(member, value, 0.0), dtype=BF16),
            jnp.asarray(np.where(member.T, 1.0, 0.0), dtype=BF16))


def _chunk_tri(tl):
    t = np.arange(tl)
    same = (t[:, None] // RW_T) == (t[None, :] // RW_T)
    fwd = same & (t[None, :] <= t[:, None])
    bwd = same & (t[None, :] >= t[:, None])
    return jnp.asarray(np.stack([fwd, bwd]).astype(np.float32), dtype=BF16)


def kernel(x, c, ctx, c_ctx, norm_g, w_ada, b_ada, w_in, shift_mu, rw_w_up, rw_w0, rw_a_up, rw_a0, rw_k_k, rw_k_a, rw_r_k, rw_ln_w, rw_ln_b, at_q_g, at_k_g, ml_gate_b, ml_norm_g, w_branch, w_out, final_g):
    bsz, n_lat, _ = x.shape
    n_ctx = ctx.shape[1]
    L = n_ctx + n_lat
    depth = w_in.shape[0]

    cos_t, sin_t = _rope_tables(n_ctx, n_lat)
    sum64, bcast64 = _group_consts(RW_N, 1.0)
    mean64, _ = _group_consts(RW_N, 1.0 / RW_N)
    mean256, bcast256 = _group_consts(ML_DV, 1.0 / ML_DV)
    consts = {"sum64": sum64, "bcast64": bcast64, "mean64": mean64, "mean256": mean256, "bcast256": bcast256,
              "tri": _chunk_tri(256)}

    cc = jnp.concatenate([c, c_ctx[None], jnp.zeros((8 - bsz - 1, D), F32)], axis=0)
    mod = _modulation(cc, w_ada, b_ada)

    w_in_t = jnp.swapaxes(w_in, 1, 2)
    z = (ctx, x)
    for l in range(depth):
        sh, sc, gt = mod[l, :, :D], mod[l, :, D:2 * D], mod[l, :, 2 * D:]
        pick = lambda t: jnp.stack([jnp.broadcast_to(t[bsz], (bsz, D)), t[:bsz]], axis=1)[:, :, None, :]
        scale = pick((1.0 + sc) * norm_g[l])
        shift = pick(sh)
        gate = pick(gt)

        h = _norm_mod(z, scale, shift, n_ctx, L, BF16)
        proj2d = _in_projection(h.reshape(bsz * L, D), w_in_t, l)
        proj = proj2d.reshape(bsz, L, N_PROJ)

        p = dict(consts, shift_mu=shift_mu[l], rw_w_up=rw_w_up[l], rw_w0=rw_w0[l], rw_a_up=rw_a_up[l],
                 rw_a0=rw_a0[l], rw_k_k=rw_k_k[l], rw_k_a=rw_k_a[l], rw_r_k=rw_r_k[l], rw_ln_w=rw_ln_w[l],
                 rw_ln_b=rw_ln_b[l], ml_norm_g=ml_norm_g[l])
        at, rt, bt, kt, v_rw, bonus, g_rw = _rw_prepare(proj, p, n_ctx)
        y_f, y_b = _rw_scan(at, rt, bt, kt, v_rw, g_rw, n_ctx)

        qn, kn, vn = _at_prepare(proj, cos_t, sin_t, at_q_g[l], at_k_g[l])
        y_att = _attention(qn, kn, vn, proj, n_ctx)

        bias = ml_gate_b[l].reshape(16)
        h_f, h_b = _mlstm(proj, bias.reshape(1, 16), bias.reshape(16, 1), n_ctx)

        ys = _epilogue(proj, y_f, y_b, bonus, h_f, h_b, p)
        mixed = _merge(ys.reshape(2, bsz * L, BR), y_att.reshape(bsz * L, BR), w_branch, l, proj2d)
        z = _out_projection(mixed.reshape(bsz, L, D), w_out, l, z, gate, n_ctx)

    ones = jnp.ones((bsz, 2, 1, D), F32) * final_g
    zeros = jnp.zeros((bsz, 2, 1, D), F32)
    return _norm_mod(z, ones, zeros, n_ctx, L, F32, latent_only=True)
```

```python
import functools
import math

import numpy as np
import jax
import jax.numpy as jnp
from jax import lax
from jax.experimental import pallas as pl
from jax.experimental.pallas import tpu as pltpu

F32 = jnp.float32
BF16 = jnp.bfloat16
HI = lax.Precision.HIGHEST

D = 2048
BR = 1024
EPS = 1e-6
GRID_W = 64

RW_H, RW_N, RW_LORA = 16, 64, 64
RW_GN_EPS = 64e-5
RW_T = 64
RW_INV_BASE = 8

AT_H, AT_KV, AT_HD = 8, 2, 128
ROPE_THETA = 10000.0

ML_H, ML_DK, ML_DV, ML_T = 4, 128, 256, 128
GATE_CAP = 15.0

D_IN = 17168
LANES, SUBLANES = 128, 8
PROJ_TN = 512
_PROJ_GROUPS = (
    ("merge", 11024, 6144), ("r", 0, 1024), ("k", 1024, 1024), ("v", 2048, 1024),
    ("rw_g", 3328, 1024), ("at_q", 4352, 1024), ("at_g", 5888, 1024), ("ml_v", 7936, 1024),
    ("ml_o", 8960, 1024), ("ml_g", 10000, 1024), ("ml_q", 6912, 512), ("ml_k", 7424, 512),
    ("at_k", 5376, 512), ("wdad", 3072, 512), ("ml_if", 9984, 512),
)
COL = {}
PROJ_SRC = []
for _name, _start, _width in _PROJ_GROUPS:
    COL[_name] = len(PROJ_SRC) * PROJ_TN
    PROJ_SRC += [_start + t * PROJ_TN for t in range(_width // PROJ_TN)]
COL["at_v"] = COL["at_k"] + 256
N_PROJ = len(PROJ_SRC) * PROJ_TN
for _name, _blk in (("merge", 2048), ("r", 1024), ("k", 1024), ("v", 1024), ("rw_g", 1024), ("at_q", 1024),
                    ("at_g", 1024), ("ml_v", 1024), ("ml_o", 1024), ("ml_g", 1024), ("ml_q", 512), ("ml_k", 512),
                    ("at_k", 256), ("at_v", 256), ("wdad", 256), ("ml_if", 128)):
    assert COL[_name] % _blk == 0, _name

VMEM_LIMIT = 48 * 1024 * 1024


def _cparams(sem):
    return pltpu.CompilerParams(dimension_semantics=sem, vmem_limit_bytes=VMEM_LIMIT)


def _dot(a, b, **kw):
    return jnp.dot(a, b, preferred_element_type=F32, **kw)


def _dot_nt(a, b, **kw):
    return lax.dot_general(a, b, (((1,), (1,)), ((), ())), preferred_element_type=F32, **kw)


def _dot_tn(a, b, **kw):
    return lax.dot_general(a, b, (((0,), (0,)), ((), ())), preferred_element_type=F32, **kw)


def _split3(x):
    h = x.astype(BF16)
    r = x - h.astype(F32)
    m = r.astype(BF16)
    l = (r - m.astype(F32)).astype(BF16)
    return h, m, l


def _split2(x):
    h = x.astype(BF16)
    return h, (x - h.astype(F32)).astype(BF16)


def _seg_reduce(x, gather, scatter):
    sh, sl = _split2(_dot(x.astype(BF16), gather))
    return _dot(sh, scatter) + _dot(sl, scatter)


def _dot_const_lhs(c, x):
    h, m, l = _split3(x)
    return _dot(c, h) + _dot(c, m) + _dot(c, l)


def _sigmoid(x):
    return 0.5 * jnp.tanh(0.5 * x) + 0.5


def _silu(x):
    return x * _sigmoid(x)


def _mod_kernel(c_ref, w_ref, b_ref, o_ref):
    @pl.when(pl.program_id(1) == 0)
    def _():
        o_ref[0] = jnp.broadcast_to(b_ref[0], o_ref.shape[1:])

    xh, xl = _split2(_silu(c_ref[...]))
    wh, wl = _split2(w_ref[0])
    o_ref[0] += _dot(xh, wh) + _dot(xh, wl) + _dot(xl, wh)


def _modulation(cc, w_ada, b_ada, tk=256):
    depth = w_ada.shape[0]
    return pl.pallas_call(
        _mod_kernel,
        grid=(depth, D // tk),
        in_specs=[pl.BlockSpec((8, tk), lambda l, k: (0, k)),
                  pl.BlockSpec((1, tk, 3 * D), lambda l, k: (l, k, 0)),
                  pl.BlockSpec((1, 1, 3 * D), lambda l, k: (l, 0, 0))],
        out_specs=pl.BlockSpec((1, 8, 3 * D), lambda l, k: (l, 0, 0)),
        out_shape=jax.ShapeDtypeStruct((depth, 8, 3 * D), F32),
        compiler_params=_cparams(("parallel", "arbitrary")),
        name="adaln_modulation",
    )(cc, w_ada, b_ada.reshape(depth, 1, 3 * D))


def _stream_specs(z, tl, nct, block, index):
    if not isinstance(z, tuple):
        return [pl.BlockSpec(block, lambda *g: index(*g))], [z]

    def ctx_index(*g):
        b, i, j = index(*g)
        return b, jnp.minimum(i, nct - 1), j

    def lat_index(*g):
        b, i, j = index(*g)
        return b, jnp.maximum(i - nct, 0), j

    return [pl.BlockSpec(block, ctx_index), pl.BlockSpec(block, lat_index)], list(z)


def _for_stream_tile(z_refs, is_ctx, body):
    if len(z_refs) == 1:
        body(z_refs[0][0])
        return
    pl.when(is_ctx)(lambda: body(z_refs[0][0]))
    pl.when(jnp.logical_not(is_ctx))(lambda: body(z_refs[1][0]))


def _norm_kernel(*refs, nct, first):
    *z_refs, sc_ref, sh_ref, o_ref = refs

    def body(x):
        y = x * lax.rsqrt(jnp.mean(x * x, axis=-1, keepdims=True) + EPS)
        o_ref[0] = (y * sc_ref[0, 0] + sh_ref[0, 0]).astype(o_ref.dtype)

    _for_stream_tile(z_refs, pl.program_id(1) + first < nct, body)


def _norm_mod(z, scale, shift, n_ctx, L, out_dtype, latent_only=False, tl=256):
    bsz = scale.shape[0]
    nct = n_ctx // tl
    first = nct if latent_only else 0
    seg = lambda b, i: (b, (i + first >= nct).astype(jnp.int32), 0, 0)
    z_specs, z_args = _stream_specs(z, tl, nct, (1, tl, D), lambda b, i: (b, i + first, 0))
    return pl.pallas_call(
        functools.partial(_norm_kernel, nct=nct, first=first),
        grid=(bsz, L // tl - first),
        in_specs=z_specs + [pl.BlockSpec((1, 1, 1, D), seg), pl.BlockSpec((1, 1, 1, D), seg)],
        out_specs=pl.BlockSpec((1, tl, D), lambda b, i: (b, i, 0)),
        out_shape=jax.ShapeDtypeStruct((bsz, L - first * tl, D), out_dtype),
        compiler_params=_cparams(("parallel", "parallel")),
        name="rmsnorm_modulate",
    )(*z_args, scale, shift)


def _inproj_kernel(start_ref, a_ref, wt_ref, o_ref):
    o_ref[...] = _dot_nt(a_ref[...], wt_ref[0].astype(BF16))


def _in_projection(a, w_t, layer, tm=2176):
    m, k = a.shape
    tm = math.gcd(m, tm)
    assert all(s % SUBLANES == 0 for s in PROJ_SRC)
    start = np.array(PROJ_SRC, np.int32) // SUBLANES
    grid_spec = pltpu.PrefetchScalarGridSpec(
        num_scalar_prefetch=1,
        grid=(m // tm, len(PROJ_SRC)),
        in_specs=[pl.BlockSpec((tm, k), lambda i, j, st: (i, 0)),
                  pl.BlockSpec((pl.Element(1), pl.Element(PROJ_TN), pl.Element(k)),
                               lambda i, j, st: (layer, st[j] * SUBLANES, 0))],
        out_specs=pl.BlockSpec((tm, PROJ_TN), lambda i, j, st: (i, j)),
    )
    return pl.pallas_call(
        _inproj_kernel,
        grid_spec=grid_spec,
        out_shape=jax.ShapeDtypeStruct((m, N_PROJ), F32),
        compiler_params=_cparams(("parallel", "arbitrary")),
        name="in_projection",
    )(jnp.asarray(start), a, w_t)


def _rwprep_kernel(r_ref, rp_ref, rn_ref, k_ref, kp_ref, kn_ref, v_ref, vp_ref, vn_ref,
                   w_ref, wp_ref, wn_ref, mur_ref, muk_ref, muv_ref, muw_ref,
                   wup_ref, w0_ref, aup_ref, a0_ref, kk_ref, ka_ref, rk_ref, gat_ref, sct_ref, tri_ref,
                   at_ref, rt_ref, bt_ref, kt_ref, vo_ref, bonus_ref, g_ref, *, tl, n_ctx, L):
    i = pl.program_id(1)
    start = i * tl
    has_prev = jnp.logical_and(start != 0, start != n_ctx).astype(F32)
    has_next = jnp.logical_and(start + tl != n_ctx, start + tl != L).astype(F32)

    def shift(x_ref, p_ref, n_ref, mu_ref):
        x = x_ref[0]
        row = lax.broadcasted_iota(jnp.int32, x.shape, 0)
        prev = jnp.where(row == 0, p_ref[0, 7:8, :] * has_prev, pltpu.roll(x, 1, axis=0))
        nxt = jnp.where(row == tl - 1, n_ref[0, 0:1, :] * has_next, pltpu.roll(x, tl - 1, axis=0))
        mu0, mu1 = mu_ref[0:1, :], mu_ref[1:2, :]
        return x * (1.0 - mu0 - mu1) + mu0 * prev + mu1 * nxt

    r = shift(r_ref, rp_ref, rn_ref, mur_ref)
    k = shift(k_ref, kp_ref, kn_ref, muk_ref)
    v = shift(v_ref, vp_ref, vn_ref, muv_ref)
    wdad = shift(w_ref, wp_ref, wn_ref, muw_ref)
    gat, sct = gat_ref[...], sct_ref[...]

    kk = k * kk_ref[...]
    kk = kk * lax.rsqrt(jnp.maximum(_seg_reduce(kk * kk, gat, sct), 1e-24))
    bonus_ref[0] = _seg_reduce(r * k * rk_ref[...], gat, sct) * v
    vo_ref[0] = v.astype(vo_ref.dtype)

    for d in range(2):
        wd = wdad[:, d * RW_LORA:(d + 1) * RW_LORA]
        ad = wdad[:, 2 * RW_LORA + d * RW_LORA:2 * RW_LORA + (d + 1) * RW_LORA]
        u = w0_ref[d:d + 1, :] + _mmx(jnp.tanh(wd), wup_ref[d], "x3")
        logw = -math.exp(-0.5) * _sigmoid(u)
        a = _sigmoid(a0_ref[d:d + 1, :] + _mmx(ad, aup_ref[d], "x3"))
        kd = k * (1.0 + (a - 1.0) * ka_ref[...])
        lg = _dot_const_lhs(tri_ref[d], logw)
        e_pos = jnp.exp(lg)
        e_neg = 1.0 / e_pos
        at_ref[0, d] = (-kk * jnp.exp(lg - logw)).astype(at_ref.dtype)
        rt_ref[0, d] = (r * e_pos).astype(rt_ref.dtype)
        bt_ref[0, d] = (kk * a * e_neg).astype(bt_ref.dtype)
        kt_ref[0, d] = (kd * e_neg).astype(kt_ref.dtype)
        for c in range(tl // RW_T):
            g_ref[0, d, c] = jnp.exp(jnp.sum(logw[c * RW_T:(c + 1) * RW_T], axis=0, keepdims=True))


def _rw_prepare(proj, p, n_ctx, tl=256):
    bsz, L, _ = proj.shape
    h8 = tl // 8
    nblk8 = L // 8
    main = lambda w, cb: pl.BlockSpec((1, tl, w), lambda b, i: (b, i, cb))
    prev = lambda w, cb: pl.BlockSpec((1, 8, w), lambda b, i: (b, jnp.maximum(i * h8 - 1, 0), cb))
    nxt = lambda w, cb: pl.BlockSpec((1, 8, w), lambda b, i: (b, jnp.minimum((i + 1) * h8, nblk8 - 1), cb))
    full = lambda shape: pl.BlockSpec(shape, lambda b, i: (0,) * len(shape))
    in_specs = []
    args = []
    for name, w in (("r", 1024), ("k", 1024), ("v", 1024), ("wdad", 256)):
        cb = COL[name] // w
        in_specs += [main(w, cb), prev(w, cb), nxt(w, cb)]
        args += [proj, proj, proj]
    mu = p["shift_mu"]
    in_specs += [pl.BlockSpec((2, 1024), lambda b, i: (0, 0)), pl.BlockSpec((2, 1024), lambda b, i: (0, 1)),
                 pl.BlockSpec((2, 1024), lambda b, i: (0, 2)), pl.BlockSpec((2, 256), lambda b, i: (0, 12))]
    args += [mu, mu, mu, mu]
    in_specs += [full((2, RW_LORA, BR)), full((2, BR)), full((2, RW_LORA, BR)), full((2, BR)),
                 full((1, BR)), full((1, BR)), full((1, BR)), full((BR, 128)), full((128, BR)), full((2, tl, tl))]
    args += [p["rw_w_up"], p["rw_w0"], p["rw_a_up"], p["rw_a0"], p["rw_k_k"].reshape(1, BR),
             p["rw_k_a"].reshape(1, BR), p["rw_r_k"].reshape(1, BR), p["sum64"], p["bcast64"], p["tri"]]
    dir_spec = pl.BlockSpec((1, 2, tl, BR), lambda b, i: (b, 0, i, 0))
    tok_spec = pl.BlockSpec((1, tl, BR), lambda b, i: (b, i, 0))
    nchunk = tl // RW_T
    out_specs = [dir_spec, dir_spec, dir_spec, dir_spec, tok_spec, tok_spec,
                 pl.BlockSpec((1, 2, nchunk, 1, BR), lambda b, i: (b, 0, i, 0, 0))]
    dir_shape = jax.ShapeDtypeStruct((bsz, 2, L, BR), BF16)
    out_shape = [dir_shape, dir_shape, dir_shape, dir_shape,
                 jax.ShapeDtypeStruct((bsz, L, BR), BF16), jax.ShapeDtypeStruct((bsz, L, BR), F32),
                 jax.ShapeDtypeStruct((bsz, 2, L // RW_T, 1, BR), F32)]
    return pl.pallas_call(
        functools.partial(_rwprep_kernel, tl=tl, n_ctx=n_ctx, L=L),
        grid=(bsz, L // tl),
        in_specs=in_specs, out_specs=out_specs, out_shape=out_shape,
        compiler_params=_cparams(("parallel", "parallel")),
        name="rwkv_prepare",
    )(*args)


def _scan_chunk(d, j, nc_ctx, nc):
    bwd = jnp.where(j < nc_ctx, nc_ctx - 1 - j, nc - 1 - j + nc_ctx)
    return jnp.where(d == 0, j, bwd)


def _mmx(a, b, mode, dims="nn"):
    f = {"nn": _dot, "nt": _dot_nt, "tn": _dot_tn}[dims]
    ah, bh = a.astype(BF16), b.astype(BF16)
    if mode == "bf16":
        return f(ah, bh)
    al = (a - ah.astype(F32)).astype(BF16)
    bl = (b - bh.astype(F32)).astype(BF16)
    return f(ah, bh) + f(ah, bl) + f(al, bh)


def _rwscan_kernel(atf_ref, rtf_ref, btf_ref, ktf_ref, vf_ref, gf_ref,
                   atb_ref, rtb_ref, btb_ref, ktb_ref, vb_ref, gb_ref, yf_ref, yb_ref, s_ref):
    j = pl.program_id(1)
    T = RW_T

    @pl.when(j == 0)
    def _():
        s_ref[...] = jnp.zeros_like(s_ref)

    tok = lax.broadcasted_iota(jnp.int32, (T, 128), 0)
    col = lax.broadcasted_iota(jnp.int32, (T, 128), 1) % RW_N
    strict = [col < tok, col > tok]
    incl = [col <= tok, col >= tok]
    eye = jnp.where(col == tok, 1.0, 0.0)
    base_blk = (tok // RW_INV_BASE) == (col // RW_INV_BASE)
    merge_blks = []
    s = RW_INV_BASE
    while s < T:
        merge_blks.append(jnp.logical_and((tok // (2 * s)) == (col // (2 * s)), (tok // s) != (col // s)))
        s *= 2
    row2 = lax.broadcasted_iota(jnp.int32, (2 * T, 128), 0)
    lane2 = lax.broadcasted_iota(jnp.int32, (2 * T, 128), 1)
    same_head = (lane2 // RW_N) == (row2 // T)

    def stack(x):
        x = x.astype(BF16)
        return jnp.where(same_head, jnp.concatenate([x, x], axis=0), 0.0)

    cat = lambda xs, axis=0: jnp.concatenate(xs, axis=axis)
    mm = lambda a, b: _dot(a.astype(BF16), b)

    refs = ((atf_ref, rtf_ref, btf_ref, ktf_ref, vf_ref, gf_ref, yf_ref),
            (atb_ref, rtb_ref, btb_ref, ktb_ref, vb_ref, gb_ref, yb_ref))
    chains = [(d, p) for d in range(2) for p in range(BR // 128)]
    dirs = [d for d, _ in chains]
    sls = [slice(p * 128, (p + 1) * 128) for _, p in chains]
    each = lambda fn, *lists: [fn(*xs) for xs in zip(*lists)]
    At = [refs[d][0][0, 0, :, sl] for d, sl in zip(dirs, sls)]
    Rt = [refs[d][1][0, 0, :, sl] for d, sl in zip(dirs, sls)]
    Bt = [refs[d][2][0, 0, :, sl] for d, sl in zip(dirs, sls)]
    Kt = [refs[d][3][0, 0, :, sl] for d, sl in zip(dirs, sls)]
    V = [refs[d][4][0, :, sl] for d, sl in zip(dirs, sls)]
    AR = each(lambda a, r: cat([a, r]), At, Rt)
    BKs = each(lambda b, k: cat([stack(b), stack(k)]), Bt, Kt)
    Vs = [stack(v) for v in V]
    sc = each(_dot_nt, AR, BKs)
    Aab = [jnp.where(strict[d], x[:T, :128], 0.0) for d, x in zip(dirs, sc)]
    Aak = [jnp.where(strict[d], x[:T, 128:], 0.0) for d, x in zip(dirs, sc)]
    Mrbk = [cat([jnp.where(incl[d], x[T:, :128], 0.0), jnp.where(incl[d], x[T:, 128:], 0.0)], axis=1)
            for d, x in zip(dirs, sc)]

    N = [jnp.where(base_blk, x, 0.0) for x in Aab]
    P = [eye + x for x in N]
    N = each(lambda n: mm(n, stack(n)), N)
    NP = each(lambda n, q: mm(cat([n, q]), stack(n)), N, P)
    P = each(lambda q, x: q + x[T:], P, NP)
    P = each(lambda q, x: q + mm(q, stack(x[:T])), P, NP)
    for off_blk in merge_blks:
        CP = each(lambda a, q: mm(jnp.where(off_blk, a, 0.0), stack(q)), Aab, P)
        P = each(lambda q, x: q + mm(q, stack(x)), P, CP)

    AV = each(mm, Aak, Vs)
    S = [s_ref[d, p] for d, p in chains]
    XS = each(lambda a, s_: _dot_nt(a, s_.astype(BF16)), AR, S)
    U = each(lambda q, x, w: mm(q, stack(x[:T] + w)).astype(BF16), P, XS, AV)
    Y = each(lambda x, m, u, v: x[T:] + mm(m, cat([stack(u), v])), XS, Mrbk, U, Vs)
    dS = each(lambda u, v, b, k: _dot_tn(cat([u, v]), cat([b, k])), U, V, Bt, Kt)
    same_head_sq = (lane2 // RW_N) == (row2 // RW_N)
    for c, (d, p) in enumerate(chains):
        refs[d][6][0, :, sls[c]] = Y[c]
        s_ref[d, p] = (S[c] + jnp.where(same_head_sq, dS[c], 0.0)) * refs[d][5][0, 0, 0, :, sls[c]]


def _rw_scan(at, rt, bt, kt, v, g, n_ctx):
    bsz, _, L, _ = at.shape
    nc, nc_ctx = L // RW_T, n_ctx // RW_T

    def specs(d):
        ch = lambda j: _scan_chunk(d, j, nc_ctx, nc)
        dspec = pl.BlockSpec((1, 1, RW_T, BR), lambda b, j: (b, d, ch(j), 0))
        tspec = pl.BlockSpec((1, RW_T, BR), lambda b, j: (b, ch(j), 0))
        return tspec, [dspec, dspec, dspec, dspec, tspec,
                       pl.BlockSpec((1, 1, 1, 1, BR), lambda b, j: (b, d, ch(j), 0, 0))]

    (out_f, in_f), (out_b, in_b) = specs(0), specs(1)
    shape = jax.ShapeDtypeStruct((bsz, L, BR), F32)
    return pl.pallas_call(
        _rwscan_kernel,
        grid=(bsz, nc),
        in_specs=in_f + in_b,
        out_specs=[out_f, out_b],
        out_shape=[shape, shape],
        scratch_shapes=[pltpu.VMEM((2, BR // 128, 128, 128), F32)],
        compiler_params=_cparams(("parallel", "arbitrary")),
        name="rwkv_scan",
    )(at, rt, bt, kt, v, g, at, rt, bt, kt, v, g)


def _atprep_kernel(q_ref, k_ref, v_ref, cos_ref, sin_ref, qg_ref, kg_ref, qo_ref, ko_ref, vo_ref):
    cos = cos_ref[...]
    sin = sin_ref[...]
    lane = lax.broadcasted_iota(jnp.int32, cos.shape, 1)
    first_half = (lane % 64) < 32

    q_scale = AT_HD ** -0.5 * math.log2(math.e)
    heads = [(q_ref, qo_ref, h, qg_ref[...] * q_scale) for h in range(AT_H)]
    heads += [(k_ref, ko_ref, h, kg_ref[...]) for h in range(AT_KV)]
    x = [src[0, :, h * AT_HD:(h + 1) * AT_HD] for src, _, h, _ in heads]
    y = [xi * lax.rsqrt(jnp.mean(xi * xi, axis=-1, keepdims=True) + EPS) * g for xi, (_, _, _, g) in zip(x, heads)]
    up = [pltpu.roll(yi, 96, axis=1) for yi in y]
    down = [pltpu.roll(yi, 32, axis=1) for yi in y]
    for (_, dst, h, _), yi, u, dn in zip(heads, y, up, down):
        dst[0, :, h * AT_HD:(h + 1) * AT_HD] = (yi * cos + jnp.where(first_half, u, dn) * sin).astype(dst.dtype)
    vo_ref[0] = v_ref[0].astype(vo_ref.dtype)


def _at_prepare(proj, cos, sin, q_g, k_g, tl=256):
    bsz, L, _ = proj.shape
    kvw = AT_KV * AT_HD
    tok = lambda w: pl.BlockSpec((1, tl, w), lambda b, i: (b, i, 0))
    return pl.pallas_call(
        _atprep_kernel,
        grid=(bsz, L // tl),
        in_specs=[pl.BlockSpec((1, tl, BR), lambda b, i: (b, i, COL["at_q"] // BR)),
                  pl.BlockSpec((1, tl, kvw), lambda b, i: (b, i, COL["at_k"] // kvw)),
                  pl.BlockSpec((1, tl, kvw), lambda b, i: (b, i, COL["at_v"] // kvw)),
                  pl.BlockSpec((tl, AT_HD), lambda b, i: (i, 0)),
                  pl.BlockSpec((tl, AT_HD), lambda b, i: (i, 0)),
                  pl.BlockSpec((1, AT_HD), lambda b, i: (0, 0)),
                  pl.BlockSpec((1, AT_HD), lambda b, i: (0, 0))],
        out_specs=[tok(BR), tok(kvw), tok(kvw)],
        out_shape=[jax.ShapeDtypeStruct((bsz, L, BR), BF16),
                   jax.ShapeDtypeStruct((bsz, L, kvw), BF16),
                   jax.ShapeDtypeStruct((bsz, L, kvw), BF16)],
        compiler_params=_cparams(("parallel", "parallel")),
        name="gqa_prepare",
    )(proj, proj, proj, cos, sin, q_g.reshape(1, AT_HD), k_g.reshape(1, AT_HD))


def _attn_kernel(q_ref, k_ref, v_ref, gate_ref, o_ref, s_ref, p_ref, l_ref, *, tq, n_ctx, kb, rows):
    i = pl.program_id(2)
    grp = AT_H // AT_KV
    def attend(n_keys):
        chunks = [slice(c * kb, (c + 1) * kb) for c in range(n_keys // kb)]

        def scores(r):
            s_ref[r, :, :n_keys] = _dot_nt(q_ref[0, :, r * AT_HD:(r + 1) * AT_HD], k_ref[0, :n_keys, :])

        def softmax(r):
            for rb in range(tq // rows):
                rs = slice(rb * rows, (rb + 1) * rows)
                mx = s_ref[r, rs, chunks[0]]
                for ch in chunks[1:]:
                    mx = jnp.maximum(mx, s_ref[r, rs, ch])
                m = jnp.broadcast_to(jnp.max(mx, axis=-1, keepdims=True), mx.shape)
                tot = jnp.zeros_like(mx)
                for ch in chunks:
                    p = jnp.exp2(s_ref[r, rs, ch] - m)
                    tot = tot + p
                    p_ref[r, rs, ch] = p.astype(BF16)
                l_ref[r, rs, :] = jnp.broadcast_to(jnp.sum(tot, axis=-1, keepdims=True), (rows, AT_HD))

        def values(r):
            hs = slice(r * AT_HD, (r + 1) * AT_HD)
            att = _dot(p_ref[r, :, :n_keys], v_ref[0, :n_keys, :]) / l_ref[r]
            o_ref[0, :, hs] = (att * _silu(gate_ref[0, :, hs])).astype(o_ref.dtype)

        stages = (scores, softmax, values)
        for t in range(grp + len(stages) - 1):
            for st, fn in enumerate(stages):
                if 0 <= t - st < grp:
                    fn(t - st)

    @pl.when(i * tq < n_ctx)
    def _():
        attend(n_ctx)

    @pl.when(i * tq >= n_ctx)
    def _():
        attend(k_ref.shape[1])


def _attention(q, k, v, proj, n_ctx, tq=256, kb=256, rows=32):
    bsz, L, _ = q.shape
    gw = (AT_H // AT_KV) * AT_HD
    return pl.pallas_call(
        functools.partial(_attn_kernel, tq=tq, n_ctx=n_ctx, kb=kb, rows=rows),
        scratch_shapes=[pltpu.VMEM((gw // AT_HD, tq, L), F32), pltpu.VMEM((gw // AT_HD, tq, L), BF16),
                        pltpu.VMEM((gw // AT_HD, tq, AT_HD), F32)],
        grid=(bsz, AT_KV, L // tq),
        in_specs=[pl.BlockSpec((1, tq, gw), lambda b, g, i: (b, i, g)),
                  pl.BlockSpec((1, L, AT_HD), lambda b, g, i: (b, 0, g)),
                  pl.BlockSpec((1, L, AT_HD), lambda b, g, i: (b, 0, g)),
                  pl.BlockSpec((1, tq, gw), lambda b, g, i: (b, i, COL["at_g"] // gw + g))],
        out_specs=pl.BlockSpec((1, tq, gw), lambda b, g, i: (b, i, g)),
        out_shape=jax.ShapeDtypeStruct((bsz, L, BR), BF16),
        compiler_params=_cparams(("parallel", "parallel", "parallel")),
        name="gqa_attention",
    )(q, k, v, proj)


def _cap_gates(pre):
    return GATE_CAP * jnp.tanh(pre / GATE_CAP)


def _log_sigmoid(x):
    return jnp.minimum(x, 0.0) - jnp.log1p(jnp.exp(-jnp.abs(x)))


def _mlstm_kernel(qf_ref, kf_ref, vf_ref, gcf_ref, qb_ref, kb_ref, vb_ref, gcb_ref,
                  bc_ref, br_ref, hf_ref, hb_ref, c_ref, n_ref, m_ref):
    j = pl.program_id(1)
    T = ML_T

    @pl.when(j == 0)
    def _():
        c_ref[...] = jnp.zeros_like(c_ref)
        n_ref[...] = jnp.zeros_like(n_ref)
        m_ref[...] = jnp.zeros_like(m_ref)

    r2 = lax.broadcasted_iota(jnp.int32, (T, T), 0)
    c2 = lax.broadcasted_iota(jnp.int32, (T, T), 1)

    refs = ((qf_ref, kf_ref, vf_ref, gcf_ref, None, hf_ref), (qb_ref, kb_ref, vb_ref, gcb_ref, None, hb_ref))
    seen = [c2 <= r2, c2 >= r2]
    seen_t = [r2 <= c2, r2 >= c2]
    gc = [_cap_gates(refs[d][3][0][:, :16] + bc_ref[...]) for d in range(2)]
    gr = [_cap_gates(refs[d][3][0].T[:16, :] + br_ref[...]) for d in range(2)]
    lsc = [_log_sigmoid(x) for x in gc]
    lsr = [_log_sigmoid(x) for x in gr]

    chains = [(d, h) for d in range(2) for h in range(ML_H)]
    each = lambda fn, *lists: [fn(*xs) for xs in zip(*lists)]
    lane16 = lax.broadcasted_iota(jnp.int32, (T, 16), 1)
    sub16 = lax.broadcasted_iota(jnp.int32, (16, T), 0)
    pick_row = lambda x, idx: jnp.sum(jnp.where(sub16 == idx, x, 0.0), axis=0, keepdims=True)
    pick_col = lambda x, idx: jnp.sum(jnp.where(lane16 == idx, x, 0.0), axis=1, keepdims=True)
    li_row = [pick_row(gr[d], d * ML_H + h) for d, h in chains]
    lf_row = [pick_row(lsr[d], (2 + d) * ML_H + h) for d, h in chains]
    li_col = [pick_col(gc[d], d * ML_H + h) for d, h in chains]
    lf_col = [pick_col(lsc[d], (2 + d) * ML_H + h) for d, h in chains]
    b_col = [jnp.sum(jnp.where(seen[d], x, 0.0), axis=1, keepdims=True) for (d, _), x in zip(chains, lf_row)]
    b_row = [jnp.sum(jnp.where(seen_t[d], x, 0.0), axis=0, keepdims=True) for (d, _), x in zip(chains, lf_col)]
    g = [jnp.sum(x, axis=0, keepdims=True) for x in lf_col]
    m_prev = [m_ref[d, h] for d, h in chains]
    q = [refs[d][0][0, :, h * ML_DK:(h + 1) * ML_DK] * (ML_DK ** -0.5) for d, h in chains]
    k = [refs[d][1][0, :, h * ML_DK:(h + 1) * ML_DK] for d, h in chains]
    vb = [refs[d][2][0, :, h * ML_DV:(h + 1) * ML_DV].astype(BF16) for d, h in chains]
    C = [c_ref[d, h] for d, h in chains]
    n = [n_ref[d, h] for d, h in chains]
    qb = [x.astype(BF16) for x in q]

    dmat = [jnp.where(seen[d], bc - br + li, -jnp.inf) for (d, _), bc, br, li in zip(chains, b_col, b_row, li_row)]
    m_inter = each(lambda bc, m: bc + m, b_col, m_prev)
    m_t = each(lambda mi, dm: jnp.maximum(mi, jnp.max(dm, axis=-1, keepdims=True)), m_inter, dmat)
    w_inter = each(lambda mi, mt: jnp.exp(mi - mt), m_inter, m_t)
    qk = each(lambda a, b: _dot_nt(a, b.astype(BF16)), qb, k)
    qc = each(lambda a, b: _dot(a, b.astype(BF16)), qb, C)
    s = each(lambda x, dm, mt: x * jnp.exp(dm - mt), qk, dmat, m_t)
    sv = each(lambda a, b: _dot(a.astype(BF16), b), s, vb)
    qn = each(lambda a, b: jnp.sum(a * b, axis=-1, keepdims=True), q, n)
    den = each(lambda w, a, x: w * a + jnp.sum(x, axis=-1, keepdims=True), w_inter, qn, s)
    for (d, h), w, a, b, dn, mt in zip(chains, w_inter, qc, sv, den, m_t):
        refs[d][5][0, :, h * ML_DV:(h + 1) * ML_DV] = (w * a + b) / jnp.maximum(jnp.abs(dn), jnp.exp(-mt))

    loga = each(lambda g_, bc, li: g_ - bc + li, g, b_col, li_col)
    m_new = each(lambda g_, m, la: jnp.maximum(g_ + m, jnp.max(la, axis=0, keepdims=True)), g, m_prev, loga)
    carry = each(lambda g_, m, mn: jnp.exp(g_ + m - mn), g, m_prev, m_new)
    wk = each(lambda la, mn, k_: jnp.exp(la - mn) * k_, loga, m_new, k)
    kv = each(lambda a, b: _dot_tn(a.astype(BF16), b), wk, vb)
    for (d, h), cr, c_, kv_, n_, wk_, mn in zip(chains, carry, C, kv, n, wk, m_new):
        c_ref[d, h] = cr * c_ + kv_
        n_ref[d, h] = cr * n_ + jnp.sum(wk_, axis=0, keepdims=True)
        m_ref[d, h] = mn


def _mlstm(proj, bias_col, bias_row, n_ctx):
    bsz, L, _ = proj.shape
    nc, nc_ctx = L // ML_T, n_ctx // ML_T
    qw, vw = ML_H * ML_DK, ML_H * ML_DV

    def dir_specs(d):
        ch = lambda j: _scan_chunk(d, j, nc_ctx, nc)
        return [pl.BlockSpec((1, ML_T, qw), lambda b, j: (b, ch(j), COL["ml_q"] // qw)),
                pl.BlockSpec((1, ML_T, qw), lambda b, j: (b, ch(j), COL["ml_k"] // qw)),
                pl.BlockSpec((1, ML_T, vw), lambda b, j: (b, ch(j), COL["ml_v"] // vw)),
                pl.BlockSpec((1, ML_T, 128), lambda b, j: (b, ch(j), COL["ml_if"] // 128))]

    def out_spec(d):
        ch = lambda j: _scan_chunk(d, j, nc_ctx, nc)
        return pl.BlockSpec((1, ML_T, vw), lambda b, j: (b, ch(j), 0))

    shape = jax.ShapeDtypeStruct((bsz, L, vw), F32)
    return pl.pallas_call(
        _mlstm_kernel,
        grid=(bsz, nc),
        in_specs=dir_specs(0) + dir_specs(1) + [pl.BlockSpec((1, 16), lambda b, j: (0, 0)),
                                                pl.BlockSpec((16, 1), lambda b, j: (0, 0))],
        out_specs=[out_spec(0), out_spec(1)],
        out_shape=[shape, shape],
        scratch_shapes=[pltpu.VMEM((2, ML_H, ML_DK, ML_DV), F32),
                        pltpu.VMEM((2, ML_H, 1, ML_DK), F32),
                        pltpu.VMEM((2, ML_H, 1, 1), F32)],
        compiler_params=_cparams(("parallel", "arbitrary")),
        name="mlstm_scan",
    )(proj, proj, proj, proj, proj, proj, proj, proj, bias_col, bias_row)


def _epilogue_kernel(yf_ref, yb_ref, bonus_ref, rwg_ref, hf_ref, hb_ref, mlo_ref, mlg_ref,
                     lnw_ref, lnb_ref, mng_ref, m64_ref, b64_ref, m256_ref, b256_ref, o_ref):
    y = yf_ref[0] + yb_ref[0]
    mu = _seg_reduce(y, m64_ref[...], b64_ref[...])
    yc = y - mu
    var = _seg_reduce(yc * yc, m64_ref[...], b64_ref[...])
    ya = yc * lax.rsqrt(var + RW_GN_EPS) * lnw_ref[...] + lnb_ref[...] + bonus_ref[0]
    o_ref[0, 0] = (ya * _silu(rwg_ref[0])).astype(o_ref.dtype)

    hh = hf_ref[0] + hb_ref[0]
    ms = _seg_reduce(hh * hh, m256_ref[...], b256_ref[...])
    hn = hh * lax.rsqrt(ms + EPS) * mng_ref[...]
    o_ref[1, 0] = (_sigmoid(mlo_ref[0]) * hn * _silu(mlg_ref[0])).astype(o_ref.dtype)


def _epilogue(proj, y_f, y_b, bonus, h_f, h_b, p, tl=256):
    bsz, L, _ = proj.shape
    tok = pl.BlockSpec((1, tl, BR), lambda b, i: (b, i, 0))
    pc = lambda name: pl.BlockSpec((1, tl, BR), lambda b, i: (b, i, COL[name] // BR))
    vec = pl.BlockSpec((1, BR), lambda b, i: (0, 0))
    gat = pl.BlockSpec((BR, 128), lambda b, i: (0, 0))
    sct = pl.BlockSpec((128, BR), lambda b, i: (0, 0))
    return pl.pallas_call(
        _epilogue_kernel,
        grid=(bsz, L // tl),
        in_specs=[tok, tok, tok, pc("rw_g"), tok, tok, pc("ml_o"), pc("ml_g"),
                  vec, vec, vec, gat, sct, gat, sct],
        out_specs=pl.BlockSpec((2, 1, tl, BR), lambda b, i: (0, b, i, 0)),
        out_shape=jax.ShapeDtypeStruct((2, bsz, L, BR), BF16),
        compiler_params=_cparams(("parallel", "parallel")),
        name="branch_epilogue",
    )(y_f, y_b, bonus, proj, h_f, h_b, proj, proj,
      p["rw_ln_w"].reshape(1, BR), p["rw_ln_b"].reshape(1, BR), p["ml_norm_g"].reshape(1, BR),
      p["mean64"], p["bcast64"], p["mean256"], p["bcast256"])


def _merge_kernel(y_ref, yatt_ref, w_ref, g0_ref, g1_ref, g2_ref, o_ref, wb_ref):
    @pl.when(pl.program_id(1) == 0)
    def _():
        wb_ref[...] = w_ref[0].astype(BF16)

    acc = _sigmoid(g0_ref[...]) * _dot(y_ref[0], wb_ref[0])
    acc += _sigmoid(g1_ref[...]) * _dot(yatt_ref[...], wb_ref[1])
    acc += _sigmoid(g2_ref[...]) * _dot(y_ref[1], wb_ref[2])
    o_ref[...] = acc.astype(o_ref.dtype)


def _merge(ys, y_att, w_branch, layer, proj2d, tm=512, tn=1024):
    _, m, _ = ys.shape
    nb = D // tn
    gate = lambda n: pl.BlockSpec((tm, tn), lambda j, i: (i, n * nb + j))
    return pl.pallas_call(
        _merge_kernel,
        grid=(D // tn, m // tm),
        in_specs=[pl.BlockSpec((2, tm, BR), lambda j, i: (0, i, 0)),
                  pl.BlockSpec((tm, BR), lambda j, i: (i, 0)),
                  pl.BlockSpec((1, 3, BR, tn), lambda j, i: (layer, 0, 0, j), pipeline_mode=pl.Buffered(1)),
                  gate(0), gate(1), gate(2)],
        out_specs=pl.BlockSpec((tm, tn), lambda j, i: (i, j)),
        out_shape=jax.ShapeDtypeStruct((m, D), BF16),
        scratch_shapes=[pltpu.VMEM((3, BR, tn), BF16)],
        compiler_params=_cparams(("arbitrary", "arbitrary")),
        name="branch_merge",
    )(ys, y_att, w_branch, proj2d, proj2d, proj2d)


def _outproj_kernel(*refs, nct, n_z, last):
    a_ref, w_ref, gt_ref, sc_ref, sh_ref = refs[:5]
    z_refs, out_refs, wb_ref = refs[5:5 + n_z], refs[5 + n_z:-1], refs[-1]

    @pl.when(jnp.logical_and(pl.program_id(0) == 0, pl.program_id(1) == 0))
    def _():
        wb_ref[...] = w_ref[0].astype(BF16)

    def body(z):
        z_new = z + gt_ref[0, 0] * _dot(a_ref[0], wb_ref[...])
        y = z_new * lax.rsqrt(jnp.mean(z_new * z_new, axis=-1, keepdims=True) + EPS)
        normed = y * sc_ref[0, 0] + sh_ref[0, 0]
        if last:
            out_refs[0][0] = normed
        else:
            out_refs[0][0] = z_new
            out_refs[1][0] = normed.astype(out_refs[1].dtype)

    _for_stream_tile(z_refs, pl.program_id(1) < nct, body)


def _out_projection(mixed, w_out, layer, z, gate, scale, shift, n_ctx, last, tl=256):
    bsz, L, _ = mixed.shape
    nct = n_ctx // tl
    z_specs, z_args = _stream_specs(z, tl, nct, (1, tl, D), lambda b, i: (b, i, 0))
    seg = lambda b, i: (b, (i >= nct).astype(jnp.int32), 0, 0)
    tok = pl.BlockSpec((1, tl, D), lambda b, i: (b, i, 0))
    if last:
        out_specs = [pl.BlockSpec((1, tl, D), lambda b, i: (b, jnp.maximum(i - nct, 0), 0))]
        out_shape = [jax.ShapeDtypeStruct((bsz, L - n_ctx, D), F32)]
    else:
        out_specs = [tok, tok]
        out_shape = [jax.ShapeDtypeStruct((bsz, L, D), F32), jax.ShapeDtypeStruct((bsz, L, D), BF16)]
    return pl.pallas_call(
        functools.partial(_outproj_kernel, nct=nct, n_z=len(z_args), last=last),
        grid=(bsz, L // tl),
        in_specs=[tok,
                  pl.BlockSpec((1, D, D), lambda b, i: (layer, 0, 0), pipeline_mode=pl.Buffered(1)),
                  pl.BlockSpec((1, 1, 1, D), seg), pl.BlockSpec((1, 1, 1, D), seg),
                  pl.BlockSpec((1, 1, 1, D), seg)] + z_specs,
        out_specs=out_specs,
        out_shape=out_shape,
        scratch_shapes=[pltpu.VMEM((D, D), BF16)],
        compiler_params=_cparams(("arbitrary", "arbitrary")),
        name="out_projection",
    )(mixed, w_out, gate, scale, shift, *z_args)


def _rope_tables(n_ctx, n_lat):
    rows = n_lat // GRID_W
    row = jnp.repeat(jnp.arange(rows), GRID_W).astype(F32)
    col = jnp.tile(jnp.arange(GRID_W), rows).astype(F32)
    inv_freq = ROPE_THETA ** (-jnp.arange(0, AT_HD // 2, 2, dtype=F32) / (AT_HD // 2))
    ang_lat = jnp.stack([row[:, None] * inv_freq, col[:, None] * inv_freq], axis=1)
    ang = jnp.concatenate([jnp.zeros((n_ctx, 2, AT_HD // 4), F32), ang_lat], axis=0)
    cos, sin = jnp.cos(ang), jnp.sin(ang)
    cos_t = jnp.concatenate([cos[:, 0], cos[:, 0], cos[:, 1], cos[:, 1]], axis=-1)
    sin_t = jnp.concatenate([-sin[:, 0], sin[:, 0], -sin[:, 1], sin[:, 1]], axis=-1)
    return cos_t, sin_t


def _group_consts(width, value):
    member = (np.arange(BR)[:, None] // width) == np.arange(128)[None, :]
    return (jnp.asarray(np.where(member, value, 0.0), dtype=BF16),
            jnp.asarray(np.where(member.T, 1.0, 0.0), dtype=BF16))


def _chunk_tri(tl):
    t = np.arange(tl)
    same = (t[:, None] // RW_T) == (t[None, :] // RW_T)
    fwd = same & (t[None, :] <= t[:, None])
    bwd = same & (t[None, :] >= t[:, None])
    return jnp.asarray(np.stack([fwd, bwd]).astype(np.float32), dtype=BF16)


def kernel(x, c, ctx, c_ctx, norm_g, w_ada, b_ada, w_in, shift_mu, rw_w_up, rw_w0, rw_a_up, rw_a0, rw_k_k, rw_k_a, rw_r_k, rw_ln_w, rw_ln_b, at_q_g, at_k_g, ml_gate_b, ml_norm_g, w_branch, w_out, final_g):
    bsz, n_lat, _ = x.shape
    n_ctx = ctx.shape[1]
    L = n_ctx + n_lat
    depth = w_in.shape[0]

    cos_t, sin_t = _rope_tables(n_ctx, n_lat)
    sum64, bcast64 = _group_consts(RW_N, 1.0)
    mean64, _ = _group_consts(RW_N, 1.0 / RW_N)
    mean256, bcast256 = _group_consts(ML_DV, 1.0 / ML_DV)
    consts = {"sum64": sum64, "bcast64": bcast64, "mean64": mean64, "mean256": mean256, "bcast256": bcast256,
              "tri": _chunk_tri(256)}

    cc = jnp.concatenate([c, c_ctx[None], jnp.zeros((8 - bsz - 1, D), F32)], axis=0)
    mod = _modulation(cc, w_ada, b_ada)

    w_in_t = jnp.swapaxes(w_in, 1, 2)
    pick = lambda t: jnp.stack([jnp.broadcast_to(t[bsz], (bsz, D)), t[:bsz]], axis=1)[:, :, None, :]
    scales = [pick((1.0 + mod[l, :, D:2 * D]) * norm_g[l]) for l in range(depth)]
    shifts = [pick(mod[l, :, :D]) for l in range(depth)]
    gates = [pick(mod[l, :, 2 * D:]) for l in range(depth)]
    scales.append(jnp.broadcast_to(final_g, (bsz, 2, 1, D)))
    shifts.append(jnp.zeros((bsz, 2, 1, D), F32))

    z = (ctx, x)
    h = _norm_mod(z, scales[0], shifts[0], n_ctx, L, BF16)
    for l in range(depth):
        proj2d = _in_projection(h.reshape(bsz * L, D), w_in_t, l)
        proj = proj2d.reshape(bsz, L, N_PROJ)

        p = dict(consts, shift_mu=shift_mu[l], rw_w_up=rw_w_up[l], rw_w0=rw_w0[l], rw_a_up=rw_a_up[l],
                 rw_a0=rw_a0[l], rw_k_k=rw_k_k[l], rw_k_a=rw_k_a[l], rw_r_k=rw_r_k[l], rw_ln_w=rw_ln_w[l],
                 rw_ln_b=rw_ln_b[l], ml_norm_g=ml_norm_g[l])
        at, rt, bt, kt, v_rw, bonus, g_rw = _rw_prepare(proj, p, n_ctx)
        y_f, y_b = _rw_scan(at, rt, bt, kt, v_rw, g_rw, n_ctx)

        qn, kn, vn = _at_prepare(proj, cos_t, sin_t, at_q_g[l], at_k_g[l])
        y_att = _attention(qn, kn, vn, proj, n_ctx)

        bias = ml_gate_b[l].reshape(16)
        h_f, h_b = _mlstm(proj, bias.reshape(1, 16), bias.reshape(16, 1), n_ctx)

        ys = _epilogue(proj, y_f, y_b, bonus, h_f, h_b, p)
        mixed = _merge(ys.reshape(2, bsz * L, BR), y_att.reshape(bsz * L, BR), w_branch, l, proj2d)
        outs = _out_projection(mixed.reshape(bsz, L, D), w_out, l, z, gates[l], scales[l + 1], shifts[l + 1],
                               n_ctx, last=(l == depth - 1))
        if l == depth - 1:
            return outs[0]
        z, h = outs
```

```python
import functools
import math

import numpy as np
import jax
import jax.numpy as jnp
from jax import lax
from jax.experimental import pallas as pl
from jax.experimental.pallas import tpu as pltpu

F32 = jnp.float32
BF16 = jnp.bfloat16
HI = lax.Precision.HIGHEST

D = 2048
BR = 1024
EPS = 1e-6
GRID_W = 64

RW_H, RW_N, RW_LORA = 16, 64, 64
RW_GN_EPS = 64e-5
RW_T = 64
RW_INV_BASE = 8

AT_H, AT_KV, AT_HD = 8, 2, 128
ROPE_THETA = 10000.0

ML_H, ML_DK, ML_DV, ML_T = 4, 128, 256, 128
GATE_CAP = 15.0

D_IN = 17168
LANES, SUBLANES = 128, 8
PROJ_TN = 512
_PROJ_GROUPS = (
    ("merge", 11024, 6144), ("r", 0, 1024), ("k", 1024, 1024), ("v", 2048, 1024),
    ("rw_g", 3328, 1024), ("at_q", 4352, 1024), ("at_g", 5888, 1024), ("ml_v", 7936, 1024),
    ("ml_o", 8960, 1024), ("ml_g", 10000, 1024), ("ml_q", 6912, 512), ("ml_k", 7424, 512),
    ("at_k", 5376, 512), ("wdad", 3072, 512), ("ml_if", 9984, 512),
)
COL = {}
PROJ_SRC = []
for _name, _start, _width in _PROJ_GROUPS:
    COL[_name] = len(PROJ_SRC) * PROJ_TN
    PROJ_SRC += [_start + t * PROJ_TN for t in range(_width // PROJ_TN)]
COL["at_v"] = COL["at_k"] + 256
N_PROJ = len(PROJ_SRC) * PROJ_TN
for _name, _blk in (("merge", 2048), ("r", 1024), ("k", 1024), ("v", 1024), ("rw_g", 1024), ("at_q", 1024),
                    ("at_g", 1024), ("ml_v", 1024), ("ml_o", 1024), ("ml_g", 1024), ("ml_q", 512), ("ml_k", 512),
                    ("at_k", 256), ("at_v", 256), ("wdad", 256), ("ml_if", 128)):
    assert COL[_name] % _blk == 0, _name

VMEM_LIMIT = 48 * 1024 * 1024


def _cparams(sem):
    return pltpu.CompilerParams(dimension_semantics=sem, vmem_limit_bytes=VMEM_LIMIT)


def _dot(a, b, **kw):
    return jnp.dot(a, b, preferred_element_type=F32, **kw)


def _dot_nt(a, b, **kw):
    return lax.dot_general(a, b, (((1,), (1,)), ((), ())), preferred_element_type=F32, **kw)


def _dot_tn(a, b, **kw):
    return lax.dot_general(a, b, (((0,), (0,)), ((), ())), preferred_element_type=F32, **kw)


def _split3(x):
    h = x.astype(BF16)
    r = x - h.astype(F32)
    m = r.astype(BF16)
    l = (r - m.astype(F32)).astype(BF16)
    return h, m, l


def _split2(x):
    h = x.astype(BF16)
    return h, (x - h.astype(F32)).astype(BF16)


def _seg_reduce(x, gather, scatter):
    sh, sl = _split2(_dot(x.astype(BF16), gather))
    return _dot(sh, scatter) + _dot(sl, scatter)


def _dot_const_lhs(c, x):
    h, m, l = _split3(x)
    return _dot(c, h) + _dot(c, m) + _dot(c, l)


def _sigmoid(x):
    return 0.5 * jnp.tanh(0.5 * x) + 0.5


def _silu(x):
    return x * _sigmoid(x)


def _mod_kernel(c_ref, w_ref, b_ref, o_ref):
    @pl.when(pl.program_id(1) == 0)
    def _():
        o_ref[0] = jnp.broadcast_to(b_ref[0], o_ref.shape[1:])

    xh, xl = _split2(_silu(c_ref[...]))
    wh, wl = _split2(w_ref[0])
    o_ref[0] += _dot(xh, wh) + _dot(xh, wl) + _dot(xl, wh)


def _modulation(cc, w_ada, b_ada, tk=256):
    depth = w_ada.shape[0]
    return pl.pallas_call(
        _mod_kernel,
        grid=(depth, D // tk),
        in_specs=[pl.BlockSpec((8, tk), lambda l, k: (0, k)),
                  pl.BlockSpec((1, tk, 3 * D), lambda l, k: (l, k, 0)),
                  pl.BlockSpec((1, 1, 3 * D), lambda l, k: (l, 0, 0))],
        out_specs=pl.BlockSpec((1, 8, 3 * D), lambda l, k: (l, 0, 0)),
        out_shape=jax.ShapeDtypeStruct((depth, 8, 3 * D), F32),
        compiler_params=_cparams(("parallel", "arbitrary")),
        name="adaln_modulation",
    )(cc, w_ada, b_ada.reshape(depth, 1, 3 * D))


def _stream_specs(z, tl, nct, block, index):
    if not isinstance(z, tuple):
        return [pl.BlockSpec(block, lambda *g: index(*g))], [z]

    def ctx_index(*g):
        b, i, j = index(*g)
        return b, jnp.minimum(i, nct - 1), j

    def lat_index(*g):
        b, i, j = index(*g)
        return b, jnp.maximum(i - nct, 0), j

    return [pl.BlockSpec(block, ctx_index), pl.BlockSpec(block, lat_index)], list(z)


def _for_stream_tile(z_refs, is_ctx, body):
    if len(z_refs) == 1:
        body(z_refs[0][0])
        return
    pl.when(is_ctx)(lambda: body(z_refs[0][0]))
    pl.when(jnp.logical_not(is_ctx))(lambda: body(z_refs[1][0]))


def _norm_kernel(*refs, nct, first):
    *z_refs, sc_ref, sh_ref, o_ref = refs

    def body(x):
        y = x * lax.rsqrt(jnp.mean(x * x, axis=-1, keepdims=True) + EPS)
        o_ref[0] = (y * sc_ref[0, 0] + sh_ref[0, 0]).astype(o_ref.dtype)

    _for_stream_tile(z_refs, pl.program_id(1) + first < nct, body)


def _norm_mod(z, scale, shift, n_ctx, L, out_dtype, latent_only=False, tl=256):
    bsz = scale.shape[0]
    nct = n_ctx // tl
    first = nct if latent_only else 0
    seg = lambda b, i: (b, (i + first >= nct).astype(jnp.int32), 0, 0)
    z_specs, z_args = _stream_specs(z, tl, nct, (1, tl, D), lambda b, i: (b, i + first, 0))
    return pl.pallas_call(
        functools.partial(_norm_kernel, nct=nct, first=first),
        grid=(bsz, L // tl - first),
        in_specs=z_specs + [pl.BlockSpec((1, 1, 1, D), seg), pl.BlockSpec((1, 1, 1, D), seg)],
        out_specs=pl.BlockSpec((1, tl, D), lambda b, i: (b, i, 0)),
        out_shape=jax.ShapeDtypeStruct((bsz, L - first * tl, D), out_dtype),
        compiler_params=_cparams(("parallel", "parallel")),
        name="rmsnorm_modulate",
    )(*z_args, scale, shift)


def _inproj_kernel(start_ref, a_ref, wt_ref, o_ref):
    o_ref[...] = _dot_nt(a_ref[...], wt_ref[0].astype(BF16))


def _in_projection(a, w_t, layer, tm=2176):
    m, k = a.shape
    tm = math.gcd(m, tm)
    assert all(s % SUBLANES == 0 for s in PROJ_SRC)
    start = np.array(PROJ_SRC, np.int32) // SUBLANES
    grid_spec = pltpu.PrefetchScalarGridSpec(
        num_scalar_prefetch=1,
        grid=(m // tm, len(PROJ_SRC)),
        in_specs=[pl.BlockSpec((tm, k), lambda i, j, st: (i, 0)),
                  pl.BlockSpec((pl.Element(1), pl.Element(PROJ_TN), pl.Element(k)),
                               lambda i, j, st: (layer, st[j] * SUBLANES, 0))],
        out_specs=pl.BlockSpec((tm, PROJ_TN), lambda i, j, st: (i, j)),
    )
    return pl.pallas_call(
        _inproj_kernel,
        grid_spec=grid_spec,
        out_shape=jax.ShapeDtypeStruct((m, N_PROJ), F32),
        compiler_params=_cparams(("parallel", "arbitrary")),
        name="in_projection",
    )(jnp.asarray(start), a, w_t)


def _rwprep_kernel(r_ref, rp_ref, rn_ref, k_ref, kp_ref, kn_ref, v_ref, vp_ref, vn_ref,
                   w_ref, wp_ref, wn_ref, mur_ref, muk_ref, muv_ref, muw_ref,
                   wup_ref, w0_ref, aup_ref, a0_ref, kk_ref, ka_ref, rk_ref, gat_ref, sct_ref, tri_ref,
                   at_ref, rt_ref, bt_ref, kt_ref, vo_ref, bonus_ref, g_ref, *, tl, n_ctx, L):
    i = pl.program_id(1)
    start = i * tl
    has_prev = jnp.logical_and(start != 0, start != n_ctx).astype(F32)
    has_next = jnp.logical_and(start + tl != n_ctx, start + tl != L).astype(F32)

    def shift(x_ref, p_ref, n_ref, mu_ref):
        x = x_ref[0]
        row = lax.broadcasted_iota(jnp.int32, x.shape, 0)
        prev = jnp.where(row == 0, p_ref[0, 7:8, :] * has_prev, pltpu.roll(x, 1, axis=0))
        nxt = jnp.where(row == tl - 1, n_ref[0, 0:1, :] * has_next, pltpu.roll(x, tl - 1, axis=0))
        mu0, mu1 = mu_ref[0:1, :], mu_ref[1:2, :]
        return x * (1.0 - mu0 - mu1) + mu0 * prev + mu1 * nxt

    r = shift(r_ref, rp_ref, rn_ref, mur_ref)
    k = shift(k_ref, kp_ref, kn_ref, muk_ref)
    v = shift(v_ref, vp_ref, vn_ref, muv_ref)
    wdad = shift(w_ref, wp_ref, wn_ref, muw_ref)
    gat, sct = gat_ref[...], sct_ref[...]

    kk = k * kk_ref[...]
    kk = kk * lax.rsqrt(jnp.maximum(_seg_reduce(kk * kk, gat, sct), 1e-24))
    bonus_ref[0] = _seg_reduce(r * k * rk_ref[...], gat, sct) * v
    vo_ref[0] = v.astype(vo_ref.dtype)

    for d in range(2):
        wd = wdad[:, d * RW_LORA:(d + 1) * RW_LORA]
        ad = wdad[:, 2 * RW_LORA + d * RW_LORA:2 * RW_LORA + (d + 1) * RW_LORA]
        u = w0_ref[d:d + 1, :] + _mmx(jnp.tanh(wd), wup_ref[d], "x3")
        logw = -math.exp(-0.5) * _sigmoid(u)
        a = _sigmoid(a0_ref[d:d + 1, :] + _mmx(ad, aup_ref[d], "x3"))
        kd = k * (1.0 + (a - 1.0) * ka_ref[...])
        lg = _dot_const_lhs(tri_ref[d], logw)
        e_pos = jnp.exp(lg)
        e_neg = 1.0 / e_pos
        at_ref[0, d] = (-kk * jnp.exp(lg - logw)).astype(at_ref.dtype)
        rt_ref[0, d] = (r * e_pos).astype(rt_ref.dtype)
        bt_ref[0, d] = (kk * a * e_neg).astype(bt_ref.dtype)
        kt_ref[0, d] = (kd * e_neg).astype(kt_ref.dtype)
        for c in range(tl // RW_T):
            g_ref[0, d, c] = jnp.exp(jnp.sum(logw[c * RW_T:(c + 1) * RW_T], axis=0, keepdims=True))


def _rw_prepare(proj, p, n_ctx, tl=256):
    bsz, L, _ = proj.shape
    h8 = tl // 8
    nblk8 = L // 8
    main = lambda w, cb: pl.BlockSpec((1, tl, w), lambda b, i: (b, i, cb))
    prev = lambda w, cb: pl.BlockSpec((1, 8, w), lambda b, i: (b, jnp.maximum(i * h8 - 1, 0), cb))
    nxt = lambda w, cb: pl.BlockSpec((1, 8, w), lambda b, i: (b, jnp.minimum((i + 1) * h8, nblk8 - 1), cb))
    full = lambda shape: pl.BlockSpec(shape, lambda b, i: (0,) * len(shape))
    in_specs = []
    args = []
    for name, w in (("r", 1024), ("k", 1024), ("v", 1024), ("wdad", 256)):
        cb = COL[name] // w
        in_specs += [main(w, cb), prev(w, cb), nxt(w, cb)]
        args += [proj, proj, proj]
    mu = p["shift_mu"]
    in_specs += [pl.BlockSpec((2, 1024), lambda b, i: (0, 0)), pl.BlockSpec((2, 1024), lambda b, i: (0, 1)),
                 pl.BlockSpec((2, 1024), lambda b, i: (0, 2)), pl.BlockSpec((2, 256), lambda b, i: (0, 12))]
    args += [mu, mu, mu, mu]
    in_specs += [full((2, RW_LORA, BR)), full((2, BR)), full((2, RW_LORA, BR)), full((2, BR)),
                 full((1, BR)), full((1, BR)), full((1, BR)), full((BR, 128)), full((128, BR)), full((2, tl, tl))]
    args += [p["rw_w_up"], p["rw_w0"], p["rw_a_up"], p["rw_a0"], p["rw_k_k"].reshape(1, BR),
             p["rw_k_a"].reshape(1, BR), p["rw_r_k"].reshape(1, BR), p["sum64"], p["bcast64"], p["tri"]]
    dir_spec = pl.BlockSpec((1, 2, tl, BR), lambda b, i: (b, 0, i, 0))
    tok_spec = pl.BlockSpec((1, tl, BR), lambda b, i: (b, i, 0))
    nchunk = tl // RW_T
    out_specs = [dir_spec, dir_spec, dir_spec, dir_spec, tok_spec, tok_spec,
                 pl.BlockSpec((1, 2, nchunk, 1, BR), lambda b, i: (b, 0, i, 0, 0))]
    dir_shape = jax.ShapeDtypeStruct((bsz, 2, L, BR), BF16)
    out_shape = [dir_shape, dir_shape, dir_shape, dir_shape,
                 jax.ShapeDtypeStruct((bsz, L, BR), BF16), jax.ShapeDtypeStruct((bsz, L, BR), F32),
                 jax.ShapeDtypeStruct((bsz, 2, L // RW_T, 1, BR), F32)]
    return pl.pallas_call(
        functools.partial(_rwprep_kernel, tl=tl, n_ctx=n_ctx, L=L),
        grid=(bsz, L // tl),
        in_specs=in_specs, out_specs=out_specs, out_shape=out_shape,
        compiler_params=_cparams(("parallel", "parallel")),
        name="rwkv_prepare",
    )(*args)


def _scan_chunk(d, j, nc_ctx, nc):
    bwd = jnp.where(j < nc_ctx, nc_ctx - 1 - j, nc - 1 - j + nc_ctx)
    return jnp.where(d == 0, j, bwd)


def _mmx(a, b, mode, dims="nn"):
    f = {"nn": _dot, "nt": _dot_nt, "tn": _dot_tn}[dims]
    ah, bh = a.astype(BF16), b.astype(BF16)
    if mode == "bf16":
        return f(ah, bh)
    al = (a - ah.astype(F32)).astype(BF16)
    bl = (b - bh.astype(F32)).astype(BF16)
    return f(ah, bh) + f(ah, bl) + f(al, bh)


def _rwscan_kernel(atf_ref, rtf_ref, btf_ref, ktf_ref, vf_ref, gf_ref,
                   atb_ref, rtb_ref, btb_ref, ktb_ref, vb_ref, gb_ref, yf_ref, yb_ref, s_ref):
    j = pl.program_id(1)
    T = RW_T

    @pl.when(j == 0)
    def _():
        s_ref[...] = jnp.zeros_like(s_ref)

    tok = lax.broadcasted_iota(jnp.int32, (T, 128), 0)
    col = lax.broadcasted_iota(jnp.int32, (T, 128), 1) % RW_N
    strict = [col < tok, col > tok]
    incl = [col <= tok, col >= tok]
    eye = jnp.where(col == tok, 1.0, 0.0)
    base_blk = (tok // RW_INV_BASE) == (col // RW_INV_BASE)
    merge_blks = []
    s = RW_INV_BASE
    while s < T:
        merge_blks.append(jnp.logical_and((tok // (2 * s)) == (col // (2 * s)), (tok // s) != (col // s)))
        s *= 2
    row2 = lax.broadcasted_iota(jnp.int32, (2 * T, 128), 0)
    lane2 = lax.broadcasted_iota(jnp.int32, (2 * T, 128), 1)
    same_head = (lane2 // RW_N) == (row2 // T)

    def stack(x):
        x = x.astype(BF16)
        return jnp.where(same_head, jnp.concatenate([x, x], axis=0), 0.0)

    cat = lambda xs, axis=0: jnp.concatenate(xs, axis=axis)
    mm = lambda a, b: _dot(a.astype(BF16), b)

    refs = ((atf_ref, rtf_ref, btf_ref, ktf_ref, vf_ref, gf_ref, yf_ref),
            (atb_ref, rtb_ref, btb_ref, ktb_ref, vb_ref, gb_ref, yb_ref))
    chains = [(d, p) for d in range(2) for p in range(BR // 128)]
    dirs = [d for d, _ in chains]
    sls = [slice(p * 128, (p + 1) * 128) for _, p in chains]
    each = lambda fn, *lists: [fn(*xs) for xs in zip(*lists)]
    At = [refs[d][0][0, 0, :, sl] for d, sl in zip(dirs, sls)]
    Rt = [refs[d][1][0, 0, :, sl] for d, sl in zip(dirs, sls)]
    Bt = [refs[d][2][0, 0, :, sl] for d, sl in zip(dirs, sls)]
    Kt = [refs[d][3][0, 0, :, sl] for d, sl in zip(dirs, sls)]
    V = [refs[d][4][0, :, sl] for d, sl in zip(dirs, sls)]
    AR = each(lambda a, r: cat([a, r]), At, Rt)
    BKs = each(lambda b, k: cat([stack(b), stack(k)]), Bt, Kt)
    Vs = [stack(v) for v in V]
    sc = each(_dot_nt, AR, BKs)
    Aab = [jnp.where(strict[d], x[:T, :128], 0.0) for d, x in zip(dirs, sc)]
    Aak = [jnp.where(strict[d], x[:T, 128:], 0.0) for d, x in zip(dirs, sc)]
    Mrbk = [cat([jnp.where(incl[d], x[T:, :128], 0.0), jnp.where(incl[d], x[T:, 128:], 0.0)], axis=1)
            for d, x in zip(dirs, sc)]

    N = [jnp.where(base_blk, x, 0.0) for x in Aab]
    P = [eye + x for x in N]
    N = each(lambda n: mm(n, stack(n)), N)
    NP = each(lambda n, q: mm(cat([n, q]), stack(n)), N, P)
    P = each(lambda q, x: q + x[T:], P, NP)
    P = each(lambda q, x: q + mm(q, stack(x[:T])), P, NP)
    for off_blk in merge_blks:
        CP = each(lambda a, q: mm(jnp.where(off_blk, a, 0.0), stack(q)), Aab, P)
        P = each(lambda q, x: q + mm(q, stack(x)), P, CP)

    AV = each(mm, Aak, Vs)
    S = [s_ref[d, p] for d, p in chains]
    XS = each(lambda a, s_: _dot_nt(a, s_.astype(BF16)), AR, S)
    U = each(lambda q, x, w: mm(q, stack(x[:T] + w)).astype(BF16), P, XS, AV)
    Y = each(lambda x, m, u, v: x[T:] + mm(m, cat([stack(u), v])), XS, Mrbk, U, Vs)
    dS = each(lambda u, v, b, k: _dot_tn(cat([u, v]), cat([b, k])), U, V, Bt, Kt)
    same_head_sq = (lane2 // RW_N) == (row2 // RW_N)
    for c, (d, p) in enumerate(chains):
        refs[d][6][0, :, sls[c]] = Y[c]
        s_ref[d, p] = (S[c] + jnp.where(same_head_sq, dS[c], 0.0)) * refs[d][5][0, 0, 0, :, sls[c]]


def _rw_scan(at, rt, bt, kt, v, g, n_ctx):
    bsz, _, L, _ = at.shape
    nc, nc_ctx = L // RW_T, n_ctx // RW_T

    def specs(d):
        ch = lambda j: _scan_chunk(d, j, nc_ctx, nc)
        dspec = pl.BlockSpec((1, 1, RW_T, BR), lambda b, j: (b, d, ch(j), 0))
        tspec = pl.BlockSpec((1, RW_T, BR), lambda b, j: (b, ch(j), 0))
        return tspec, [dspec, dspec, dspec, dspec, tspec,
                       pl.BlockSpec((1, 1, 1, 1, BR), lambda b, j: (b, d, ch(j), 0, 0))]

    (out_f, in_f), (out_b, in_b) = specs(0), specs(1)
    shape = jax.ShapeDtypeStruct((bsz, L, BR), F32)
    return pl.pallas_call(
        _rwscan_kernel,
        grid=(bsz, nc),
        in_specs=in_f + in_b,
        out_specs=[out_f, out_b],
        out_shape=[shape, shape],
        scratch_shapes=[pltpu.VMEM((2, BR // 128, 128, 128), F32)],
        compiler_params=_cparams(("parallel", "arbitrary")),
        name="rwkv_scan",
    )(at, rt, bt, kt, v, g, at, rt, bt, kt, v, g)


def _atprep_kernel(q_ref, k_ref, v_ref, cos_ref, sin_ref, qg_ref, kg_ref, qo_ref, ko_ref, vo_ref):
    cos = cos_ref[...]
    sin = sin_ref[...]
    lane = lax.broadcasted_iota(jnp.int32, cos.shape, 1)
    first_half = (lane % 64) < 32

    q_scale = AT_HD ** -0.5 * math.log2(math.e)
    heads = [(q_ref, qo_ref, h, qg_ref[...] * q_scale) for h in range(AT_H)]
    heads += [(k_ref, ko_ref, h, kg_ref[...]) for h in range(AT_KV)]
    x = [src[0, :, h * AT_HD:(h + 1) * AT_HD] for src, _, h, _ in heads]
    y = [xi * lax.rsqrt(jnp.mean(xi * xi, axis=-1, keepdims=True) + EPS) * g for xi, (_, _, _, g) in zip(x, heads)]
    up = [pltpu.roll(yi, 96, axis=1) for yi in y]
    down = [pltpu.roll(yi, 32, axis=1) for yi in y]
    for (_, dst, h, _), yi, u, dn in zip(heads, y, up, down):
        dst[0, :, h * AT_HD:(h + 1) * AT_HD] = (yi * cos + jnp.where(first_half, u, dn) * sin).astype(dst.dtype)
    vo_ref[0] = v_ref[0].astype(vo_ref.dtype)


def _at_prepare(proj, cos, sin, q_g, k_g, tl=256):
    bsz, L, _ = proj.shape
    kvw = AT_KV * AT_HD
    tok = lambda w: pl.BlockSpec((1, tl, w), lambda b, i: (b, i, 0))
    return pl.pallas_call(
        _atprep_kernel,
        grid=(bsz, L // tl),
        in_specs=[pl.BlockSpec((1, tl, BR), lambda b, i: (b, i, COL["at_q"] // BR)),
                  pl.BlockSpec((1, tl, kvw), lambda b, i: (b, i, COL["at_k"] // kvw)),
                  pl.BlockSpec((1, tl, kvw), lambda b, i: (b, i, COL["at_v"] // kvw)),
                  pl.BlockSpec((tl, AT_HD), lambda b, i: (i, 0)),
                  pl.BlockSpec((tl, AT_HD), lambda b, i: (i, 0)),
                  pl.BlockSpec((1, AT_HD), lambda b, i: (0, 0)),
                  pl.BlockSpec((1, AT_HD), lambda b, i: (0, 0))],
        out_specs=[tok(BR), tok(kvw), tok(kvw)],
        out_shape=[jax.ShapeDtypeStruct((bsz, L, BR), BF16),
                   jax.ShapeDtypeStruct((bsz, L, kvw), BF16),
                   jax.ShapeDtypeStruct((bsz, L, kvw), BF16)],
        compiler_params=_cparams(("parallel", "parallel")),
        name="gqa_prepare",
    )(proj, proj, proj, cos, sin, q_g.reshape(1, AT_HD), k_g.reshape(1, AT_HD))


ATTN_STAGES = 3


def _attn_kernel(q_ref, k_ref, v_ref, gate_ref, o_ref, s_ref, p_ref, l_ref, *, tq, n_ctx, kb, rows):
    i = pl.program_id(1)
    grp = AT_H // AT_KV

    def attend(n_keys):
        chunks = [slice(c * kb, (c + 1) * kb) for c in range(n_keys // kb)]
        head = lambda r: slice(r * AT_HD, (r + 1) * AT_HD)
        kv_head = lambda r: slice((r // grp) * AT_HD, (r // grp + 1) * AT_HD)

        def scores(r):
            s_ref[r % ATTN_STAGES, :, :n_keys] = _dot_nt(q_ref[0, :, head(r)], k_ref[0, :n_keys, kv_head(r)])

        def softmax(r):
            slot = r % ATTN_STAGES
            for rb in range(tq // rows):
                rs = slice(rb * rows, (rb + 1) * rows)
                mx = s_ref[slot, rs, chunks[0]]
                for ch in chunks[1:]:
                    mx = jnp.maximum(mx, s_ref[slot, rs, ch])
                m = jnp.broadcast_to(jnp.max(mx, axis=-1, keepdims=True), mx.shape)
                tot = jnp.zeros_like(mx)
                for ch in chunks:
                    p = jnp.exp2(s_ref[slot, rs, ch] - m)
                    tot = tot + p
                    p_ref[slot, rs, ch] = p.astype(BF16)
                l_ref[slot, rs, :] = jnp.broadcast_to(jnp.sum(tot, axis=-1, keepdims=True), (rows, AT_HD))

        def values(r):
            slot = r % ATTN_STAGES
            att = _dot(p_ref[slot, :, :n_keys], v_ref[0, :n_keys, kv_head(r)]) / l_ref[slot]
            o_ref[0, :, head(r)] = (att * _silu(gate_ref[0, :, head(r)])).astype(o_ref.dtype)

        stages = (scores, softmax, values)
        assert len(stages) == ATTN_STAGES
        for t in range(AT_H + ATTN_STAGES - 1):
            for st, fn in enumerate(stages):
                if 0 <= t - st < AT_H:
                    fn(t - st)

    @pl.when(i * tq < n_ctx)
    def _():
        attend(n_ctx)

    @pl.when(i * tq >= n_ctx)
    def _():
        attend(k_ref.shape[1])


def _attention(q, k, v, proj, n_ctx, tq=256, kb=256, rows=32):
    bsz, L, _ = q.shape
    kvw = AT_KV * AT_HD
    tok = pl.BlockSpec((1, tq, BR), lambda b, i: (b, i, 0))
    return pl.pallas_call(
        functools.partial(_attn_kernel, tq=tq, n_ctx=n_ctx, kb=kb, rows=rows),
        scratch_shapes=[pltpu.VMEM((ATTN_STAGES, tq, L), F32), pltpu.VMEM((ATTN_STAGES, tq, L), BF16),
                        pltpu.VMEM((ATTN_STAGES, tq, AT_HD), F32)],
        grid=(bsz, L // tq),
        in_specs=[tok,
                  pl.BlockSpec((1, L, kvw), lambda b, i: (b, 0, 0)),
                  pl.BlockSpec((1, L, kvw), lambda b, i: (b, 0, 0)),
                  pl.BlockSpec((1, tq, BR), lambda b, i: (b, i, COL["at_g"] // BR))],
        out_specs=tok,
        out_shape=jax.ShapeDtypeStruct((bsz, L, BR), BF16),
        compiler_params=_cparams(("parallel", "parallel")),
        name="gqa_attention",
    )(q, k, v, proj)


def _cap_gates(pre):
    return GATE_CAP * jnp.tanh(pre / GATE_CAP)


def _log_sigmoid(x):
    return jnp.minimum(x, 0.0) - jnp.log1p(jnp.exp(-jnp.abs(x)))


def _mlstm_kernel(qf_ref, kf_ref, vf_ref, gcf_ref, qb_ref, kb_ref, vb_ref, gcb_ref,
                  bc_ref, br_ref, hf_ref, hb_ref, c_ref, n_ref, m_ref):
    j = pl.program_id(1)
    T = ML_T

    @pl.when(j == 0)
    def _():
        c_ref[...] = jnp.zeros_like(c_ref)
        n_ref[...] = jnp.zeros_like(n_ref)
        m_ref[...] = jnp.zeros_like(m_ref)

    r2 = lax.broadcasted_iota(jnp.int32, (T, T), 0)
    c2 = lax.broadcasted_iota(jnp.int32, (T, T), 1)

    refs = ((qf_ref, kf_ref, vf_ref, gcf_ref, None, hf_ref), (qb_ref, kb_ref, vb_ref, gcb_ref, None, hb_ref))
    seen = [c2 <= r2, c2 >= r2]
    seen_t = [r2 <= c2, r2 >= c2]
    gc = [_cap_gates(refs[d][3][0][:, :16] + bc_ref[...]) for d in range(2)]
    gr = [_cap_gates(refs[d][3][0].T[:16, :] + br_ref[...]) for d in range(2)]
    lsc = [_log_sigmoid(x) for x in gc]
    lsr = [_log_sigmoid(x) for x in gr]

    chains = [(d, h) for d in range(2) for h in range(ML_H)]
    each = lambda fn, *lists: [fn(*xs) for xs in zip(*lists)]
    lane16 = lax.broadcasted_iota(jnp.int32, (T, 16), 1)
    sub16 = lax.broadcasted_iota(jnp.int32, (16, T), 0)
    pick_row = lambda x, idx: jnp.sum(jnp.where(sub16 == idx, x, 0.0), axis=0, keepdims=True)
    pick_col = lambda x, idx: jnp.sum(jnp.where(lane16 == idx, x, 0.0), axis=1, keepdims=True)
    li_row = [pick_row(gr[d], d * ML_H + h) for d, h in chains]
    lf_row = [pick_row(lsr[d], (2 + d) * ML_H + h) for d, h in chains]
    li_col = [pick_col(gc[d], d * ML_H + h) for d, h in chains]
    lf_col = [pick_col(lsc[d], (2 + d) * ML_H + h) for d, h in chains]
    b_col = [jnp.sum(jnp.where(seen[d], x, 0.0), axis=1, keepdims=True) for (d, _), x in zip(chains, lf_row)]
    b_row = [jnp.sum(jnp.where(seen_t[d], x, 0.0), axis=0, keepdims=True) for (d, _), x in zip(chains, lf_col)]
    g = [jnp.sum(x, axis=0, keepdims=True) for x in lf_col]
    m_prev = [m_ref[d, h] for d, h in chains]
    q = [refs[d][0][0, :, h * ML_DK:(h + 1) * ML_DK] * (ML_DK ** -0.5) for d, h in chains]
    k = [refs[d][1][0, :, h * ML_DK:(h + 1) * ML_DK] for d, h in chains]
    vb = [refs[d][2][0, :, h * ML_DV:(h + 1) * ML_DV].astype(BF16) for d, h in chains]
    C = [c_ref[d, h] for d, h in chains]
    n = [n_ref[d, h] for d, h in chains]
    qb = [x.astype(BF16) for x in q]

    dmat = [jnp.where(seen[d], bc - br + li, -jnp.inf) for (d, _), bc, br, li in zip(chains, b_col, b_row, li_row)]
    m_inter = each(lambda bc, m: bc + m, b_col, m_prev)
    m_t = each(lambda mi, dm: jnp.maximum(mi, jnp.max(dm, axis=-1, keepdims=True)), m_inter, dmat)
    w_inter = each(lambda mi, mt: jnp.exp(mi - mt), m_inter, m_t)
    qk = each(lambda a, b: _dot_nt(a, b.astype(BF16)), qb, k)
    qc = each(lambda a, b: _dot(a, b.astype(BF16)), qb, C)
    s = each(lambda x, dm, mt: x * jnp.exp(dm - mt), qk, dmat, m_t)
    sv = each(lambda a, b: _dot(a.astype(BF16), b), s, vb)
    qn = each(lambda a, b: jnp.sum(a * b, axis=-1, keepdims=True), q, n)
    den = each(lambda w, a, x: w * a + jnp.sum(x, axis=-1, keepdims=True), w_inter, qn, s)
    for (d, h), w, a, b, dn, mt in zip(chains, w_inter, qc, sv, den, m_t):
        refs[d][5][0, :, h * ML_DV:(h + 1) * ML_DV] = (w * a + b) / jnp.maximum(jnp.abs(dn), jnp.exp(-mt))

    loga = each(lambda g_, bc, li: g_ - bc + li, g, b_col, li_col)
    m_new = each(lambda g_, m, la: jnp.maximum(g_ + m, jnp.max(la, axis=0, keepdims=True)), g, m_prev, loga)
    carry = each(lambda g_, m, mn: jnp.exp(g_ + m - mn), g, m_prev, m_new)
    wk = each(lambda la, mn, k_: jnp.exp(la - mn) * k_, loga, m_new, k)
    kv = each(lambda a, b: _dot_tn(a.astype(BF16), b), wk, vb)
    for (d, h), cr, c_, kv_, n_, wk_, mn in zip(chains, carry, C, kv, n, wk, m_new):
        c_ref[d, h] = cr * c_ + kv_
        n_ref[d, h] = cr * n_ + jnp.sum(wk_, axis=0, keepdims=True)
        m_ref[d, h] = mn


def _mlstm(proj, bias_col, bias_row, n_ctx):
    bsz, L, _ = proj.shape
    nc, nc_ctx = L // ML_T, n_ctx // ML_T
    qw, vw = ML_H * ML_DK, ML_H * ML_DV

    def dir_specs(d):
        ch = lambda j: _scan_chunk(d, j, nc_ctx, nc)
        return [pl.BlockSpec((1, ML_T, qw), lambda b, j: (b, ch(j), COL["ml_q"] // qw)),
                pl.BlockSpec((1, ML_T, qw), lambda b, j: (b, ch(j), COL["ml_k"] // qw)),
                pl.BlockSpec((1, ML_T, vw), lambda b, j: (b, ch(j), COL["ml_v"] // vw)),
                pl.BlockSpec((1, ML_T, 128), lambda b, j: (b, ch(j), COL["ml_if"] // 128))]

    def out_spec(d):
        ch = lambda j: _scan_chunk(d, j, nc_ctx, nc)
        return pl.BlockSpec((1, ML_T, vw), lambda b, j: (b, ch(j), 0))

    shape = jax.ShapeDtypeStruct((bsz, L, vw), F32)
    return pl.pallas_call(
        _mlstm_kernel,
        grid=(bsz, nc),
        in_specs=dir_specs(0) + dir_specs(1) + [pl.BlockSpec((1, 16), lambda b, j: (0, 0)),
                                                pl.BlockSpec((16, 1), lambda b, j: (0, 0))],
        out_specs=[out_spec(0), out_spec(1)],
        out_shape=[shape, shape],
        scratch_shapes=[pltpu.VMEM((2, ML_H, ML_DK, ML_DV), F32),
                        pltpu.VMEM((2, ML_H, 1, ML_DK), F32),
                        pltpu.VMEM((2, ML_H, 1, 1), F32)],
        compiler_params=_cparams(("parallel", "arbitrary")),
        name="mlstm_scan",
    )(proj, proj, proj, proj, proj, proj, proj, proj, bias_col, bias_row)


def _epilogue_kernel(yf_ref, yb_ref, bonus_ref, rwg_ref, hf_ref, hb_ref, mlo_ref, mlg_ref,
                     lnw_ref, lnb_ref, mng_ref, m64_ref, b64_ref, m256_ref, b256_ref, o_ref):
    y = yf_ref[0] + yb_ref[0]
    mu = _seg_reduce(y, m64_ref[...], b64_ref[...])
    yc = y - mu
    var = _seg_reduce(yc * yc, m64_ref[...], b64_ref[...])
    ya = yc * lax.rsqrt(var + RW_GN_EPS) * lnw_ref[...] + lnb_ref[...] + bonus_ref[0]
    o_ref[0, 0] = (ya * _silu(rwg_ref[0])).astype(o_ref.dtype)

    hh = hf_ref[0] + hb_ref[0]
    ms = _seg_reduce(hh * hh, m256_ref[...], b256_ref[...])
    hn = hh * lax.rsqrt(ms + EPS) * mng_ref[...]
    o_ref[1, 0] = (_sigmoid(mlo_ref[0]) * hn * _silu(mlg_ref[0])).astype(o_ref.dtype)


def _epilogue(proj, y_f, y_b, bonus, h_f, h_b, p, tl=256):
    bsz, L, _ = proj.shape
    tok = pl.BlockSpec((1, tl, BR), lambda b, i: (b, i, 0))
    pc = lambda name: pl.BlockSpec((1, tl, BR), lambda b, i: (b, i, COL[name] // BR))
    vec = pl.BlockSpec((1, BR), lambda b, i: (0, 0))
    gat = pl.BlockSpec((BR, 128), lambda b, i: (0, 0))
    sct = pl.BlockSpec((128, BR), lambda b, i: (0, 0))
    return pl.pallas_call(
        _epilogue_kernel,
        grid=(bsz, L // tl),
        in_specs=[tok, tok, tok, pc("rw_g"), tok, tok, pc("ml_o"), pc("ml_g"),
                  vec, vec, vec, gat, sct, gat, sct],
        out_specs=pl.BlockSpec((2, 1, tl, BR), lambda b, i: (0, b, i, 0)),
        out_shape=jax.ShapeDtypeStruct((2, bsz, L, BR), BF16),
        compiler_params=_cparams(("parallel", "parallel")),
        name="branch_epilogue",
    )(y_f, y_b, bonus, proj, h_f, h_b, proj, proj,
      p["rw_ln_w"].reshape(1, BR), p["rw_ln_b"].reshape(1, BR), p["ml_norm_g"].reshape(1, BR),
      p["mean64"], p["bcast64"], p["mean256"], p["bcast256"])


def _merge_kernel(y_ref, yatt_ref, w_ref, g0_ref, g1_ref, g2_ref, o_ref, wb_ref):
    @pl.when(pl.program_id(1) == 0)
    def _():
        wb_ref[...] = w_ref[0].astype(BF16)

    acc = _sigmoid(g0_ref[...]) * _dot(y_ref[0], wb_ref[0])
    acc += _sigmoid(g1_ref[...]) * _dot(yatt_ref[...], wb_ref[1])
    acc += _sigmoid(g2_ref[...]) * _dot(y_ref[1], wb_ref[2])
    o_ref[...] = acc.astype(o_ref.dtype)


def _merge(ys, y_att, w_branch, layer, proj2d, tm=512, tn=1024):
    _, m, _ = ys.shape
    nb = D // tn
    gate = lambda n: pl.BlockSpec((tm, tn), lambda j, i: (i, n * nb + j))
    return pl.pallas_call(
        _merge_kernel,
        grid=(D // tn, m // tm),
        in_specs=[pl.BlockSpec((2, tm, BR), lambda j, i: (0, i, 0)),
                  pl.BlockSpec((tm, BR), lambda j, i: (i, 0)),
                  pl.BlockSpec((1, 3, BR, tn), lambda j, i: (layer, 0, 0, j), pipeline_mode=pl.Buffered(1)),
                  gate(0), gate(1), gate(2)],
        out_specs=pl.BlockSpec((tm, tn), lambda j, i: (i, j)),
        out_shape=jax.ShapeDtypeStruct((m, D), BF16),
        scratch_shapes=[pltpu.VMEM((3, BR, tn), BF16)],
        compiler_params=_cparams(("arbitrary", "arbitrary")),
        name="branch_merge",
    )(ys, y_att, w_branch, proj2d, proj2d, proj2d)


def _outproj_kernel(*refs, nct, n_z, last):
    a_ref, w_ref, gt_ref, sc_ref, sh_ref = refs[:5]
    z_refs, out_refs, wb_ref = refs[5:5 + n_z], refs[5 + n_z:-1], refs[-1]

    @pl.when(jnp.logical_and(pl.program_id(0) == 0, pl.program_id(1) == 0))
    def _():
        wb_ref[...] = w_ref[0].astype(BF16)

    def body(z):
        z_new = z + gt_ref[0, 0] * _dot(a_ref[0], wb_ref[...])
        y = z_new * lax.rsqrt(jnp.mean(z_new * z_new, axis=-1, keepdims=True) + EPS)
        normed = y * sc_ref[0, 0] + sh_ref[0, 0]
        if last:
            out_refs[0][0] = normed
        else:
            out_refs[0][0] = z_new
            out_refs[1][0] = normed.astype(out_refs[1].dtype)

    _for_stream_tile(z_refs, pl.program_id(1) < nct, body)


def _out_projection(mixed, w_out, layer, z, gate, scale, shift, n_ctx, last, tl=256):
    bsz, L, _ = mixed.shape
    nct = n_ctx // tl
    z_specs, z_args = _stream_specs(z, tl, nct, (1, tl, D), lambda b, i: (b, i, 0))
    seg = lambda b, i: (b, (i >= nct).astype(jnp.int32), 0, 0)
    tok = pl.BlockSpec((1, tl, D), lambda b, i: (b, i, 0))
    if last:
        out_specs = [pl.BlockSpec((1, tl, D), lambda b, i: (b, jnp.maximum(i - nct, 0), 0))]
        out_shape = [jax.ShapeDtypeStruct((bsz, L - n_ctx, D), F32)]
    else:
        out_specs = [tok, tok]
        out_shape = [jax.ShapeDtypeStruct((bsz, L, D), F32), jax.ShapeDtypeStruct((bsz, L, D), BF16)]
    return pl.pallas_call(
        functools.partial(_outproj_kernel, nct=nct, n_z=len(z_args), last=last),
        grid=(bsz, L // tl),
        in_specs=[tok,
                  pl.BlockSpec((1, D, D), lambda b, i: (layer, 0, 0), pipeline_mode=pl.Buffered(1)),
                  pl.BlockSpec((1, 1, 1, D), seg), pl.BlockSpec((1, 1, 1, D), seg),
                  pl.BlockSpec((1, 1, 1, D), seg)] + z_specs,
        out_specs=out_specs,
        out_shape=out_shape,
        scratch_shapes=[pltpu.VMEM((D, D), BF16)],
        compiler_params=_cparams(("arbitrary", "arbitrary")),
        name="out_projection",
    )(mixed, w_out, gate, scale, shift, *z_args)


def _rope_tables(n_ctx, n_lat):
    rows = n_lat // GRID_W
    row = jnp.repeat(jnp.arange(rows), GRID_W).astype(F32)
    col = jnp.tile(jnp.arange(GRID_W), rows).astype(F32)
    inv_freq = ROPE_THETA ** (-jnp.arange(0, AT_HD // 2, 2, dtype=F32) / (AT_HD // 2))
    ang_lat = jnp.stack([row[:, None] * inv_freq, col[:, None] * inv_freq], axis=1)
    ang = jnp.concatenate([jnp.zeros((n_ctx, 2, AT_HD // 4), F32), ang_lat], axis=0)
    cos, sin = jnp.cos(ang), jnp.sin(ang)
    cos_t = jnp.concatenate([cos[:, 0], cos[:, 0], cos[:, 1], cos[:, 1]], axis=-1)
    sin_t = jnp.concatenate([-sin[:, 0], sin[:, 0], -sin[:, 1], sin[:, 1]], axis=-1)
    return cos_t, sin_t


def _group_consts(width, value):
    member = (np.arange(BR)[:, None] // width) == np.arange(128)[None, :]
    return (jnp.asarray(np.where(member, value, 0.0), dtype=BF16),
            jnp.asarray(np.where(member.T, 1.0, 0.0), dtype=BF16))


def _chunk_tri(tl):
    t = np.arange(tl)
    same = (t[:, None] // RW_T) == (t[None, :] // RW_T)
    fwd = same & (t[None, :] <= t[:, None])
    bwd = same & (t[None, :] >= t[:, None])
    return jnp.asarray(np.stack([fwd, bwd]).astype(np.float32), dtype=BF16)


def kernel(x, c, ctx, c_ctx, norm_g, w_ada, b_ada, w_in, shift_mu, rw_w_up, rw_w0, rw_a_up, rw_a0, rw_k_k, rw_k_a, rw_r_k, rw_ln_w, rw_ln_b, at_q_g, at_k_g, ml_gate_b, ml_norm_g, w_branch, w_out, final_g):
    bsz, n_lat, _ = x.shape
    n_ctx = ctx.shape[1]
    L = n_ctx + n_lat
    depth = w_in.shape[0]

    cos_t, sin_t = _rope_tables(n_ctx, n_lat)
    sum64, bcast64 = _group_consts(RW_N, 1.0)
    mean64, _ = _group_consts(RW_N, 1.0 / RW_N)
    mean256, bcast256 = _group_consts(ML_DV, 1.0 / ML_DV)
    consts = {"sum64": sum64, "bcast64": bcast64, "mean64": mean64, "mean256": mean256, "bcast256": bcast256,
              "tri": _chunk_tri(256)}

    cc = jnp.concatenate([c, c_ctx[None], jnp.zeros((8 - bsz - 1, D), F32)], axis=0)
    mod = _modulation(cc, w_ada, b_ada)

    w_in_t = jnp.swapaxes(w_in, 1, 2)
    pick = lambda t: jnp.stack([jnp.broadcast_to(t[bsz], (bsz, D)), t[:bsz]], axis=1)[:, :, None, :]
    scales = [pick((1.0 + mod[l, :, D:2 * D]) * norm_g[l]) for l in range(depth)]
    shifts = [pick(mod[l, :, :D]) for l in range(depth)]
    gates = [pick(mod[l, :, 2 * D:]) for l in range(depth)]
    scales.append(jnp.broadcast_to(final_g, (bsz, 2, 1, D)))
    shifts.append(jnp.zeros((bsz, 2, 1, D), F32))

    z = (ctx, x)
    h = _norm_mod(z, scales[0], shifts[0], n_ctx, L, BF16)
    for l in range(depth):
        proj2d = _in_projection(h.reshape(bsz * L, D), w_in_t, l)
        proj = proj2d.reshape(bsz, L, N_PROJ)

        p = dict(consts, shift_mu=shift_mu[l], rw_w_up=rw_w_up[l], rw_w0=rw_w0[l], rw_a_up=rw_a_up[l],
                 rw_a0=rw_a0[l], rw_k_k=rw_k_k[l], rw_k_a=rw_k_a[l], rw_r_k=rw_r_k[l], rw_ln_w=rw_ln_w[l],
                 rw_ln_b=rw_ln_b[l], ml_norm_g=ml_norm_g[l])
        at, rt, bt, kt, v_rw, bonus, g_rw = _rw_prepare(proj, p, n_ctx)
        y_f, y_b = _rw_scan(at, rt, bt, kt, v_rw, g_rw, n_ctx)

        qn, kn, vn = _at_prepare(proj, cos_t, sin_t, at_q_g[l], at_k_g[l])
        y_att = _attention(qn, kn, vn, proj, n_ctx)

        bias = ml_gate_b[l].reshape(16)
        h_f, h_b = _mlstm(proj, bias.reshape(1, 16), bias.reshape(16, 1), n_ctx)

        ys = _epilogue(proj, y_f, y_b, bonus, h_f, h_b, p)
        mixed = _merge(ys.reshape(2, bsz * L, BR), y_att.reshape(bsz * L, BR), w_branch, l, proj2d)
        outs = _out_projection(mixed.reshape(bsz, L, D), w_out, l, z, gates[l], scales[l + 1], shifts[l + 1],
                               n_ctx, last=(l == depth - 1))
        if l == depth - 1:
            return outs[0]
        z, h = outs
```
